```python
import jax
import jax.numpy as jnp
from jax import lax
import numpy as np

D_MODEL = 1024
BATCH = 8
SEQ = 2048
DEPTH = 4

GRID_W = 64
CTX_LEN = 256
HEAD_DIM = 64
N_EVEN = (DEPTH + 1) // 2
N_ODD = DEPTH // 2
EPS = 1e-6
NEG_INF = -1e30

NA_HEADS = (D_MODEL // 2) // HEAD_DIM
NA_WIN_ROWS = 8
NA_WIN_COLS = 16
NA_COL_BLOCK = 16
NA_KEY_COLS = NA_COL_BLOCK + NA_WIN_COLS
GLA_HEADS = 4
GLA_DK = (D_MODEL // 4) // GLA_HEADS
GLA_DV = (D_MODEL // 2) // GLA_HEADS
GLA_RANK = 16
GLA_NORMALIZER = 16.0
GLA_CHUNK = 64
SWA_HEADS = D_MODEL // HEAD_DIM
SWA_KV_HEADS = 4
SWA_GROUP = SWA_HEADS // SWA_KV_HEADS
SWA_WINDOW = 128
SWA_BLOCK = 128
D_FF = 4 * D_MODEL
ROPE_THETA = 10000.0

NA_WIDTH = NA_HEADS * HEAD_DIM
GLA_QK_WIDTH = GLA_HEADS * GLA_DK
GLA_V_WIDTH = GLA_HEADS * GLA_DV
AB_IN = 3 * NA_WIDTH + 2 * GLA_QK_WIDTH + 2 * GLA_V_WIDTH + 2 * GLA_RANK
AB_OUT = NA_WIDTH + GLA_V_WIDTH
SWA_Q_WIDTH = SWA_HEADS * HEAD_DIM
SWA_KV_WIDTH = SWA_KV_HEADS * HEAD_DIM
SWA_IN = SWA_Q_WIDTH + 2 * SWA_KV_WIDTH

kernel_name = 'hybrid_na_gla_swa_dit_prefix'


def _split_at(t, sizes):
    idx, acc = [], 0
    for s in sizes[:-1]:
        acc += s
        idx.append(acc)
    return jnp.split(t, idx, axis=-1)


def rms_norm(x, gain):
    xf = x.astype(jnp.float32)
    y = xf * lax.rsqrt(jnp.mean(xf * xf, axis=-1, keepdims=True) + EPS)
    return (y * gain.astype(jnp.float32)).astype(x.dtype)


def modulate(h, shift, scale):
    return h * (1.0 + scale) + shift


def _heads(t, n_heads):
    b, n, _ = t.shape
    return t.reshape(b, n, n_heads, -1).transpose(0, 2, 1, 3)


def _merge_heads(t):
    b, h, n, d = t.shape
    return t.transpose(0, 2, 1, 3).reshape(b, n, h * d)


def axial_rope_tables(n_tokens):
    t = jnp.arange(n_tokens)
    row = (t // GRID_W).astype(jnp.float32)
    col = (t % GRID_W).astype(jnp.float32)
    n_freq = HEAD_DIM // 4
    inv = ROPE_THETA ** (-jnp.arange(n_freq, dtype=jnp.float32) / n_freq)
    ang = jnp.concatenate([row[:, None] * inv, col[:, None] * inv], axis=-1)
    return jnp.cos(ang), jnp.sin(ang)


def apply_rope(x, cos, sin):
    xf = x.astype(jnp.float32).reshape(x.shape[:-1] + (HEAD_DIM // 2, 2))
    xe, xo = xf[..., 0], xf[..., 1]
    out = jnp.stack([xe * cos - xo * sin, xe * sin + xo * cos], axis=-1)
    return out.reshape(x.shape).astype(x.dtype)


def squared_relu_mlp(h, w1, w2):
    return jnp.square(jax.nn.relu(h @ w1)) @ w2


def neighbourhood_attention(q, k, v, kc, vc, rpb):
    b, h, n, dh = q.shape
    rows = n // GRID_W
    kr = min(NA_WIN_ROWS, rows)
    n_cb = GRID_W // NA_COL_BLOCK
    qg = (q * dh ** -0.5).reshape(b, h, rows, n_cb, NA_COL_BLOCK, dh)
    kg = k.reshape(b, h, rows, GRID_W, dh)
    vg = v.reshape(b, h, rows, GRID_W, dh)
    q_col = np.arange(GRID_W).reshape(n_cb, NA_COL_BLOCK)
    strip0 = np.clip(np.arange(n_cb) * NA_COL_BLOCK - NA_WIN_COLS // 2, 0, GRID_W - NA_KEY_COLS)
    k_col = strip0[:, None] + np.arange(NA_KEY_COLS)
    win0 = np.clip(q_col - NA_WIN_COLS // 2, 0, GRID_W - NA_WIN_COLS)
    col_in = (k_col[:, None, :] >= win0[..., None]) & (k_col[:, None, :] < win0[..., None] + NA_WIN_COLS)
    col_off = k_col[:, None, :] - q_col[..., None] + (NA_WIN_COLS - 1)
    rpb32 = rpb.astype(jnp.float32)
    n_nb = kr * NA_KEY_COLS

    def row_block(r):
        r0 = jnp.clip(r - kr // 2, 0, rows - kr)
        k_strip = lax.dynamic_slice_in_dim(kg, r0, kr, axis=2)[:, :, :, k_col]
        v_strip = lax.dynamic_slice_in_dim(vg, r0, kr, axis=2)[:, :, :, k_col]
        q_r = lax.dynamic_index_in_dim(qg, r, axis=2, keepdims=False)
        row_off = r0 + jnp.arange(kr) - r + (NA_WIN_ROWS - 1)
        bias = rpb32[:, row_off][:, :, col_off].transpose(0, 2, 3, 1, 4)
        s_nb = jnp.einsum('bhnqd,bhrnkd->bhnqrk', q_r, k_strip).astype(jnp.float32) + bias
        s_nb = jnp.where(col_in[:, :, None, :], s_nb, NEG_INF).reshape(b, h, n_cb, NA_COL_BLOCK, n_nb)
        s_ctx = jnp.einsum('bhnqd,bhcd->bhnqc', q_r, kc).astype(jnp.float32)
        p = jax.nn.softmax(jnp.concatenate([s_nb, s_ctx], axis=-1), axis=-1).astype(v.dtype)
        p_nb = p[..., :n_nb].reshape(b, h, n_cb, NA_COL_BLOCK, kr, NA_KEY_COLS)
        return (jnp.einsum('bhnqrk,bhrnkd->bhnqd', p_nb, v_strip)
                + jnp.einsum('bhnqc,bhcd->bhnqd', p[..., n_nb:], vc))

    o = lax.map(row_block, jnp.arange(rows))
    return jnp.moveaxis(o, 0, 2).reshape(b, h, n, dh)


def context_attention(qc, kc, vc):
    s = jnp.einsum('bhqd,bhkd->bhqk', qc * qc.shape[-1] ** -0.5, kc).astype(jnp.float32)
    return jnp.einsum('bhqk,bhkd->bhqd', jax.nn.softmax(s, axis=-1).astype(vc.dtype), vc)


def gla_chunked(q, k, v, log_a, s0):
    b, h, n, dk = q.shape
    dv = v.shape[-1]
    n_chunks = n // GLA_CHUNK

    def chunks(t):
        return jnp.moveaxis(t.reshape(b, h, n_chunks, GLA_CHUNK, t.shape[-1]), 2, 0)

    causal = jnp.tril(jnp.ones((GLA_CHUNK, GLA_CHUNK), dtype=bool))

    def step(s, inp):
        qc, kc, vc, gc = inp
        cum = jnp.cumsum(gc, axis=-2)
        cum_last = cum[..., -1:, :]
        q_t = qc * jnp.exp(cum)
        k_t = kc * jnp.exp(-cum)
        k_end = kc * jnp.exp(cum_last - cum)
        a = jnp.where(causal, jnp.einsum('bhid,bhjd->bhij', q_t, k_t), 0.0)
        o = jnp.einsum('bhij,bhjv->bhiv', a, vc) + jnp.einsum('bhid,bhdv->bhiv', q_t, s)
        s_new = jnp.exp(cum_last[..., 0, :])[..., None] * s + jnp.einsum('bhjd,bhjv->bhdv', k_end, vc)
        return s_new, o

    s_fin, o = lax.scan(step, s0, (chunks(q), chunks(k), chunks(v), chunks(log_a)))
    return jnp.moveaxis(o, 0, 2).reshape(b, h, n, dv), s_fin


def _gla_log_decay(lr, wa2_d, ba_d, d):
    z = jnp.einsum('btr,rk->btk', lr[..., d * GLA_RANK:(d + 1) * GLA_RANK], wa2_d) + ba_d
    return _heads(jax.nn.log_sigmoid(z.astype(jnp.float32)) / GLA_NORMALIZER, GLA_HEADS)


def gla_bidirectional(q, k, v, lr, qc, kc, vc, lrc, wa2, ba, with_ctx_out):
    f32 = jnp.float32
    scale = GLA_DK ** -0.5
    q, k, v = q.astype(f32) * scale, k.astype(f32), v.astype(f32)
    qc, kc, vc = qc.astype(f32) * scale, kc.astype(f32), vc.astype(f32)
    s0 = jnp.zeros((q.shape[0], GLA_HEADS, GLA_DK, GLA_DV), f32)

    def direction(d, reverse):
        orient = (lambda t: jnp.flip(t, axis=2)) if reverse else (lambda t: t)
        a_x = orient(_gla_log_decay(lr, wa2[d], ba[d], d))
        a_c = orient(_gla_log_decay(lrc, wa2[d], ba[d], d))
        o_c, s_c = gla_chunked(orient(qc), orient(kc), orient(vc), a_c, s0)
        o_x, _ = gla_chunked(orient(q), orient(k), orient(v), a_x, s_c)
        return orient(o_x), orient(o_c)

    ox_f, oc_f = direction(0, False)
    ox_b, oc_b = direction(1, True)
    oc = (oc_f + oc_b) if with_ctx_out else None
    return ox_f + ox_b, oc


def gla_output(o, gate, gnorm):
    o = o * lax.rsqrt(jnp.mean(o * o, axis=-1, keepdims=True) + EPS)
    o = _merge_heads(o) * gnorm.astype(jnp.float32)
    return (o * jax.nn.silu(gate.astype(jnp.float32))).astype(gate.dtype)


def even_mixer(hx, hc, w_in, w_out, rpb, wa2, ba, gnorm, with_ctx_out):
    sizes = (NA_WIDTH, NA_WIDTH, NA_WIDTH, GLA_QK_WIDTH, GLA_QK_WIDTH, GLA_V_WIDTH, GLA_V_WIDTH, 2 * GLA_RANK)

    def project(h):
        qa, ka, va, qb, kb, vb, gb, lr = _split_at(h @ w_in, sizes)
        return (_heads(qa, NA_HEADS), _heads(ka, NA_HEADS), _heads(va, NA_HEADS),
                _heads(qb, GLA_HEADS), _heads(kb, GLA_HEADS), _heads(vb, GLA_HEADS), gb, lr)

    qa, ka, va, qb, kb, vb, gb, lr = project(hx)
    qac, kac, vac, qbc, kbc, vbc, gbc, lrc = project(hc)
    oa = neighbourhood_attention(qa, ka, va, kac, vac, rpb)
    ob, obc = gla_bidirectional(qb, kb, vb, lr, qbc, kbc, vbc, lrc, wa2, ba, with_ctx_out)
    yx = jnp.concatenate([_merge_heads(oa), gla_output(ob, gb, gnorm)], axis=-1) @ w_out
    if not with_ctx_out:
        return yx, None
    oac = context_attention(qac, kac, vac)
    yc = jnp.concatenate([_merge_heads(oac), gla_output(obc, gbc, gnorm)], axis=-1) @ w_out
    return yx, yc


def sliding_window_attention(q, k, v, kc, vc, sink_logit, cos, sin):
    b, hkv, g, n, dh = q.shape
    n_blk = n // SWA_BLOCK
    q_rot = apply_rope(q, cos, sin)
    pad = ((0, 0), (0, 0), (SWA_BLOCK, SWA_BLOCK), (0, 0))
    k_pad = jnp.pad(apply_rope(k, cos, sin), pad)
    v_pad = jnp.pad(v, pad)
    n_loc = 3 * SWA_BLOCK

    def block(i):
        q0 = i * SWA_BLOCK
        qr = lax.dynamic_slice_in_dim(q_rot, q0, SWA_BLOCK, axis=3)
        qp = lax.dynamic_slice_in_dim(q, q0, SWA_BLOCK, axis=3)
        kb = lax.dynamic_slice_in_dim(k_pad, q0, n_loc, axis=2)
        vb = lax.dynamic_slice_in_dim(v_pad, q0, n_loc, axis=2)
        qpos = q0 + jnp.arange(SWA_BLOCK)
        kpos = q0 - SWA_BLOCK + jnp.arange(n_loc)
        ok = ((jnp.abs(kpos[None, :] - qpos[:, None]) <= SWA_WINDOW)
              & (kpos >= 0)[None, :] & (kpos < n)[None, :])
        s_loc = jnp.where(ok, jnp.einsum('bhgqd,bhkd->bhgqk', qr, kb).astype(jnp.float32), NEG_INF)
        s_ctx = jnp.einsum('bhgqd,bhcd->bhgqc', qp, kc).astype(jnp.float32)
        s_sink = jnp.broadcast_to(sink_logit, s_loc.shape[:-1] + (1,))
        p = jax.nn.softmax(jnp.concatenate([s_loc, s_ctx, s_sink], axis=-1), axis=-1).astype(v.dtype)
        return (jnp.einsum('bhgqk,bhkd->bhgqd', p[..., :n_loc], vb)
                + jnp.einsum('bhgqc,bhcd->bhgqd', p[..., n_loc:n_loc + kc.shape[2]], vc))

    o = lax.map(block, jnp.arange(n_blk))
    return jnp.moveaxis(o, 0, 3).reshape(b, hkv, g, n, dh)


def odd_mixer(hx, hc, w_in, w_out, sink, cos, sin, with_ctx_out):
    def project(h):
        qf, kf, vf = _split_at(h @ w_in, (SWA_Q_WIDTH, SWA_KV_WIDTH, SWA_KV_WIDTH))
        b, n, _ = h.shape
        q = qf.reshape(b, n, SWA_KV_HEADS, SWA_GROUP, HEAD_DIM).transpose(0, 2, 3, 1, 4) * HEAD_DIM ** -0.5
        return q, _heads(kf, SWA_KV_HEADS), _heads(vf, SWA_KV_HEADS)

    def merge(o):
        b, _, _, n, _ = o.shape
        return o.transpose(0, 3, 1, 2, 4).reshape(b, n, SWA_Q_WIDTH)

    sink_logit = sink.astype(jnp.float32).reshape(SWA_KV_HEADS, SWA_GROUP)[None, :, :, None, None]
    q, k, v = project(hx)
    qc, kc, vc = project(hc)
    yx = merge(sliding_window_attention(q, k, v, kc, vc, sink_logit, cos, sin)) @ w_out
    if not with_ctx_out:
        return yx, None
    s = jnp.einsum('bhgqd,bhkd->bhgqk', qc, kc).astype(jnp.float32)
    s = jnp.concatenate([s, jnp.broadcast_to(sink_logit, s.shape[:-1] + (1,))], axis=-1)
    p = jax.nn.softmax(s, axis=-1)[..., :-1].astype(vc.dtype)
    yc = merge(jnp.einsum('bhgqk,bhkd->bhgqd', p, vc)) @ w_out
    return yx, yc


def setup_inputs(seed: int = 0) -> dict:
    key = jax.random.key(seed)
    ks = jax.random.split(key, 24)
    d = D_MODEL

    def nrm(k, shape, s):
        return jax.random.normal(k, shape, jnp.float32) * s

    return {
        'x': nrm(ks[0], (BATCH, SEQ, d), 1.0),
        'c': nrm(ks[1], (BATCH, d), 1.0),
        'ctx': nrm(ks[2], (BATCH, CTX_LEN, d), 1.0),
        'c_ctx': nrm(ks[3], (d,), 1.0),
        'ada_w': nrm(ks[4], (DEPTH, d, 6 * d), 0.3 * d ** -0.5),
        'ada_b': nrm(ks[5], (DEPTH, 6 * d), 0.02),
        'norm_mix': 1.0 + nrm(ks[6], (DEPTH, d), 0.02),
        'norm_mlp': 1.0 + nrm(ks[7], (DEPTH, d), 0.02),
        'mlp_w1': nrm(ks[8], (DEPTH, d, D_FF), d ** -0.5),
        'mlp_w2': nrm(ks[9], (DEPTH, D_FF, d), D_FF ** -0.5),
        'ab_w_in': nrm(ks[10], (N_EVEN, d, AB_IN), d ** -0.5),
        'ab_w_out': nrm(ks[11], (N_EVEN, AB_OUT, d), AB_OUT ** -0.5),
        'na_rpb': nrm(ks[12], (N_EVEN, NA_HEADS, 2 * NA_WIN_ROWS - 1, 2 * NA_WIN_COLS - 1), 0.1),
        'gla_wa2': nrm(ks[13], (N_EVEN, 2, GLA_RANK, GLA_QK_WIDTH), GLA_RANK ** -0.5),
        'gla_ba': nrm(ks[14], (N_EVEN, 2, GLA_QK_WIDTH), 0.1),
        'gla_gnorm': 1.0 + nrm(ks[15], (N_EVEN, GLA_V_WIDTH), 0.02),
        'swa_w_in': nrm(ks[16], (N_ODD, d, SWA_IN), d ** -0.5),
        'swa_w_out': nrm(ks[17], (N_ODD, SWA_Q_WIDTH, d), SWA_Q_WIDTH ** -0.5),
        'swa_sink': nrm(ks[18], (N_ODD, SWA_HEADS), 0.5),
        'norm_final': 1.0 + nrm(ks[19], (d,), 0.02),
    }


def reference(x, c, ctx, c_ctx, ada_w, ada_b, norm_mix, norm_mlp, mlp_w1, mlp_w2,
              ab_w_in, ab_w_out, na_rpb, gla_wa2, gla_ba, gla_gnorm,
              swa_w_in, swa_w_out, swa_sink, norm_final):
    cos, sin = axial_rope_tables(x.shape[1])
    sc = jax.nn.silu(c)
    sc_ctx = jax.nn.silu(c_ctx)
    for l in range(DEPTH):
        ctx_out = l < DEPTH - 1
        mx = jnp.split(sc @ ada_w[l] + ada_b[l], 6, axis=-1)
        mc = jnp.split(sc_ctx @ ada_w[l] + ada_b[l], 6, axis=-1)
        hx = modulate(rms_norm(x, norm_mix[l]), mx[0][:, None], mx[1][:, None])
        hc = modulate(rms_norm(ctx, norm_mix[l]), mc[0], mc[1])
        if l % 2 == 0:
            j = l // 2
            yx, yc = even_mixer(hx, hc, ab_w_in[j], ab_w_out[j], na_rpb[j], gla_wa2[j], gla_ba[j],
                                gla_gnorm[j], ctx_out)
        else:
            j = l // 2
            yx, yc = odd_mixer(hx, hc, swa_w_in[j], swa_w_out[j], swa_sink[j], cos, sin, ctx_out)
        x = x + mx[2][:, None] * yx
        hx = modulate(rms_norm(x, norm_mlp[l]), mx[3][:, None], mx[4][:, None])
        x = x + mx[5][:, None] * squared_relu_mlp(hx, mlp_w1[l], mlp_w2[l])
        if ctx_out:
            ctx = ctx + mc[2] * yc
            hc = modulate(rms_norm(ctx, norm_mlp[l]), mc[3], mc[4])
            ctx = ctx + mc[5] * squared_relu_mlp(hc, mlp_w1[l], mlp_w2[l])
    return rms_norm(x, norm_final)
```

```python
import functools

import numpy as np
import jax
import jax.numpy as jnp
from jax import lax
from jax.experimental import pallas as pl
from jax.experimental.pallas import tpu as pltpu

D_MODEL = 1024
SEQ = 2048
DEPTH = 4
GRID_W = 64
GRID_ROWS = SEQ // GRID_W
CTX_LEN = 256
TOK = SEQ + CTX_LEN
HEAD_DIM = 64
EPS = 1e-6
NEG_INF = -1e30

NA_HEADS = 8
NA_WIN_ROWS = 8
NA_WIN_COLS = 16
NA_WIDTH = NA_HEADS * HEAD_DIM
NA_Q_ROWS = 4
NA_K_ROWS = NA_Q_ROWS + NA_WIN_ROWS - 1
NA_QN = NA_Q_ROWS * GRID_W
NA_KN = NA_K_ROWS * GRID_W
NA_ROW_BLOCKS = GRID_ROWS // NA_Q_ROWS

GLA_HEADS = 4
GLA_DK = 64
GLA_DV = 128
GLA_RANK = 16
GLA_NORMALIZER = 16.0
GLA_CHUNK = 64
GLA_QK_WIDTH = GLA_HEADS * GLA_DK
GLA_V_WIDTH = GLA_HEADS * GLA_DV
GLA_IN_WIDTH = 2 * GLA_QK_WIDTH + 2 * GLA_V_WIDTH + 128
GLA_CTX_CHUNKS = CTX_LEN // GLA_CHUNK
GLA_CHUNKS = TOK // GLA_CHUNK

SWA_HEADS = 16
SWA_KV_HEADS = 4
SWA_GROUP = SWA_HEADS // SWA_KV_HEADS
SWA_WINDOW = 128
SWA_KV_WIDTH = SWA_KV_HEADS * HEAD_DIM
SWA_TQ = 256
SWA_NLOC = SWA_TQ + 2 * SWA_WINDOW

D_FF = 4 * D_MODEL
FF_CHUNK = 1024
ROPE_THETA = 10000.0

TM = 256
TILES = TOK // TM
LAT_TILES = SEQ // TM
MOD_ROWS = 16
VMEM_LIMIT = 56 * 1024 * 1024

F32 = jnp.float32
BF16 = jnp.bfloat16


def _nt(a, b):
    return lax.dot_general(a, b, (((1,), (1,)), ((), ())), preferred_element_type=F32)


def _tn(a, b):
    return lax.dot_general(a, b, (((0,), (0,)), ((), ())), preferred_element_type=F32)


def _mm(a, b):
    return jnp.dot(a, b, preferred_element_type=F32)


def _params(*sem):
    return pltpu.CompilerParams(dimension_semantics=sem, vmem_limit_bytes=VMEM_LIMIT)


def _mod_row(b, t):
    return jnp.where(t < LAT_TILES, b, 8)


def _ada_kernel(c_ref, w_ref, b_ref, o_ref):
    s = c_ref[...]
    s = s * jax.nn.sigmoid(s)
    o_ref[0] = _mm(s.astype(BF16), w_ref[0].astype(BF16)) + b_ref[0]


def _ada_table(cvec, ada_w, ada_b):
    nb = 6 * D_MODEL // 1024
    return pl.pallas_call(
        _ada_kernel,
        grid=(DEPTH, nb),
        in_specs=[
            pl.BlockSpec((MOD_ROWS, D_MODEL), lambda l, n: (0, 0)),
            pl.BlockSpec((1, D_MODEL, 1024), lambda l, n: (l, 0, n)),
            pl.BlockSpec((1, 1, 1024), lambda l, n: (l, 0, n)),
        ],
        out_specs=pl.BlockSpec((1, MOD_ROWS, 1024), lambda l, n: (l, 0, n)),
        out_shape=jax.ShapeDtypeStruct((DEPTH, MOD_ROWS, 6 * D_MODEL), F32),
        compiler_params=_params("parallel", "parallel"),
    )(cvec, ada_w, ada_b.reshape(DEPTH, 1, 6 * D_MODEL))


def _norm_modulate(x, gain, shift, scale):
    y = x * lax.rsqrt(jnp.mean(x * x, axis=-1, keepdims=True) + EPS) * gain
    return y * (1.0 + scale) + shift


def _inproj_even_kernel(x_ref, mod_ref, g_ref, w_ref, na_ref, gla_ref):
    mod = mod_ref[0]
    h = _norm_modulate(x_ref[0], g_ref[...], mod[0:1], mod[1:2]).astype(BF16)
    n_na = 3 * NA_WIDTH
    na_ref[0] = _mm(h, w_ref[:, :n_na]).astype(BF16)
    gla_ref[0] = _mm(h, w_ref[:, n_na:])


def _inproj_even(xs, mod, gain, w):
    bsz = xs.shape[0]
    n_na = 3 * NA_WIDTH
    return pl.pallas_call(
        _inproj_even_kernel,
        grid=(bsz, TILES),
        in_specs=[
            pl.BlockSpec((1, TM, D_MODEL), lambda b, t: (b, t, 0)),
            pl.BlockSpec((1, 6, D_MODEL), lambda b, t: (_mod_row(b, t), 0, 0)),
            pl.BlockSpec((1, D_MODEL), lambda b, t: (0, 0)),
            pl.BlockSpec((D_MODEL, n_na + GLA_IN_WIDTH), lambda b, t: (0, 0)),
        ],
        out_specs=[
            pl.BlockSpec((1, TM, n_na), lambda b, t: (b, t, 0)),
            pl.BlockSpec((1, TM, GLA_IN_WIDTH), lambda b, t: (b, t, 0)),
        ],
        out_shape=[
            jax.ShapeDtypeStruct((bsz, TOK, n_na), BF16),
            jax.ShapeDtypeStruct((bsz, TOK, GLA_IN_WIDTH), F32),
        ],
        compiler_params=_params("parallel", "parallel"),
    )(xs, mod, gain, w)


def _rope(a, cos, sin, first_half):
    swapped = jnp.where(first_half, pltpu.roll(a, 96, 1), pltpu.roll(a, 32, 1))
    return a * cos + swapped * sin


def _inproj_odd_kernel(x_ref, mod_ref, g_ref, w_ref, cos_ref, sin_ref, q_ref, qr_ref, kr_ref, v_ref):
    mod = mod_ref[0]
    h = _norm_modulate(x_ref[0], g_ref[...], mod[0:1], mod[1:2]).astype(BF16)
    cos = cos_ref[...]
    sin = sin_ref[...]
    first_half = (lax.broadcasted_iota(jnp.int32, (TM, 128), 1) % HEAD_DIM) < HEAD_DIM // 2
    for j in range(D_MODEL // 128):
        a = _mm(h, w_ref[:, j * 128:(j + 1) * 128])
        q_ref[0, :, j * 128:(j + 1) * 128] = a.astype(BF16)
        qr_ref[0, :, j * 128:(j + 1) * 128] = _rope(a, cos, sin, first_half).astype(BF16)
    for j in range(SWA_KV_WIDTH // 128):
        c0 = D_MODEL + j * 128
        a = _mm(h, w_ref[:, c0:c0 + 128])
        kr_ref[0, :, j * 128:(j + 1) * 128] = _rope(a, cos, sin, first_half).astype(BF16)
    v_ref[0] = _mm(h, w_ref[:, D_MODEL + SWA_KV_WIDTH:]).astype(BF16)


def _inproj_odd(xs, mod, gain, w, cos_t, sin_t):
    bsz = xs.shape[0]
    n_in = D_MODEL + 2 * SWA_KV_WIDTH
    tile = lambda b, t: (b, t, 0)
    return pl.pallas_call(
        _inproj_odd_kernel,
        grid=(bsz, TILES),
        in_specs=[
            pl.BlockSpec((1, TM, D_MODEL), tile),
            pl.BlockSpec((1, 6, D_MODEL), lambda b, t: (_mod_row(b, t), 0, 0)),
            pl.BlockSpec((1, D_MODEL), lambda b, t: (0, 0)),
            pl.BlockSpec((D_MODEL, n_in), lambda b, t: (0, 0)),
            pl.BlockSpec((TM, 128), lambda b, t: (t, 0)),
            pl.BlockSpec((TM, 128), lambda b, t: (t, 0)),
        ],
        out_specs=[
            pl.BlockSpec((1, TM, D_MODEL), tile),
            pl.BlockSpec((1, TM, D_MODEL), tile),
            pl.BlockSpec((1, TM, SWA_KV_WIDTH), tile),
            pl.BlockSpec((1, TM, SWA_KV_WIDTH), tile),
        ],
        out_shape=[
            jax.ShapeDtypeStruct((bsz, TOK, D_MODEL), BF16),
            jax.ShapeDtypeStruct((bsz, TOK, D_MODEL), BF16),
            jax.ShapeDtypeStruct((bsz, TOK, SWA_KV_WIDTH), BF16),
            jax.ShapeDtypeStruct((bsz, TOK, SWA_KV_WIDTH), BF16),
        ],
        compiler_params=_params("parallel", "parallel"),
    )(xs, mod, gain, w, cos_t, sin_t)


def _rope_tables():
    t = np.arange(SEQ)
    n_freq = HEAD_DIM // 4
    inv = jnp.asarray(ROPE_THETA, F32) ** (-jnp.arange(n_freq, dtype=F32) / n_freq)
    row = jnp.asarray(t // GRID_W, F32)
    col = jnp.asarray(t % GRID_W, F32)
    ang = jnp.concatenate([row[:, None] * inv, col[:, None] * inv], axis=-1)
    cos, sin = jnp.cos(ang), jnp.sin(ang)
    cos_t = jnp.tile(cos, (1, 4))
    sin_t = jnp.tile(jnp.concatenate([-sin, sin], axis=-1), (1, 2))
    cos_t = jnp.concatenate([cos_t, jnp.ones((CTX_LEN, 128), F32)], axis=0)
    sin_t = jnp.concatenate([sin_t, jnp.zeros((CTX_LEN, 128), F32)], axis=0)
    return cos_t, sin_t


def _na_bias_index():
    q_row0 = np.array([0, NA_Q_ROWS, GRID_ROWS - NA_Q_ROWS])
    k_row0 = np.clip(q_row0 - NA_WIN_ROWS // 2, 0, GRID_ROWS - NA_K_ROWS)
    qi = np.arange(NA_QN)
    kj = np.arange(NA_KN)
    r = q_row0[:, None] + qi[None, :] // GRID_W
    c = qi % GRID_W
    kr = k_row0[:, None] + kj[None, :] // GRID_W
    kc = kj % GRID_W
    r0 = np.clip(r - NA_WIN_ROWS // 2, 0, GRID_ROWS - NA_WIN_ROWS)
    row_ok = (kr[:, None, :] >= r0[:, :, None]) & (kr[:, None, :] < r0[:, :, None] + NA_WIN_ROWS)
    w0 = np.clip(c - NA_WIN_COLS // 2, 0, GRID_W - NA_WIN_COLS)
    col_ok = (kc[None, :] >= w0[:, None]) & (kc[None, :] < w0[:, None] + NA_WIN_COLS)
    ok = row_ok & col_ok[None]
    ri = np.clip(kr[:, None, :] - r[:, :, None] + NA_WIN_ROWS - 1, 0, 2 * NA_WIN_ROWS - 2)
    ci = np.clip(kc[None, :] - c[:, None] + NA_WIN_COLS - 1, 0, 2 * NA_WIN_COLS - 2)
    flat = ri * (2 * NA_WIN_COLS - 1) + ci[None]
    return flat.astype(np.int32), ok


_NA_BIAS_FLAT, _NA_BIAS_OK = _na_bias_index()


def _na_bias_table(rpb):
    flat = rpb.astype(F32).reshape(NA_HEADS, -1)
    g = jnp.take(flat, jnp.asarray(_NA_BIAS_FLAT.reshape(-1)), axis=1).reshape(NA_HEADS, 3, NA_QN, NA_KN)
    return jnp.where(jnp.asarray(_NA_BIAS_OK)[None], g, NEG_INF)


def _na_kernel(q_ref, k_ref, v_ref, bias_ref, o_ref):
    rb = pl.program_id(2)
    q = q_ref[0]
    kc = k_ref[0, SEQ:, :]
    vc = v_ref[0, SEQ:, :]

    @pl.when(rb < NA_ROW_BLOCKS)
    def _latent():
        k_row0 = jnp.clip(rb * NA_Q_ROWS - NA_WIN_ROWS // 2, 0, GRID_ROWS - NA_K_ROWS)
        start = pl.multiple_of(k_row0 * GRID_W, GRID_W)
        cls = jnp.where(rb == 0, 0, jnp.where(rb == NA_ROW_BLOCKS - 1, 2, 1))
        kn = k_ref[0, pl.ds(start, NA_KN), :]
        vn = v_ref[0, pl.ds(start, NA_KN), :]
        outs = []
        for h in range(2):
            sl = slice(h * HEAD_DIM, (h + 1) * HEAD_DIM)
            s_nb = _nt(q[:, sl], kn[:, sl]) + bias_ref[0, h * 3 + cls]
            s_cx = _nt(q[:, sl], kc[:, sl])
            m = jnp.maximum(jnp.max(s_nb, axis=-1, keepdims=True), jnp.max(s_cx, axis=-1, keepdims=True))
            p_nb = jnp.exp(s_nb - m)
            p_cx = jnp.exp(s_cx - m)
            den = jnp.sum(p_nb, axis=-1, keepdims=True) + jnp.sum(p_cx, axis=-1, keepdims=True)
            o = _mm(p_nb.astype(BF16), vn[:, sl]) + _mm(p_cx.astype(BF16), vc[:, sl])
            outs.append(o / den)
        o_ref[0] = jnp.concatenate(outs, axis=-1).astype(BF16)

    @pl.when(rb == NA_ROW_BLOCKS)
    def _context():
        outs = []
        for h in range(2):
            sl = slice(h * HEAD_DIM, (h + 1) * HEAD_DIM)
            s = _nt(q[:, sl], kc[:, sl])
            p = jnp.exp(s - jnp.max(s, axis=-1, keepdims=True))
            o = _mm(p.astype(BF16), vc[:, sl])
            outs.append(o / jnp.sum(p, axis=-1, keepdims=True))
        o_ref[0] = jnp.concatenate(outs, axis=-1).astype(BF16)


def _na_attention(qkv, bias):
    bsz = qkv.shape[0]
    n_pairs = NA_HEADS // 2
    return pl.pallas_call(
        _na_kernel,
        grid=(n_pairs, bsz, NA_ROW_BLOCKS + 1),
        in_specs=[
            pl.BlockSpec((1, NA_QN, 128), lambda p, b, r: (b, r, p)),
            pl.BlockSpec((1, TOK, 128), lambda p, b, r: (b, 0, n_pairs + p)),
            pl.BlockSpec((1, TOK, 128), lambda p, b, r: (b, 0, 2 * n_pairs + p)),
            pl.BlockSpec((1, 6, NA_QN, NA_KN), lambda p, b, r: (p, 0, 0, 0)),
        ],
        out_specs=pl.BlockSpec((1, NA_QN, 128), lambda p, b, r: (b, r, p)),
        out_shape=jax.ShapeDtypeStruct((bsz, TOK, NA_WIDTH), BF16),
        compiler_params=_params("parallel", "parallel", "arbitrary"),
    )(qkv, qkv, qkv, bias.reshape(n_pairs, 6, NA_QN, NA_KN))


def _split3(g):
    hi = g.astype(BF16)
    r1 = g - hi.astype(F32)
    mid = r1.astype(BF16)
    lo = (r1 - mid.astype(F32)).astype(BF16)
    return hi, mid, lo


def _gla_kernel(q_ref, k_ref, v_ref, gate_ref, lr_ref, wa2_ref, ba_ref, gn_ref, o_ref, acc_ref, st_ref):
    c = GLA_CHUNK
    acc_ref[...] = jnp.zeros_like(acc_ref)
    st_ref[...] = jnp.zeros_like(st_ref)
    ii = lax.broadcasted_iota(jnp.int32, (c, c), 0)
    jj = lax.broadcasted_iota(jnp.int32, (c, c), 1)
    incl = (jj <= ii, jj >= ii)
    tri = tuple(m.astype(BF16) for m in incl)

    def chunk(i, carry):
        in_ctx = i < GLA_CTX_CHUNKS
        offs = (
            jnp.where(in_ctx, SEQ + c * i, c * (i - GLA_CTX_CHUNKS)),
            jnp.where(in_ctx, SEQ + c * (GLA_CTX_CHUNKS - 1 - i), c * (GLA_CHUNKS - 1 - i)),
        )
        for d in range(2):
            off = pl.multiple_of(offs[d], c)
            rows = pl.ds(off, c)
            lr = lr_ref[0, rows, :][:, d * GLA_RANK:(d + 1) * GLA_RANK]
            z = _mm(lr.astype(BF16), wa2_ref[d].astype(BF16)) + ba_ref[d]
            g = (jnp.minimum(z, 0.0) - jnp.log1p(jnp.exp(-jnp.abs(z)))) / GLA_NORMALIZER
            hi, mid, lo = _split3(g)
            cum = _mm(tri[d], hi) + _mm(tri[d], mid) + _mm(tri[d], lo)
            tot = cum[c - 1:c, :] if d == 0 else cum[0:1, :]
            qc = q_ref[0, rows, :]
            kc = k_ref[0, rows, :]
            q_t = (qc * jnp.exp(cum)).astype(BF16)
            k_t = (kc * jnp.exp(-cum)).astype(BF16)
            k_end = (kc * jnp.exp(tot - cum)).astype(BF16)
            decay = jnp.exp(tot)
            for h in range(2):
                sl = slice(h * GLA_DK, (h + 1) * GLA_DK)
                vs = slice(h * GLA_DV, (h + 1) * GLA_DV)
                vch = v_ref[0, rows, vs].astype(BF16)
                a = jnp.where(incl[d], _nt(q_t[:, sl], k_t[:, sl]), 0.0).astype(BF16)
                s_t = st_ref[d, h]
                o = _mm(a, vch) + _nt(q_t[:, sl], s_t.astype(BF16))
                acc_ref[rows, vs] = acc_ref[rows, vs] + o
                st_ref[d, h] = decay[:, sl] * s_t + _tn(vch, k_end[:, sl])
        return carry

    lax.fori_loop(0, GLA_CHUNKS, chunk, 0)

    def finish(t, carry):
        rows = pl.ds(pl.multiple_of(t * TM, TM), TM)
        gate = gate_ref[0, rows, :]
        sw = gate * jax.nn.sigmoid(gate)
        for h in range(2):
            vs = slice(h * GLA_DV, (h + 1) * GLA_DV)
            o = acc_ref[rows, vs]
            o = o * lax.rsqrt(jnp.mean(o * o, axis=-1, keepdims=True) + EPS)
            o_ref[0, rows, vs] = (o * gn_ref[:, vs] * sw[:, vs]).astype(BF16)
        return carry

    lax.fori_loop(0, TILES, finish, 0)


def _gla(gla_in, wa2, ba, gnorm):
    bsz = gla_in.shape[0]
    pairs = GLA_HEADS // 2
    qk_blocks = GLA_QK_WIDTH // 128
    v_blocks = GLA_V_WIDTH // 256
    v0 = 2 * GLA_QK_WIDTH // 256
    lr_block = (2 * GLA_QK_WIDTH + 2 * GLA_V_WIDTH) // 128
    return pl.pallas_call(
        _gla_kernel,
        grid=(bsz, pairs),
        in_specs=[
            pl.BlockSpec((1, TOK, 128), lambda b, p: (b, 0, p)),
            pl.BlockSpec((1, TOK, 128), lambda b, p: (b, 0, qk_blocks + p)),
            pl.BlockSpec((1, TOK, 256), lambda b, p: (b, 0, v0 + p)),
            pl.BlockSpec((1, TOK, 256), lambda b, p: (b, 0, v0 + v_blocks + p)),
            pl.BlockSpec((1, TOK, 128), lambda b, p: (b, 0, lr_block)),
            pl.BlockSpec((2, GLA_RANK, 128), lambda b, p: (0, 0, p)),
            pl.BlockSpec((2, 1, 128), lambda b, p: (0, 0, p)),
            pl.BlockSpec((1, 256), lambda b, p: (0, p)),
        ],
        out_specs=pl.BlockSpec((1, TOK, 256), lambda b, p: (b, 0, p)),
        out_shape=jax.ShapeDtypeStruct((bsz, TOK, GLA_V_WIDTH), BF16),
        scratch_shapes=[
            pltpu.VMEM((TOK, 256), F32),
            pltpu.VMEM((2, 2, GLA_DV, GLA_DK), F32),
        ],
        compiler_params=_params("parallel", "parallel"),
    )(gla_in, gla_in, gla_in, gla_in, gla_in, wa2, ba.reshape(2, 1, GLA_QK_WIDTH), gnorm.reshape(1, GLA_V_WIDTH))


def _swa_kernel(sink_ref, q_ref, qr_ref, k_ref, v_ref, o_ref):
    kp = pl.program_id(1)
    qb = pl.program_id(2)
    kc = k_ref[0, SEQ:, :]
    vc = v_ref[0, SEQ:, :]
    heads_per_step = 2 * SWA_GROUP

    def head_out(h, s_list, v_list):
        sink = sink_ref[kp * heads_per_step + h]
        m = jnp.max(s_list[0], axis=-1, keepdims=True)
        for s in s_list[1:]:
            m = jnp.maximum(m, jnp.max(s, axis=-1, keepdims=True))
        m = jnp.maximum(m, sink)
        den = jnp.exp(sink - m)
        o = None
        for s, v in zip(s_list, v_list):
            p = jnp.exp(s - m)
            den = den + jnp.sum(p, axis=-1, keepdims=True)
            pv = _mm(p.astype(BF16), v)
            o = pv if o is None else o + pv
        return o / den

    @pl.when(qb < LAT_TILES)
    def _latent():
        q0 = qb * SWA_TQ
        start = pl.multiple_of(jnp.clip(q0 - SWA_WINDOW, 0, SEQ - SWA_NLOC), SWA_WINDOW)
        kl = k_ref[0, pl.ds(start, SWA_NLOC), :]
        vl = v_ref[0, pl.ds(start, SWA_NLOC), :]
        qpos = q0 + lax.broadcasted_iota(jnp.int32, (SWA_TQ, SWA_NLOC), 0)
        kpos = start + lax.broadcasted_iota(jnp.int32, (SWA_TQ, SWA_NLOC), 1)
        ok = jnp.abs(kpos - qpos) <= SWA_WINDOW
        q = q_ref[0]
        qr = qr_ref[0]
        outs = []
        for h in range(heads_per_step):
            sl = slice(h * HEAD_DIM, (h + 1) * HEAD_DIM)
            kv = slice((h // SWA_GROUP) * HEAD_DIM, (h // SWA_GROUP + 1) * HEAD_DIM)
            s_loc = jnp.where(ok, _nt(qr[:, sl], kl[:, kv]), NEG_INF)
            s_cx = _nt(q[:, sl], kc[:, kv])
            outs.append(head_out(h, [s_loc, s_cx], [vl[:, kv], vc[:, kv]]))
        o_ref[0] = jnp.concatenate(outs, axis=-1).astype(BF16)

    @pl.when(qb == LAT_TILES)
    def _context():
        q = q_ref[0]
        outs = []
        for h in range(heads_per_step):
            sl = slice(h * HEAD_DIM, (h + 1) * HEAD_DIM)
            kv = slice((h // SWA_GROUP) * HEAD_DIM, (h // SWA_GROUP + 1) * HEAD_DIM)
            outs.append(head_out(h, [_nt(q[:, sl], kc[:, kv])], [vc[:, kv]]))
        o_ref[0] = jnp.concatenate(outs, axis=-1).astype(BF16)


def _swa_attention(q, qr, kr, v, sink):
    bsz = q.shape[0]
    pairs = SWA_KV_HEADS // 2
    qw = 2 * SWA_GROUP * HEAD_DIM
    return pl.pallas_call(
        _swa_kernel,
        grid=(bsz, pairs, TILES),
        in_specs=[
            pl.BlockSpec(memory_space=pltpu.SMEM),
            pl.BlockSpec((1, SWA_TQ, qw), lambda b, p, t: (b, t, p)),
            pl.BlockSpec((1, SWA_TQ, qw), lambda b, p, t: (b, t, p)),
            pl.BlockSpec((1, TOK, 128), lambda b, p, t: (b, 0, p)),
            pl.BlockSpec((1, TOK, 128), lambda b, p, t: (b, 0, p)),
        ],
        out_specs=pl.BlockSpec((1, SWA_TQ, qw), lambda b, p, t: (b, t, p)),
        out_shape=jax.ShapeDtypeStruct((bsz, TOK, D_MODEL), BF16),
        compiler_params=_params("parallel", "parallel", "arbitrary"),
    )(sink.astype(F32), q, qr, kr, v)


def _mlp_kernel(x_ref, oa_ref, ob_ref, mod_ref, g_ref, gf_ref, wo_ref, w1_ref, w2_ref, out_ref, *, final_norm):
    mod = mod_ref[0]
    half = wo_ref.shape[0] // 2
    y = _mm(oa_ref[0], wo_ref[:half, :]) + _mm(ob_ref[0], wo_ref[half:, :])
    x1 = x_ref[0] + mod[2:3] * y
    h = _norm_modulate(x1, g_ref[...], mod[3:4], mod[4:5]).astype(BF16)
    acc = jnp.zeros((TM, D_MODEL), F32)
    for c in range(D_FF // FF_CHUNK):
        t = jnp.maximum(_mm(h, w1_ref[:, c * FF_CHUNK:(c + 1) * FF_CHUNK]), 0.0)
        acc = acc + _mm((t * t).astype(BF16), w2_ref[c * FF_CHUNK:(c + 1) * FF_CHUNK, :])
    x2 = x1 + mod[5:6] * acc
    if final_norm:
        x2 = x2 * lax.rsqrt(jnp.mean(x2 * x2, axis=-1, keepdims=True) + EPS) * gf_ref[...]
    out_ref[0] = x2


def _outproj_mlp(xs, oa, ob, ob_block, mod, gain, gain_final, wo, w1, w2, final_norm):
    bsz = xs.shape[0]
    half = D_MODEL // 2
    tiles = LAT_TILES if final_norm else TILES
    tile = lambda b, t: (b, t, 0)
    const = lambda b, t: (0, 0)
    return pl.pallas_call(
        functools.partial(_mlp_kernel, final_norm=final_norm),
        grid=(bsz, tiles),
        in_specs=[
            pl.BlockSpec((1, TM, D_MODEL), tile),
            pl.BlockSpec((1, TM, half), tile),
            pl.BlockSpec((1, TM, half), lambda b, t: (b, t, ob_block)),
            pl.BlockSpec((1, 6, D_MODEL), lambda b, t: (_mod_row(b, t), 0, 0)),
            pl.BlockSpec((1, D_MODEL), const),
            pl.BlockSpec((1, D_MODEL), const),
            pl.BlockSpec((D_MODEL, D_MODEL), const),
            pl.BlockSpec((D_MODEL, D_FF), const),
            pl.BlockSpec((D_FF, D_MODEL), const),
        ],
        out_specs=pl.BlockSpec((1, TM, D_MODEL), tile),
        out_shape=jax.ShapeDtypeStruct((bsz, tiles * TM, D_MODEL), F32),
        compiler_params=_params("parallel", "parallel"),
    )(xs, oa, ob, mod, gain, gain_final, wo, w1, w2)


def _even_in_weight(w):
    n_na = 3 * NA_WIDTH
    scale = np.ones((w.shape[1],), np.float32)
    scale[:NA_WIDTH] = HEAD_DIM ** -0.5
    scale[n_na:n_na + GLA_QK_WIDTH] = GLA_DK ** -0.5
    w = w * jnp.asarray(scale)
    pad = n_na + GLA_IN_WIDTH - w.shape[1]
    return jnp.pad(w, ((0, 0), (0, pad))).astype(BF16)


def _odd_in_weight(w):
    perm = np.concatenate([np.arange(0, HEAD_DIM, 2), np.arange(1, HEAD_DIM, 2)])
    n_rot = D_MODEL + SWA_KV_WIDTH
    cols = np.arange(w.shape[1])
    cols[:n_rot] = (np.arange(n_rot) // HEAD_DIM) * HEAD_DIM + perm[np.arange(n_rot) % HEAD_DIM]
    scale = np.ones((w.shape[1],), np.float32)
    scale[:D_MODEL] = HEAD_DIM ** -0.5
    return (w[:, cols] * jnp.asarray(scale)).astype(BF16)


def kernel(x, c, ctx, c_ctx, ada_w, ada_b, norm_mix, norm_mlp, mlp_w1, mlp_w2, ab_w_in, ab_w_out, na_rpb,
           gla_wa2, gla_ba, gla_gnorm, swa_w_in, swa_w_out, swa_sink, norm_final):
    bsz = x.shape[0]
    assert x.shape == (bsz, SEQ, D_MODEL) and ctx.shape == (bsz, CTX_LEN, D_MODEL) and bsz <= 8

    cvec = jnp.zeros((MOD_ROWS, D_MODEL), F32).at[:bsz].set(c).at[8].set(c_ctx)
    mods = _ada_table(cvec, ada_w, ada_b).reshape(DEPTH, MOD_ROWS, 6, D_MODEL)
    cos_t, sin_t = _rope_tables()
    gain_final = norm_final.reshape(1, D_MODEL)

    xs = jnp.concatenate([x, ctx], axis=1)
    for l in range(DEPTH):
        j = l // 2
        mod = mods[l]
        g_mix = norm_mix[l].reshape(1, D_MODEL)
        g_mlp = norm_mlp[l].reshape(1, D_MODEL)
        if l % 2 == 0:
            na_in, gla_in = _inproj_even(xs, mod, g_mix, _even_in_weight(ab_w_in[j]))
            oa = _na_attention(na_in, _na_bias_table(na_rpb[j]))
            ob = _gla(gla_in, gla_wa2[j], gla_ba[j], gla_gnorm[j])
            ob_block = 0
            wo = ab_w_out[j].astype(BF16)
        else:
            q, qr, kr, v = _inproj_odd(xs, mod, g_mix, _odd_in_weight(swa_w_in[j]), cos_t, sin_t)
            oa = ob = _swa_attention(q, qr, kr, v, swa_sink[j])
            ob_block = 1
            wo = swa_w_out[j].astype(BF16)
        xs = _outproj_mlp(xs, oa, ob, ob_block, mod, g_mlp, gain_final, wo,
                          mlp_w1[l].astype(BF16), mlp_w2[l].astype(BF16), l == DEPTH - 1)
    return xs
```

```python
import functools

import numpy as np
import jax
import jax.numpy as jnp
from jax import lax
from jax.experimental import pallas as pl
from jax.experimental.pallas import tpu as pltpu

D_MODEL = 1024
SEQ = 2048
DEPTH = 4
GRID_W = 64
GRID_ROWS = SEQ // GRID_W
CTX_LEN = 256
TOK = SEQ + CTX_LEN
HEAD_DIM = 64
EPS = 1e-6
NEG_INF = -1e30

NA_HEADS = 8
NA_WIN_ROWS = 8
NA_WIN_COLS = 16
NA_WIDTH = NA_HEADS * HEAD_DIM
NA_Q_ROWS = 4
NA_K_ROWS = NA_Q_ROWS + NA_WIN_ROWS - 1
NA_QN = NA_Q_ROWS * GRID_W
NA_KN = NA_K_ROWS * GRID_W
NA_ROW_BLOCKS = GRID_ROWS // NA_Q_ROWS

GLA_HEADS = 4
GLA_DK = 64
GLA_DV = 128
GLA_RANK = 16
GLA_NORMALIZER = 16.0
GLA_CHUNK = 64
GLA_QK_WIDTH = GLA_HEADS * GLA_DK
GLA_V_WIDTH = GLA_HEADS * GLA_DV
GLA_IN_WIDTH = 2 * GLA_QK_WIDTH + 2 * GLA_V_WIDTH + 128
GLA_CTX_CHUNKS = CTX_LEN // GLA_CHUNK
GLA_CHUNKS = TOK // GLA_CHUNK

SWA_HEADS = 16
SWA_KV_HEADS = 4
SWA_GROUP = SWA_HEADS // SWA_KV_HEADS
SWA_WINDOW = 128
SWA_KV_WIDTH = SWA_KV_HEADS * HEAD_DIM
SWA_TQ = 256
SWA_NLOC = SWA_TQ + 2 * SWA_WINDOW

D_FF = 4 * D_MODEL
FF_CHUNK = 1024
ROPE_THETA = 10000.0

TM = 256
TILES = TOK // TM
LAT_TILES = SEQ // TM
MOD_ROWS = 16
VMEM_LIMIT = 56 * 1024 * 1024

F32 = jnp.float32
BF16 = jnp.bfloat16


def _nt(a, b):
    return lax.dot_general(a, b, (((1,), (1,)), ((), ())), preferred_element_type=F32)


def _tn(a, b):
    return lax.dot_general(a, b, (((0,), (0,)), ((), ())), preferred_element_type=F32)


def _mm(a, b):
    return jnp.dot(a, b, preferred_element_type=F32)


def _params(*sem):
    return pltpu.CompilerParams(dimension_semantics=sem, vmem_limit_bytes=VMEM_LIMIT)


def _mod_row(b, t):
    return jnp.where(t < LAT_TILES, b, 8)


def _ada_kernel(c_ref, w_ref, b_ref, o_ref):
    s = c_ref[...]
    s = s * jax.nn.sigmoid(s)
    o_ref[0] = _mm(s.astype(BF16), w_ref[0].astype(BF16)) + b_ref[0]


def _ada_table(cvec, ada_w, ada_b):
    nb = 6 * D_MODEL // 1024
    return pl.pallas_call(
        _ada_kernel,
        grid=(DEPTH, nb),
        in_specs=[
            pl.BlockSpec((MOD_ROWS, D_MODEL), lambda l, n: (0, 0)),
            pl.BlockSpec((1, D_MODEL, 1024), lambda l, n: (l, 0, n)),
            pl.BlockSpec((1, 1, 1024), lambda l, n: (l, 0, n)),
        ],
        out_specs=pl.BlockSpec((1, MOD_ROWS, 1024), lambda l, n: (l, 0, n)),
        out_shape=jax.ShapeDtypeStruct((DEPTH, MOD_ROWS, 6 * D_MODEL), F32),
        compiler_params=_params("parallel", "parallel"),
    )(cvec, ada_w, ada_b.reshape(DEPTH, 1, 6 * D_MODEL))


def _norm_modulate(x, gain, shift, scale):
    y = x * lax.rsqrt(jnp.mean(x * x, axis=-1, keepdims=True) + EPS) * gain
    return y * (1.0 + scale) + shift


def _inproj_even_kernel(x_ref, mod_ref, g_ref, w_ref, na_ref, gla_ref):
    mod = mod_ref[0]
    h = _norm_modulate(x_ref[0], g_ref[...], mod[0:1], mod[1:2]).astype(BF16)
    n_na = 3 * NA_WIDTH
    na_ref[0] = _mm(h, w_ref[:, :n_na]).astype(BF16)
    gla_ref[0] = _mm(h, w_ref[:, n_na:])


def _inproj_even(xs, mod, gain, w):
    bsz = xs.shape[0]
    n_na = 3 * NA_WIDTH
    return pl.pallas_call(
        _inproj_even_kernel,
        grid=(bsz, TILES),
        in_specs=[
            pl.BlockSpec((1, TM, D_MODEL), lambda b, t: (b, t, 0)),
            pl.BlockSpec((1, 6, D_MODEL), lambda b, t: (_mod_row(b, t), 0, 0)),
            pl.BlockSpec((1, D_MODEL), lambda b, t: (0, 0)),
            pl.BlockSpec((D_MODEL, n_na + GLA_IN_WIDTH), lambda b, t: (0, 0)),
        ],
        out_specs=[
            pl.BlockSpec((1, TM, n_na), lambda b, t: (b, t, 0)),
            pl.BlockSpec((1, TM, GLA_IN_WIDTH), lambda b, t: (b, t, 0)),
        ],
        out_shape=[
            jax.ShapeDtypeStruct((bsz, TOK, n_na), BF16),
            jax.ShapeDtypeStruct((bsz, TOK, GLA_IN_WIDTH), F32),
        ],
        compiler_params=_params("parallel", "parallel"),
    )(xs, mod, gain, w)


def _rope(a, cos, sin, first_half):
    swapped = jnp.where(first_half, pltpu.roll(a, 96, 1), pltpu.roll(a, 32, 1))
    return a * cos + swapped * sin


def _inproj_odd_kernel(x_ref, mod_ref, g_ref, w_ref, cos_ref, sin_ref, q_ref, qr_ref, kr_ref, v_ref):
    mod = mod_ref[0]
    h = _norm_modulate(x_ref[0], g_ref[...], mod[0:1], mod[1:2]).astype(BF16)
    cos = cos_ref[...]
    sin = sin_ref[...]
    first_half = (lax.broadcasted_iota(jnp.int32, (TM, 128), 1) % HEAD_DIM) < HEAD_DIM // 2
    for j in range(D_MODEL // 128):
        a = _mm(h, w_ref[:, j * 128:(j + 1) * 128])
        q_ref[0, :, j * 128:(j + 1) * 128] = a.astype(BF16)
        qr_ref[0, :, j * 128:(j + 1) * 128] = _rope(a, cos, sin, first_half).astype(BF16)
    for j in range(SWA_KV_WIDTH // 128):
        c0 = D_MODEL + j * 128
        a = _mm(h, w_ref[:, c0:c0 + 128])
        kr_ref[0, :, j * 128:(j + 1) * 128] = _rope(a, cos, sin, first_half).astype(BF16)
    v_ref[0] = _mm(h, w_ref[:, D_MODEL + SWA_KV_WIDTH:]).astype(BF16)


def _inproj_odd(xs, mod, gain, w, cos_t, sin_t):
    bsz = xs.shape[0]
    n_in = D_MODEL + 2 * SWA_KV_WIDTH
    tile = lambda b, t: (b, t, 0)
    return pl.pallas_call(
        _inproj_odd_kernel,
        grid=(bsz, TILES),
        in_specs=[
            pl.BlockSpec((1, TM, D_MODEL), tile),
            pl.BlockSpec((1, 6, D_MODEL), lambda b, t: (_mod_row(b, t), 0, 0)),
            pl.BlockSpec((1, D_MODEL), lambda b, t: (0, 0)),
            pl.BlockSpec((D_MODEL, n_in), lambda b, t: (0, 0)),
            pl.BlockSpec((TM, 128), lambda b, t: (t, 0)),
            pl.BlockSpec((TM, 128), lambda b, t: (t, 0)),
        ],
        out_specs=[
            pl.BlockSpec((1, TM, D_MODEL), tile),
            pl.BlockSpec((1, TM, D_MODEL), tile),
            pl.BlockSpec((1, TM, SWA_KV_WIDTH), tile),
            pl.BlockSpec((1, TM, SWA_KV_WIDTH), tile),
        ],
        out_shape=[
            jax.ShapeDtypeStruct((bsz, TOK, D_MODEL), BF16),
            jax.ShapeDtypeStruct((bsz, TOK, D_MODEL), BF16),
            jax.ShapeDtypeStruct((bsz, TOK, SWA_KV_WIDTH), BF16),
            jax.ShapeDtypeStruct((bsz, TOK, SWA_KV_WIDTH), BF16),
        ],
        compiler_params=_params("parallel", "parallel"),
    )(xs, mod, gain, w, cos_t, sin_t)


def _rope_tables():
    t = np.arange(SEQ)
    n_freq = HEAD_DIM // 4
    inv = jnp.asarray(ROPE_THETA, F32) ** (-jnp.arange(n_freq, dtype=F32) / n_freq)
    row = jnp.asarray(t // GRID_W, F32)
    col = jnp.asarray(t % GRID_W, F32)
    ang = jnp.concatenate([row[:, None] * inv, col[:, None] * inv], axis=-1)
    cos, sin = jnp.cos(ang), jnp.sin(ang)
    cos_t = jnp.tile(cos, (1, 4))
    sin_t = jnp.tile(jnp.concatenate([-sin, sin], axis=-1), (1, 2))
    cos_t = jnp.concatenate([cos_t, jnp.ones((CTX_LEN, 128), F32)], axis=0)
    sin_t = jnp.concatenate([sin_t, jnp.zeros((CTX_LEN, 128), F32)], axis=0)
    return cos_t, sin_t


_NA_Q_ROW0 = np.array([0, NA_Q_ROWS, GRID_ROWS - NA_Q_ROWS])
_NA_K_ROW0 = np.clip(_NA_Q_ROW0 - NA_WIN_ROWS // 2, 0, GRID_ROWS - NA_K_ROWS)


def _na_window_mask():
    qi = np.arange(NA_QN)
    kj = np.arange(NA_KN)
    r = _NA_Q_ROW0[:, None] + qi[None, :] // GRID_W
    c = qi % GRID_W
    kr = _NA_K_ROW0[:, None] + kj[None, :] // GRID_W
    kc = kj % GRID_W
    r0 = np.clip(r - NA_WIN_ROWS // 2, 0, GRID_ROWS - NA_WIN_ROWS)
    row_ok = (kr[:, None, :] >= r0[:, :, None]) & (kr[:, None, :] < r0[:, :, None] + NA_WIN_ROWS)
    w0 = np.clip(c - NA_WIN_COLS // 2, 0, GRID_W - NA_WIN_COLS)
    col_ok = (kc[None, :] >= w0[:, None]) & (kc[None, :] < w0[:, None] + NA_WIN_COLS)
    return row_ok & col_ok[None]


_NA_BIAS_OK = _na_window_mask()


def _na_bias_table(rpb):
    n = GRID_W
    lead = n - NA_WIN_COLS
    w = jnp.pad(rpb.astype(F32), ((0, 0), (0, 0), (lead, 2 * n - lead - (2 * NA_WIN_COLS - 1))))
    col = jnp.tile(w, (1, 1, n))[..., n - 1:n - 1 + n * (2 * n - 1)]
    col = col.reshape(NA_HEADS, 2 * NA_WIN_ROWS - 1, n, 2 * n - 1)[..., :n]
    lo = NA_Q_ROWS - 1
    colp = jnp.pad(col, ((0, 0), (lo, NA_K_ROWS), (0, 0), (0, 0)))
    blocks = []
    for cls in range(3):
        off = int(_NA_K_ROW0[cls] - _NA_Q_ROW0[cls]) + NA_WIN_ROWS - 1
        for i in range(NA_Q_ROWS):
            s0 = off - i + lo
            blocks.append(colp[:, s0:s0 + NA_K_ROWS])
    g = jnp.stack(blocks, axis=1).reshape(NA_HEADS, 3, NA_Q_ROWS, NA_K_ROWS, n, n)
    g = g.transpose(0, 1, 2, 4, 3, 5).reshape(NA_HEADS, 3, NA_QN, NA_KN)
    return jnp.where(jnp.asarray(_NA_BIAS_OK)[None], g, NEG_INF)


def _na_kernel(q_ref, k_ref, v_ref, bias_ref, o_ref):
    rb = pl.program_id(2)
    q = q_ref[0]
    kc = k_ref[0, SEQ:, :]
    vc = v_ref[0, SEQ:, :]

    @pl.when(rb < NA_ROW_BLOCKS)
    def _latent():
        k_row0 = jnp.clip(rb * NA_Q_ROWS - NA_WIN_ROWS // 2, 0, GRID_ROWS - NA_K_ROWS)
        start = pl.multiple_of(k_row0 * GRID_W, GRID_W)
        cls = jnp.where(rb == 0, 0, jnp.where(rb == NA_ROW_BLOCKS - 1, 2, 1))
        kn = k_ref[0, pl.ds(start, NA_KN), :]
        vn = v_ref[0, pl.ds(start, NA_KN), :]
        outs = []
        for h in range(2):
            sl = slice(h * HEAD_DIM, (h + 1) * HEAD_DIM)
            s_nb = _nt(q[:, sl], kn[:, sl]) + bias_ref[0, h * 3 + cls]
            s_cx = _nt(q[:, sl], kc[:, sl])
            m = jnp.maximum(jnp.max(s_nb, axis=-1, keepdims=True), jnp.max(s_cx, axis=-1, keepdims=True))
            p_nb = jnp.exp(s_nb - m)
            p_cx = jnp.exp(s_cx - m)
            den = jnp.sum(p_nb, axis=-1, keepdims=True) + jnp.sum(p_cx, axis=-1, keepdims=True)
            o = _mm(p_nb.astype(BF16), vn[:, sl]) + _mm(p_cx.astype(BF16), vc[:, sl])
            outs.append(o / den)
        o_ref[0] = jnp.concatenate(outs, axis=-1).astype(BF16)

    @pl.when(rb == NA_ROW_BLOCKS)
    def _context():
        outs = []
        for h in range(2):
            sl = slice(h * HEAD_DIM, (h + 1) * HEAD_DIM)
            s = _nt(q[:, sl], kc[:, sl])
            p = jnp.exp(s - jnp.max(s, axis=-1, keepdims=True))
            o = _mm(p.astype(BF16), vc[:, sl])
            outs.append(o / jnp.sum(p, axis=-1, keepdims=True))
        o_ref[0] = jnp.concatenate(outs, axis=-1).astype(BF16)


def _na_attention(qkv, bias):
    bsz = qkv.shape[0]
    n_pairs = NA_HEADS // 2
    return pl.pallas_call(
        _na_kernel,
        grid=(n_pairs, bsz, NA_ROW_BLOCKS + 1),
        in_specs=[
            pl.BlockSpec((1, NA_QN, 128), lambda p, b, r: (b, r, p)),
            pl.BlockSpec((1, TOK, 128), lambda p, b, r: (b, 0, n_pairs + p)),
            pl.BlockSpec((1, TOK, 128), lambda p, b, r: (b, 0, 2 * n_pairs + p)),
            pl.BlockSpec((1, 6, NA_QN, NA_KN), lambda p, b, r: (p, 0, 0, 0)),
        ],
        out_specs=pl.BlockSpec((1, NA_QN, 128), lambda p, b, r: (b, r, p)),
        out_shape=jax.ShapeDtypeStruct((bsz, TOK, NA_WIDTH), BF16),
        compiler_params=_params("parallel", "parallel", "arbitrary"),
    )(qkv, qkv, qkv, bias.reshape(n_pairs, 6, NA_QN, NA_KN))


def _split3(g):
    hi = g.astype(BF16)
    r1 = g - hi.astype(F32)
    mid = r1.astype(BF16)
    lo = (r1 - mid.astype(F32)).astype(BF16)
    return hi, mid, lo


def _gla_kernel(q_ref, k_ref, v_ref, gate_ref, lr_ref, wa2_ref, ba_ref, gn_ref, o_ref,
                acc_ref, cum_ref, qt_ref, u_ref, dec_ref, sp_ref, st_ref):
    c = GLA_CHUNK
    ii = lax.broadcasted_iota(jnp.int32, (c, c), 0)
    jj = lax.broadcasted_iota(jnp.int32, (c, c), 1)
    incl = (jj <= ii, jj >= ii)
    tri = tuple(jnp.where(m, 1.0, 0.0).astype(BF16) for m in incl)
    head0 = lax.broadcasted_iota(jnp.int32, (1, 128), 1) < GLA_DK
    per_tile = TM // c

    def decays(t, carry):
        rows = pl.ds(pl.multiple_of(t * TM, TM), TM)
        lr = lr_ref[0, rows, :].astype(BF16)
        for d in range(2):
            z = _mm(lr, wa2_ref[d]) + ba_ref[d]
            g = (jnp.minimum(z, 0.0) - jnp.log1p(jnp.exp(-jnp.abs(z)))) / GLA_NORMALIZER
            wide = jnp.concatenate([g[i * c:(i + 1) * c] for i in range(per_tile)], axis=1)
            hi, mid, lo = _split3(wide)
            cum = _mm(tri[d], hi) + _mm(tri[d], mid) + _mm(tri[d], lo)
            cum_ref[d, rows, :] = jnp.concatenate([cum[:, i * 128:(i + 1) * 128] for i in range(per_tile)], axis=0)
        return carry

    lax.fori_loop(0, TILES, decays, 0)

    def intra(ci, carry):
        rows = pl.ds(pl.multiple_of(ci * c, c), c)
        qc = q_ref[0, rows, :]
        kc = k_ref[0, rows, :]
        v2 = v_ref[0, rows, :].astype(BF16)
        o = [None, None]
        for d in range(2):
            cum = cum_ref[d, rows, :]
            tot = cum[c - 1:c, :] if d == 0 else cum[0:1, :]
            q_t = qc * jnp.exp(cum)
            k_t = (kc * jnp.exp(-cum)).astype(BF16)
            k_end = (kc * jnp.exp(tot - cum)).astype(BF16)
            dec_ref[d, ci] = jnp.exp(tot)
            qt_ref[d, rows, :] = q_t.astype(BF16)
            uu = _tn(v2, k_end)
            u_ref[d, ci] = jnp.where(head0, uu[:GLA_DV], uu[GLA_DV:])
            for h in range(2):
                q_h = jnp.where(head0 if h == 0 else ~head0, q_t, 0.0).astype(BF16)
                a = jnp.where(incl[d], _nt(q_h, k_t), 0.0).astype(BF16)
                oh = _mm(a, v2[:, h * GLA_DV:(h + 1) * GLA_DV])
                o[h] = oh if o[h] is None else o[h] + oh
        acc_ref[rows, :] = jnp.concatenate(o, axis=1)
        return carry

    lax.fori_loop(0, GLA_CHUNKS, intra, 0, unroll=4)

    st_ref[...] = jnp.zeros_like(st_ref)

    def scan(i, carry):
        order = (jnp.where(i < GLA_CTX_CHUNKS, GLA_CHUNKS - GLA_CTX_CHUNKS + i, i - GLA_CTX_CHUNKS),
                 GLA_CHUNKS - 1 - i)
        for d in range(2):
            s = st_ref[d]
            sp_ref[d, order[d]] = s.astype(BF16)
            st_ref[d] = dec_ref[d, order[d]] * s + u_ref[d, order[d]]
        return carry

    lax.fori_loop(0, GLA_CHUNKS, scan, 0)

    def inter(ci, carry):
        rows = pl.ds(pl.multiple_of(ci * c, c), c)
        o = [None, None]
        for d in range(2):
            q_t = qt_ref[d, rows, :]
            s = sp_ref[d, ci]
            for h in range(2):
                q_h = jnp.where(head0 if h == 0 else ~head0, q_t, jnp.zeros_like(q_t))
                oh = _nt(q_h, s)
                o[h] = oh if o[h] is None else o[h] + oh
        acc_ref[rows, :] = acc_ref[rows, :] + jnp.concatenate(o, axis=1)
        return carry

    lax.fori_loop(0, GLA_CHUNKS, inter, 0, unroll=4)

    def finish(t, carry):
        rows = pl.ds(pl.multiple_of(t * TM, TM), TM)
        gate = gate_ref[0, rows, :]
        sw = gate * jax.nn.sigmoid(gate)
        for h in range(2):
            vs = slice(h * GLA_DV, (h + 1) * GLA_DV)
            o = acc_ref[rows, vs]
            o = o * lax.rsqrt(jnp.mean(o * o, axis=-1, keepdims=True) + EPS)
            o_ref[0, rows, vs] = (o * gn_ref[:, vs] * sw[:, vs]).astype(BF16)
        return carry

    lax.fori_loop(0, TILES, finish, 0)


def _gla(gla_in, wa2, ba, gnorm):
    bsz = gla_in.shape[0]
    pairs = GLA_HEADS // 2
    qk_blocks = GLA_QK_WIDTH // 128
    v_blocks = GLA_V_WIDTH // 256
    v0 = 2 * GLA_QK_WIDTH // 256
    lr_block = (2 * GLA_QK_WIDTH + 2 * GLA_V_WIDTH) // 128
    wa2_rows = jnp.zeros((2, 128, GLA_QK_WIDTH), F32)
    for d in range(2):
        wa2_rows = wa2_rows.at[d, d * GLA_RANK:(d + 1) * GLA_RANK].set(wa2[d])
    wa2 = wa2_rows.astype(BF16)
    return pl.pallas_call(
        _gla_kernel,
        grid=(bsz, pairs),
        in_specs=[
            pl.BlockSpec((1, TOK, 128), lambda b, p: (b, 0, p)),
            pl.BlockSpec((1, TOK, 128), lambda b, p: (b, 0, qk_blocks + p)),
            pl.BlockSpec((1, TOK, 256), lambda b, p: (b, 0, v0 + p)),
            pl.BlockSpec((1, TOK, 256), lambda b, p: (b, 0, v0 + v_blocks + p)),
            pl.BlockSpec((1, TOK, 128), lambda b, p: (b, 0, lr_block)),
            pl.BlockSpec((2, 128, 128), lambda b, p: (0, 0, p)),
            pl.BlockSpec((2, 1, 128), lambda b, p: (0, 0, p)),
            pl.BlockSpec((1, 256), lambda b, p: (0, p)),
        ],
        out_specs=pl.BlockSpec((1, TOK, 256), lambda b, p: (b, 0, p)),
        out_shape=jax.ShapeDtypeStruct((bsz, TOK, GLA_V_WIDTH), BF16),
        scratch_shapes=[
            pltpu.VMEM((TOK, 2 * GLA_DV), F32),
            pltpu.VMEM((2, TOK, 128), F32),
            pltpu.VMEM((2, TOK, 128), BF16),
            pltpu.VMEM((2, GLA_CHUNKS, GLA_DV, 128), F32),
            pltpu.VMEM((2, GLA_CHUNKS, 1, 128), F32),
            pltpu.VMEM((2, GLA_CHUNKS, GLA_DV, 128), BF16),
            pltpu.VMEM((2, GLA_DV, 128), F32),
        ],
        compiler_params=_params("parallel", "parallel"),
    )(gla_in, gla_in, gla_in, gla_in, gla_in, wa2, ba.reshape(2, 1, GLA_QK_WIDTH), gnorm.reshape(1, GLA_V_WIDTH))


def _swa_kernel(sink_ref, q_ref, qr_ref, k_ref, v_ref, o_ref):
    kp = pl.program_id(1)
    qb = pl.program_id(2)
    kc = k_ref[0, SEQ:, :]
    vc = v_ref[0, SEQ:, :]
    heads_per_step = 2 * SWA_GROUP

    def head_out(h, s_list, v_list):
        sink = sink_ref[kp * heads_per_step + h]
        m = jnp.max(s_list[0], axis=-1, keepdims=True)
        for s in s_list[1:]:
            m = jnp.maximum(m, jnp.max(s, axis=-1, keepdims=True))
        m = jnp.maximum(m, sink)
        den = jnp.exp(sink - m)
        o = None
        for s, v in zip(s_list, v_list):
            p = jnp.exp(s - m)
            den = den + jnp.sum(p, axis=-1, keepdims=True)
            pv = _mm(p.astype(BF16), v)
            o = pv if o is None else o + pv
        return o / den

    @pl.when(qb < LAT_TILES)
    def _latent():
        q0 = qb * SWA_TQ
        start = pl.multiple_of(jnp.clip(q0 - SWA_WINDOW, 0, SEQ - SWA_NLOC), SWA_WINDOW)
        kl = k_ref[0, pl.ds(start, SWA_NLOC), :]
        vl = v_ref[0, pl.ds(start, SWA_NLOC), :]
        qpos = q0 + lax.broadcasted_iota(jnp.int32, (SWA_TQ, SWA_NLOC), 0)
        kpos = start + lax.broadcasted_iota(jnp.int32, (SWA_TQ, SWA_NLOC), 1)
        ok = jnp.abs(kpos - qpos) <= SWA_WINDOW
        q = q_ref[0]
        qr = qr_ref[0]
        outs = []
        for h in range(heads_per_step):
            sl = slice(h * HEAD_DIM, (h + 1) * HEAD_DIM)
            kv = slice((h // SWA_GROUP) * HEAD_DIM, (h // SWA_GROUP + 1) * HEAD_DIM)
            s_loc = jnp.where(ok, _nt(qr[:, sl], kl[:, kv]), NEG_INF)
            s_cx = _nt(q[:, sl], kc[:, kv])
            outs.append(head_out(h, [s_loc, s_cx], [vl[:, kv], vc[:, kv]]))
        o_ref[0] = jnp.concatenate(outs, axis=-1).astype(BF16)

    @pl.when(qb == LAT_TILES)
    def _context():
        q = q_ref[0]
        outs = []
        for h in range(heads_per_step):
            sl = slice(h * HEAD_DIM, (h + 1) * HEAD_DIM)
            kv = slice((h // SWA_GROUP) * HEAD_DIM, (h // SWA_GROUP + 1) * HEAD_DIM)
            outs.append(head_out(h, [_nt(q[:, sl], kc[:, kv])], [vc[:, kv]]))
        o_ref[0] = jnp.concatenate(outs, axis=-1).astype(BF16)


def _swa_attention(q, qr, kr, v, sink):
    bsz = q.shape[0]
    pairs = SWA_KV_HEADS // 2
    qw = 2 * SWA_GROUP * HEAD_DIM
    return pl.pallas_call(
        _swa_kernel,
        grid=(bsz, pairs, TILES),
        in_specs=[
            pl.BlockSpec(memory_space=pltpu.SMEM),
            pl.BlockSpec((1, SWA_TQ, qw), lambda b, p, t: (b, t, p)),
            pl.BlockSpec((1, SWA_TQ, qw), lambda b, p, t: (b, t, p)),
            pl.BlockSpec((1, TOK, 128), lambda b, p, t: (b, 0, p)),
            pl.BlockSpec((1, TOK, 128), lambda b, p, t: (b, 0, p)),
        ],
        out_specs=pl.BlockSpec((1, SWA_TQ, qw), lambda b, p, t: (b, t, p)),
        out_shape=jax.ShapeDtypeStruct((bsz, TOK, D_MODEL), BF16),
        compiler_params=_params("parallel", "parallel", "arbitrary"),
    )(sink.astype(F32), q, qr, kr, v)


def _mlp_kernel(x_ref, oa_ref, ob_ref, mod_ref, g_ref, gf_ref, wo_ref, w1_ref, w2_ref, out_ref, *, final_norm):
    mod = mod_ref[0]
    half = wo_ref.shape[0] // 2
    y = _mm(oa_ref[0], wo_ref[:half, :]) + _mm(ob_ref[0], wo_ref[half:, :])
    x1 = x_ref[0] + mod[2:3] * y
    h = _norm_modulate(x1, g_ref[...], mod[3:4], mod[4:5]).astype(BF16)
    acc = jnp.zeros((TM, D_MODEL), F32)
    for c in range(D_FF // FF_CHUNK):
        t = jnp.maximum(_mm(h, w1_ref[:, c * FF_CHUNK:(c + 1) * FF_CHUNK]), 0.0)
        acc = acc + _mm((t * t).astype(BF16), w2_ref[c * FF_CHUNK:(c + 1) * FF_CHUNK, :])
    x2 = x1 + mod[5:6] * acc
    if final_norm:
        x2 = x2 * lax.rsqrt(jnp.mean(x2 * x2, axis=-1, keepdims=True) + EPS) * gf_ref[...]
    out_ref[0] = x2


def _outproj_mlp(xs, oa, ob, ob_block, mod, gain, gain_final, wo, w1, w2, final_norm):
    bsz = xs.shape[0]
    half = D_MODEL // 2
    tiles = LAT_TILES if final_norm else TILES
    tile = lambda b, t: (b, t, 0)
    const = lambda b, t: (0, 0)
    return pl.pallas_call(
        functools.partial(_mlp_kernel, final_norm=final_norm),
        grid=(bsz, tiles),
        in_specs=[
            pl.BlockSpec((1, TM, D_MODEL), tile),
            pl.BlockSpec((1, TM, half), tile),
            pl.BlockSpec((1, TM, half), lambda b, t: (b, t, ob_block)),
            pl.BlockSpec((1, 6, D_MODEL), lambda b, t: (_mod_row(b, t), 0, 0)),
            pl.BlockSpec((1, D_MODEL), const),
            pl.BlockSpec((1, D_MODEL), const),
            pl.BlockSpec((D_MODEL, D_MODEL), const),
            pl.BlockSpec((D_MODEL, D_FF), const),
            pl.BlockSpec((D_FF, D_MODEL), const),
        ],
        out_specs=pl.BlockSpec((1, TM, D_MODEL), tile),
        out_shape=jax.ShapeDtypeStruct((bsz, tiles * TM, D_MODEL), F32),
        compiler_params=_params("parallel", "parallel"),
    )(xs, oa, ob, mod, gain, gain_final, wo, w1, w2)


def _even_in_weight(w):
    n_na = 3 * NA_WIDTH
    scale = np.ones((w.shape[1],), np.float32)
    scale[:NA_WIDTH] = HEAD_DIM ** -0.5
    scale[n_na:n_na + GLA_QK_WIDTH] = GLA_DK ** -0.5
    w = w * jnp.asarray(scale)
    pad = n_na + GLA_IN_WIDTH - w.shape[1]
    return jnp.pad(w, ((0, 0), (0, pad))).astype(BF16)


def _odd_in_weight(w):
    n_rot = D_MODEL + SWA_KV_WIDTH
    rot = w[:, :n_rot].reshape(D_MODEL, n_rot // HEAD_DIM, HEAD_DIM // 2, 2)
    rot = jnp.swapaxes(rot, 2, 3).reshape(D_MODEL, n_rot)
    scale = np.ones((n_rot,), np.float32)
    scale[:D_MODEL] = HEAD_DIM ** -0.5
    return jnp.concatenate([rot * jnp.asarray(scale), w[:, n_rot:]], axis=1).astype(BF16)


def kernel(x, c, ctx, c_ctx, ada_w, ada_b, norm_mix, norm_mlp, mlp_w1, mlp_w2, ab_w_in, ab_w_out, na_rpb,
           gla_wa2, gla_ba, gla_gnorm, swa_w_in, swa_w_out, swa_sink, norm_final):
    bsz = x.shape[0]
    assert x.shape == (bsz, SEQ, D_MODEL) and ctx.shape == (bsz, CTX_LEN, D_MODEL) and bsz <= 8

    cvec = jnp.zeros((MOD_ROWS, D_MODEL), F32).at[:bsz].set(c).at[8].set(c_ctx)
    mods = _ada_table(cvec, ada_w, ada_b).reshape(DEPTH, MOD_ROWS, 6, D_MODEL)
    cos_t, sin_t = _rope_tables()
    gain_final = norm_final.reshape(1, D_MODEL)

    xs = jnp.concatenate([x, ctx], axis=1)
    for l in range(DEPTH):
        j = l // 2
        mod = mods[l]
        g_mix = norm_mix[l].reshape(1, D_MODEL)
        g_mlp = norm_mlp[l].reshape(1, D_MODEL)
        if l % 2 == 0:
            na_in, gla_in = _inproj_even(xs, mod, g_mix, _even_in_weight(ab_w_in[j]))
            oa = _na_attention(na_in, _na_bias_table(na_rpb[j]))
            ob = _gla(gla_in, gla_wa2[j], gla_ba[j], gla_gnorm[j])
            ob_block = 0
            wo = ab_w_out[j].astype(BF16)
        else:
            q, qr, kr, v = _inproj_odd(xs, mod, g_mix, _odd_in_weight(swa_w_in[j]), cos_t, sin_t)
            oa = ob = _swa_attention(q, qr, kr, v, swa_sink[j])
            ob_block = 1
            wo = swa_w_out[j].astype(BF16)
        xs = _outproj_mlp(xs, oa, ob, ob_block, mod, g_mlp, gain_final, wo,
                          mlp_w1[l].astype(BF16), mlp_w2[l].astype(BF16), l == DEPTH - 1)
    return xs
```

```python
import functools

import numpy as np
import jax
import jax.numpy as jnp
from jax import lax
from jax.experimental import pallas as pl
from jax.experimental.pallas import tpu as pltpu

D_MODEL = 1024
SEQ = 2048
DEPTH = 4
GRID_W = 64
GRID_ROWS = SEQ // GRID_W
CTX_LEN = 256
TOK = SEQ + CTX_LEN
HEAD_DIM = 64
EPS = 1e-6
NEG_INF = -1e30
LOG2E = 1.4426950408889634

NA_HEADS = 8
NA_WIN_ROWS = 8
NA_WIN_COLS = 16
NA_WIDTH = NA_HEADS * HEAD_DIM
NA_Q_ROWS = 4
NA_K_ROWS = NA_Q_ROWS + NA_WIN_ROWS - 1
NA_QN = NA_Q_ROWS * GRID_W
NA_KN = NA_K_ROWS * GRID_W
NA_ROW_BLOCKS = GRID_ROWS // NA_Q_ROWS

GLA_HEADS = 4
GLA_DK = 64
GLA_DV = 128
GLA_RANK = 16
GLA_NORMALIZER = 16.0
GLA_CHUNK = 64
GLA_QK_WIDTH = GLA_HEADS * GLA_DK
GLA_V_WIDTH = GLA_HEADS * GLA_DV
GLA_IN_WIDTH = 2 * GLA_QK_WIDTH + 2 * GLA_V_WIDTH + 128
GLA_CTX_CHUNKS = CTX_LEN // GLA_CHUNK
GLA_CHUNKS = TOK // GLA_CHUNK

SWA_HEADS = 16
SWA_KV_HEADS = 4
SWA_GROUP = SWA_HEADS // SWA_KV_HEADS
SWA_WINDOW = 128
SWA_KV_WIDTH = SWA_KV_HEADS * HEAD_DIM
SWA_TQ = 256
SWA_NLOC = SWA_TQ + 2 * SWA_WINDOW

D_FF = 4 * D_MODEL
FF_CHUNK = 1024
ROPE_THETA = 10000.0

TM = 256
TILES = TOK // TM
LAT_TILES = SEQ // TM
MOD_ROWS = 16
VMEM_LIMIT = 56 * 1024 * 1024

F32 = jnp.float32
BF16 = jnp.bfloat16


def _nt(a, b):
    return lax.dot_general(a, b, (((1,), (1,)), ((), ())), preferred_element_type=F32)


def _tn(a, b):
    return lax.dot_general(a, b, (((0,), (0,)), ((), ())), preferred_element_type=F32)


def _mm(a, b):
    return jnp.dot(a, b, preferred_element_type=F32)


def _params(*sem):
    return pltpu.CompilerParams(dimension_semantics=sem, vmem_limit_bytes=VMEM_LIMIT)


def _mod_row(b, t):
    return jnp.where(t < LAT_TILES, b, 8)


def _ada_kernel(c_ref, w_ref, b_ref, o_ref):
    s = c_ref[...]
    s = s * jax.nn.sigmoid(s)
    o_ref[0] = _mm(s.astype(BF16), w_ref[0].astype(BF16)) + b_ref[0]


def _ada_table(cvec, ada_w, ada_b):
    nb = 6 * D_MODEL // 1024
    return pl.pallas_call(
        _ada_kernel,
        grid=(DEPTH, nb),
        in_specs=[
            pl.BlockSpec((MOD_ROWS, D_MODEL), lambda l, n: (0, 0)),
            pl.BlockSpec((1, D_MODEL, 1024), lambda l, n: (l, 0, n)),
            pl.BlockSpec((1, 1, 1024), lambda l, n: (l, 0, n)),
        ],
        out_specs=pl.BlockSpec((1, MOD_ROWS, 1024), lambda l, n: (l, 0, n)),
        out_shape=jax.ShapeDtypeStruct((DEPTH, MOD_ROWS, 6 * D_MODEL), F32),
        compiler_params=_params("parallel", "parallel"),
    )(cvec, ada_w, ada_b.reshape(DEPTH, 1, 6 * D_MODEL))


def _norm_modulate(x, gain, shift, scale):
    y = x * lax.rsqrt(jnp.mean(x * x, axis=-1, keepdims=True) + EPS) * gain
    return y * (1.0 + scale) + shift


def _inproj_even_kernel(x_ref, mod_ref, g_ref, w_ref, na_ref, gla_ref):
    mod = mod_ref[0]
    h = _norm_modulate(x_ref[0], g_ref[...], mod[0:1], mod[1:2]).astype(BF16)
    n_na = 3 * NA_WIDTH
    na_ref[0] = _mm(h, w_ref[:, :n_na]).astype(BF16)
    gla_ref[0] = _mm(h, w_ref[:, n_na:])


def _inproj_even(xs, mod, gain, w):
    bsz = xs.shape[0]
    n_na = 3 * NA_WIDTH
    return pl.pallas_call(
        _inproj_even_kernel,
        grid=(bsz, TILES),
        in_specs=[
            pl.BlockSpec((1, TM, D_MODEL), lambda b, t: (b, t, 0)),
            pl.BlockSpec((1, 6, D_MODEL), lambda b, t: (_mod_row(b, t), 0, 0)),
            pl.BlockSpec((1, D_MODEL), lambda b, t: (0, 0)),
            pl.BlockSpec((D_MODEL, n_na + GLA_IN_WIDTH), lambda b, t: (0, 0)),
        ],
        out_specs=[
            pl.BlockSpec((1, TM, n_na), lambda b, t: (b, t, 0)),
            pl.BlockSpec((1, TM, GLA_IN_WIDTH), lambda b, t: (b, t, 0)),
        ],
        out_shape=[
            jax.ShapeDtypeStruct((bsz, TOK, n_na), BF16),
            jax.ShapeDtypeStruct((bsz, TOK, GLA_IN_WIDTH), F32),
        ],
        compiler_params=_params("parallel", "parallel"),
    )(xs, mod, gain, w)


def _rope(a, cos, sin, first_half):
    swapped = jnp.where(first_half, pltpu.roll(a, 96, 1), pltpu.roll(a, 32, 1))
    return a * cos + swapped * sin


def _inproj_odd_kernel(x_ref, mod_ref, g_ref, w_ref, cos_ref, sin_ref, q_ref, qr_ref, kr_ref, v_ref):
    mod = mod_ref[0]
    h = _norm_modulate(x_ref[0], g_ref[...], mod[0:1], mod[1:2]).astype(BF16)
    cos = cos_ref[...]
    sin = sin_ref[...]
    first_half = (lax.broadcasted_iota(jnp.int32, (TM, 128), 1) % HEAD_DIM) < HEAD_DIM // 2
    for j in range(D_MODEL // 128):
        a = _mm(h, w_ref[:, j * 128:(j + 1) * 128])
        q_ref[0, :, j * 128:(j + 1) * 128] = a.astype(BF16)
        qr_ref[0, :, j * 128:(j + 1) * 128] = _rope(a, cos, sin, first_half).astype(BF16)
    for j in range(SWA_KV_WIDTH // 128):
        c0 = D_MODEL + j * 128
        a = _mm(h, w_ref[:, c0:c0 + 128])
        kr_ref[0, :, j * 128:(j + 1) * 128] = _rope(a, cos, sin, first_half).astype(BF16)
    v_ref[0] = _mm(h, w_ref[:, D_MODEL + SWA_KV_WIDTH:]).astype(BF16)


def _inproj_odd(xs, mod, gain, w, cos_t, sin_t):
    bsz = xs.shape[0]
    n_in = D_MODEL + 2 * SWA_KV_WIDTH
    tile = lambda b, t: (b, t, 0)
    return pl.pallas_call(
        _inproj_odd_kernel,
        grid=(bsz, TILES),
        in_specs=[
            pl.BlockSpec((1, TM, D_MODEL), tile),
            pl.BlockSpec((1, 6, D_MODEL), lambda b, t: (_mod_row(b, t), 0, 0)),
            pl.BlockSpec((1, D_MODEL), lambda b, t: (0, 0)),
            pl.BlockSpec((D_MODEL, n_in), lambda b, t: (0, 0)),
            pl.BlockSpec((TM, 128), lambda b, t: (t, 0)),
            pl.BlockSpec((TM, 128), lambda b, t: (t, 0)),
        ],
        out_specs=[
            pl.BlockSpec((1, TM, D_MODEL), tile),
            pl.BlockSpec((1, TM, D_MODEL), tile),
            pl.BlockSpec((1, TM, SWA_KV_WIDTH), tile),
            pl.BlockSpec((1, TM, SWA_KV_WIDTH), tile),
        ],
        out_shape=[
            jax.ShapeDtypeStruct((bsz, TOK, D_MODEL), BF16),
            jax.ShapeDtypeStruct((bsz, TOK, D_MODEL), BF16),
            jax.ShapeDtypeStruct((bsz, TOK, SWA_KV_WIDTH), BF16),
            jax.ShapeDtypeStruct((bsz, TOK, SWA_KV_WIDTH), BF16),
        ],
        compiler_params=_params("parallel", "parallel"),
    )(xs, mod, gain, w, cos_t, sin_t)


def _rope_tables():
    t = np.arange(SEQ)
    n_freq = HEAD_DIM // 4
    inv = jnp.asarray(ROPE_THETA, F32) ** (-jnp.arange(n_freq, dtype=F32) / n_freq)
    row = jnp.asarray(t // GRID_W, F32)
    col = jnp.asarray(t % GRID_W, F32)
    ang = jnp.concatenate([row[:, None] * inv, col[:, None] * inv], axis=-1)
    cos, sin = jnp.cos(ang), jnp.sin(ang)
    cos_t = jnp.tile(cos, (1, 4))
    sin_t = jnp.tile(jnp.concatenate([-sin, sin], axis=-1), (1, 2))
    cos_t = jnp.concatenate([cos_t, jnp.ones((CTX_LEN, 128), F32)], axis=0)
    sin_t = jnp.concatenate([sin_t, jnp.zeros((CTX_LEN, 128), F32)], axis=0)
    return cos_t, sin_t


_NA_Q_ROW0 = np.array([0, NA_Q_ROWS, GRID_ROWS - NA_Q_ROWS])
_NA_K_ROW0 = np.clip(_NA_Q_ROW0 - NA_WIN_ROWS // 2, 0, GRID_ROWS - NA_K_ROWS)


def _na_window_mask():
    qi = np.arange(NA_QN)
    kj = np.arange(NA_KN)
    r = _NA_Q_ROW0[:, None] + qi[None, :] // GRID_W
    c = qi % GRID_W
    kr = _NA_K_ROW0[:, None] + kj[None, :] // GRID_W
    kc = kj % GRID_W
    r0 = np.clip(r - NA_WIN_ROWS // 2, 0, GRID_ROWS - NA_WIN_ROWS)
    row_ok = (kr[:, None, :] >= r0[:, :, None]) & (kr[:, None, :] < r0[:, :, None] + NA_WIN_ROWS)
    w0 = np.clip(c - NA_WIN_COLS // 2, 0, GRID_W - NA_WIN_COLS)
    col_ok = (kc[None, :] >= w0[:, None]) & (kc[None, :] < w0[:, None] + NA_WIN_COLS)
    return row_ok & col_ok[None]


_NA_BIAS_OK = _na_window_mask()


def _na_bias_table(rpb):
    n = GRID_W
    lead = n - NA_WIN_COLS
    w = jnp.pad(rpb.astype(F32), ((0, 0), (0, 0), (lead, 2 * n - lead - (2 * NA_WIN_COLS - 1))))
    col = jnp.tile(w, (1, 1, n))[..., n - 1:n - 1 + n * (2 * n - 1)]
    col = col.reshape(NA_HEADS, 2 * NA_WIN_ROWS - 1, n, 2 * n - 1)[..., :n]
    lo = NA_Q_ROWS - 1
    colp = jnp.pad(col, ((0, 0), (lo, NA_K_ROWS), (0, 0), (0, 0)))
    blocks = []
    for cls in range(3):
        off = int(_NA_K_ROW0[cls] - _NA_Q_ROW0[cls]) + NA_WIN_ROWS - 1
        for i in range(NA_Q_ROWS):
            s0 = off - i + lo
            blocks.append(colp[:, s0:s0 + NA_K_ROWS])
    g = jnp.stack(blocks, axis=1).reshape(NA_HEADS, 3, NA_Q_ROWS, NA_K_ROWS, n, n)
    g = g.transpose(0, 1, 2, 4, 3, 5).reshape(NA_HEADS, 3, NA_QN, NA_KN)
    return jnp.where(jnp.asarray(_NA_BIAS_OK)[None], g * LOG2E, NEG_INF)


def _pair_softmax_pv(qs, keys, values, biases, floor):
    lane = lax.broadcasted_iota(jnp.int32, (1, 128), 1)
    half = (lane < HEAD_DIM, lane >= HEAD_DIM)
    res = []
    for h in range(2):
        s = [_nt(jnp.where(half[h], q, jnp.zeros_like(q)), k) for q, k in zip(qs, keys)]
        if biases[h] is not None:
            s = [x if b is None else x + b for x, b in zip(s, biases[h])]
        m = jnp.max(s[0], axis=-1, keepdims=True)
        for x in s[1:]:
            m = jnp.maximum(m, jnp.max(x, axis=-1, keepdims=True))
        if floor[h] is not None:
            m = jnp.maximum(m, floor[h])
        acc = None
        for x, v in zip(s, values):
            p = jnp.exp2((x - m).astype(BF16))
            va = jnp.where(half[h], v, jnp.ones_like(v))
            pv = _mm(p, va)
            acc = pv if acc is None else acc + pv
        den = pltpu.roll(acc, HEAD_DIM, 1)
        if floor[h] is not None:
            den = den + jnp.exp2(floor[h] - m)
        res.append(acc / den)
    return jnp.where(half[0], res[0], res[1])


def _na_kernel(q_ref, k_ref, v_ref, bias_ref, o_ref):
    rb = pl.program_id(2)
    n_blocks = q_ref.shape[2] // 128

    @pl.when(rb < NA_ROW_BLOCKS)
    def _latent():
        k_row0 = jnp.clip(rb * NA_Q_ROWS - NA_WIN_ROWS // 2, 0, GRID_ROWS - NA_K_ROWS)
        start = pl.multiple_of(k_row0 * GRID_W, GRID_W)
        cls = jnp.where(rb == 0, 0, jnp.where(rb == NA_ROW_BLOCKS - 1, 2, 1))
        for j in range(n_blocks):
            ln = slice(j * 128, (j + 1) * 128)
            keys = [k_ref[0, pl.ds(start, NA_KN), ln], k_ref[0, SEQ:, ln]]
            values = [v_ref[0, pl.ds(start, NA_KN), ln], v_ref[0, SEQ:, ln]]
            biases = [[bias_ref[0, (2 * j + h) * 3 + cls], None] for h in range(2)]
            q = q_ref[0, :, ln]
            o_ref[0, :, ln] = _pair_softmax_pv([q, q], keys, values, biases, [None, None]).astype(BF16)

    @pl.when(rb == NA_ROW_BLOCKS)
    def _context():
        for j in range(n_blocks):
            ln = slice(j * 128, (j + 1) * 128)
            o_ref[0, :, ln] = _pair_softmax_pv([q_ref[0, :, ln]], [k_ref[0, SEQ:, ln]], [v_ref[0, SEQ:, ln]],
                                               [None, None], [None, None]).astype(BF16)


def _na_attention(qkv, bias):
    bsz = qkv.shape[0]
    hps = 4
    groups = NA_HEADS // hps
    w = hps * HEAD_DIM
    return pl.pallas_call(
        _na_kernel,
        grid=(groups, bsz, NA_ROW_BLOCKS + 1),
        in_specs=[
            pl.BlockSpec((1, NA_QN, w), lambda p, b, r: (b, r, p)),
            pl.BlockSpec((1, TOK, w), lambda p, b, r: (b, 0, groups + p)),
            pl.BlockSpec((1, TOK, w), lambda p, b, r: (b, 0, 2 * groups + p)),
            pl.BlockSpec((1, 3 * hps, NA_QN, NA_KN), lambda p, b, r: (p, 0, 0, 0)),
        ],
        out_specs=pl.BlockSpec((1, NA_QN, w), lambda p, b, r: (b, r, p)),
        out_shape=jax.ShapeDtypeStruct((bsz, TOK, NA_WIDTH), BF16),
        compiler_params=_params("parallel", "parallel", "arbitrary"),
    )(qkv, qkv, qkv, bias.reshape(groups, 3 * hps, NA_QN, NA_KN))


def _split3(g):
    hi = g.astype(BF16)
    r1 = g - hi.astype(F32)
    mid = r1.astype(BF16)
    lo = (r1 - mid.astype(F32)).astype(BF16)
    return hi, mid, lo


def _gla_kernel(q_ref, k_ref, v_ref, gate_ref, lr_ref, wa2_ref, ba_ref, gn_ref, o_ref,
                acc_ref, cum_ref, qt_ref, u_ref, dec_ref, sp_ref, st_ref):
    c = GLA_CHUNK
    ii = lax.broadcasted_iota(jnp.int32, (c, c), 0)
    jj = lax.broadcasted_iota(jnp.int32, (c, c), 1)
    incl = (jj <= ii, jj >= ii)
    tri = tuple(jnp.where(m, 1.0, 0.0).astype(BF16) for m in incl)
    head0 = lax.broadcasted_iota(jnp.int32, (1, 128), 1) < GLA_DK
    per_tile = TM // c

    def decays(t, carry):
        rows = pl.ds(pl.multiple_of(t * TM, TM), TM)
        lr = lr_ref[0, rows, :].astype(BF16)
        for d in range(2):
            z = _mm(lr, wa2_ref[d]) + ba_ref[d]
            g = (jnp.minimum(z, 0.0) - jnp.log1p(jnp.exp(-jnp.abs(z)))) / GLA_NORMALIZER
            wide = jnp.concatenate([g[i * c:(i + 1) * c] for i in range(per_tile)], axis=1)
            hi, mid, lo = _split3(wide)
            cum = _mm(tri[d], hi) + _mm(tri[d], mid) + _mm(tri[d], lo)
            cum_ref[d, rows, :] = jnp.concatenate([cum[:, i * 128:(i + 1) * 128] for i in range(per_tile)], axis=0)
        return carry

    lax.fori_loop(0, TILES, decays, 0)

    def intra(ci, carry):
        rows = pl.ds(pl.multiple_of(ci * c, c), c)
        qc = q_ref[0, rows, :]
        kc = k_ref[0, rows, :]
        v2 = v_ref[0, rows, :].astype(BF16)
        o = [None, None]
        for d in range(2):
            cum = cum_ref[d, rows, :]
            tot = cum[c - 1:c, :] if d == 0 else cum[0:1, :]
            q_t = qc * jnp.exp(cum)
            k_t = (kc * jnp.exp(-cum)).astype(BF16)
            k_end = (kc * jnp.exp(tot - cum)).astype(BF16)
            dec_ref[d, ci] = jnp.exp(tot)
            qt_ref[d, rows, :] = q_t.astype(BF16)
            uu = _tn(v2, k_end)
            u_ref[d, ci] = jnp.where(head0, uu[:GLA_DV], uu[GLA_DV:])
            for h in range(2):
                q_h = jnp.where(head0 if h == 0 else ~head0, q_t, 0.0).astype(BF16)
                a = jnp.where(incl[d], _nt(q_h, k_t), 0.0).astype(BF16)
                oh = _mm(a, v2[:, h * GLA_DV:(h + 1) * GLA_DV])
                o[h] = oh if o[h] is None else o[h] + oh
        acc_ref[rows, :] = jnp.concatenate(o, axis=1)
        return carry

    lax.fori_loop(0, GLA_CHUNKS, intra, 0, unroll=4)

    st_ref[...] = jnp.zeros_like(st_ref)

    def scan(i, carry):
        order = (jnp.where(i < GLA_CTX_CHUNKS, GLA_CHUNKS - GLA_CTX_CHUNKS + i, i - GLA_CTX_CHUNKS),
                 GLA_CHUNKS - 1 - i)
        for d in range(2):
            s = st_ref[d]
            sp_ref[d, order[d]] = s.astype(BF16)
            st_ref[d] = dec_ref[d, order[d]] * s + u_ref[d, order[d]]
        return carry

    lax.fori_loop(0, GLA_CHUNKS, scan, 0)

    def inter(ci, carry):
        rows = pl.ds(pl.multiple_of(ci * c, c), c)
        o = [None, None]
        for d in range(2):
            q_t = qt_ref[d, rows, :]
            s = sp_ref[d, ci]
            for h in range(2):
                q_h = jnp.where(head0 if h == 0 else ~head0, q_t, jnp.zeros_like(q_t))
                oh = _nt(q_h, s)
                o[h] = oh if o[h] is None else o[h] + oh
        acc_ref[rows, :] = acc_ref[rows, :] + jnp.concatenate(o, axis=1)
        return carry

    lax.fori_loop(0, GLA_CHUNKS, inter, 0, unroll=4)

    def finish(t, carry):
        rows = pl.ds(pl.multiple_of(t * TM, TM), TM)
        gate = gate_ref[0, rows, :]
        sw = gate * jax.nn.sigmoid(gate)
        for h in range(2):
            vs = slice(h * GLA_DV, (h + 1) * GLA_DV)
            o = acc_ref[rows, vs]
            o = o * lax.rsqrt(jnp.mean(o * o, axis=-1, keepdims=True) + EPS)
            o_ref[0, rows, vs] = (o * gn_ref[:, vs] * sw[:, vs]).astype(BF16)
        return carry

    lax.fori_loop(0, TILES, finish, 0)


def _gla(gla_in, wa2, ba, gnorm):
    bsz = gla_in.shape[0]
    pairs = GLA_HEADS // 2
    qk_blocks = GLA_QK_WIDTH // 128
    v_blocks = GLA_V_WIDTH // 256
    v0 = 2 * GLA_QK_WIDTH // 256
    lr_block = (2 * GLA_QK_WIDTH + 2 * GLA_V_WIDTH) // 128
    wa2_rows = jnp.zeros((2, 128, GLA_QK_WIDTH), F32)
    for d in range(2):
        wa2_rows = wa2_rows.at[d, d * GLA_RANK:(d + 1) * GLA_RANK].set(wa2[d])
    wa2 = wa2_rows.astype(BF16)
    return pl.pallas_call(
        _gla_kernel,
        grid=(bsz, pairs),
        in_specs=[
            pl.BlockSpec((1, TOK, 128), lambda b, p: (b, 0, p)),
            pl.BlockSpec((1, TOK, 128), lambda b, p: (b, 0, qk_blocks + p)),
            pl.BlockSpec((1, TOK, 256), lambda b, p: (b, 0, v0 + p)),
            pl.BlockSpec((1, TOK, 256), lambda b, p: (b, 0, v0 + v_blocks + p)),
            pl.BlockSpec((1, TOK, 128), lambda b, p: (b, 0, lr_block)),
            pl.BlockSpec((2, 128, 128), lambda b, p: (0, 0, p)),
            pl.BlockSpec((2, 1, 128), lambda b, p: (0, 0, p)),
            pl.BlockSpec((1, 256), lambda b, p: (0, p)),
        ],
        out_specs=pl.BlockSpec((1, TOK, 256), lambda b, p: (b, 0, p)),
        out_shape=jax.ShapeDtypeStruct((bsz, TOK, GLA_V_WIDTH), BF16),
        scratch_shapes=[
            pltpu.VMEM((TOK, 2 * GLA_DV), F32),
            pltpu.VMEM((2, TOK, 128), F32),
            pltpu.VMEM((2, TOK, 128), BF16),
            pltpu.VMEM((2, GLA_CHUNKS, GLA_DV, 128), F32),
            pltpu.VMEM((2, GLA_CHUNKS, 1, 128), F32),
            pltpu.VMEM((2, GLA_CHUNKS, GLA_DV, 128), BF16),
            pltpu.VMEM((2, GLA_DV, 128), F32),
        ],
        compiler_params=_params("parallel", "parallel"),
    )(gla_in, gla_in, gla_in, gla_in, gla_in, wa2, ba.reshape(2, 1, GLA_QK_WIDTH), gnorm.reshape(1, GLA_V_WIDTH))


def _swa_kernel(sink_ref, q_ref, qr_ref, k_ref, v_ref, o_ref):
    kp = pl.program_id(1)
    qb = pl.program_id(2)
    kc = k_ref[0, SEQ:, :]
    vc = v_ref[0, SEQ:, :]

    def sinks(j):
        return [sink_ref[kp * 2 * SWA_GROUP + hk * SWA_GROUP + j] for hk in range(2)]

    @pl.when(qb < LAT_TILES)
    def _latent():
        q0 = qb * SWA_TQ
        start = pl.multiple_of(jnp.clip(q0 - SWA_WINDOW, 0, SEQ - SWA_NLOC), SWA_WINDOW)
        kl = k_ref[0, pl.ds(start, SWA_NLOC), :]
        vl = v_ref[0, pl.ds(start, SWA_NLOC), :]
        qpos = q0 + lax.broadcasted_iota(jnp.int32, (SWA_TQ, SWA_NLOC), 0)
        kpos = start + lax.broadcasted_iota(jnp.int32, (SWA_TQ, SWA_NLOC), 1)
        window = jnp.where(jnp.abs(kpos - qpos) <= SWA_WINDOW, 0.0, NEG_INF)
        for j in range(SWA_GROUP):
            ln = slice(j * 128, (j + 1) * 128)
            o_ref[0, :, ln] = _pair_softmax_pv(
                [qr_ref[0, :, ln], q_ref[0, :, ln]], [kl, kc], [vl, vc],
                [[window, None], [window, None]], sinks(j)).astype(BF16)

    @pl.when(qb == LAT_TILES)
    def _context():
        for j in range(SWA_GROUP):
            ln = slice(j * 128, (j + 1) * 128)
            o_ref[0, :, ln] = _pair_softmax_pv([q_ref[0, :, ln]], [kc], [vc], [None, None], sinks(j)).astype(BF16)


def _swa_attention(q, qr, kr, v, sink):
    bsz = q.shape[0]
    pairs = SWA_KV_HEADS // 2
    qw = 2 * SWA_GROUP * HEAD_DIM
    return pl.pallas_call(
        _swa_kernel,
        grid=(bsz, pairs, TILES),
        in_specs=[
            pl.BlockSpec(memory_space=pltpu.SMEM),
            pl.BlockSpec((1, SWA_TQ, qw), lambda b, p, t: (b, t, p)),
            pl.BlockSpec((1, SWA_TQ, qw), lambda b, p, t: (b, t, p)),
            pl.BlockSpec((1, TOK, 128), lambda b, p, t: (b, 0, p)),
            pl.BlockSpec((1, TOK, 128), lambda b, p, t: (b, 0, p)),
        ],
        out_specs=pl.BlockSpec((1, SWA_TQ, qw), lambda b, p, t: (b, t, p)),
        out_shape=jax.ShapeDtypeStruct((bsz, TOK, D_MODEL), BF16),
        compiler_params=_params("parallel", "parallel", "arbitrary"),
    )(sink.astype(F32), q, qr, kr, v)


def _mlp_kernel(x_ref, oa_ref, ob_ref, mod_ref, g_ref, gf_ref, wo_ref, w1_ref, w2_ref, out_ref, *, final_norm):
    mod = mod_ref[0]
    half = wo_ref.shape[0] // 2
    y = _mm(oa_ref[0], wo_ref[:half, :]) + _mm(ob_ref[0], wo_ref[half:, :])
    x1 = x_ref[0] + mod[2:3] * y
    h = _norm_modulate(x1, g_ref[...], mod[3:4], mod[4:5]).astype(BF16)
    acc = jnp.zeros((TM, D_MODEL), F32)
    for c in range(D_FF // FF_CHUNK):
        t = jnp.maximum(_mm(h, w1_ref[:, c * FF_CHUNK:(c + 1) * FF_CHUNK]), 0.0)
        acc = acc + _mm((t * t).astype(BF16), w2_ref[c * FF_CHUNK:(c + 1) * FF_CHUNK, :])
    x2 = x1 + mod[5:6] * acc
    if final_norm:
        x2 = x2 * lax.rsqrt(jnp.mean(x2 * x2, axis=-1, keepdims=True) + EPS) * gf_ref[...]
    out_ref[0] = x2


def _outproj_mlp(xs, oa, ob, ob_block, mod, gain, gain_final, wo, w1, w2, final_norm):
    bsz = xs.shape[0]
    half = D_MODEL // 2
    tiles = LAT_TILES if final_norm else TILES
    tile = lambda b, t: (b, t, 0)
    const = lambda b, t: (0, 0)
    return pl.pallas_call(
        functools.partial(_mlp_kernel, final_norm=final_norm),
        grid=(bsz, tiles),
        in_specs=[
            pl.BlockSpec((1, TM, D_MODEL), tile),
            pl.BlockSpec((1, TM, half), tile),
            pl.BlockSpec((1, TM, half), lambda b, t: (b, t, ob_block)),
            pl.BlockSpec((1, 6, D_MODEL), lambda b, t: (_mod_row(b, t), 0, 0)),
            pl.BlockSpec((1, D_MODEL), const),
            pl.BlockSpec((1, D_MODEL), const),
            pl.BlockSpec((D_MODEL, D_MODEL), const),
            pl.BlockSpec((D_MODEL, D_FF), const),
            pl.BlockSpec((D_FF, D_MODEL), const),
        ],
        out_specs=pl.BlockSpec((1, TM, D_MODEL), tile),
        out_shape=jax.ShapeDtypeStruct((bsz, tiles * TM, D_MODEL), F32),
        compiler_params=_params("parallel", "parallel"),
    )(xs, oa, ob, mod, gain, gain_final, wo, w1, w2)


def _even_in_weight(w):
    n_na = 3 * NA_WIDTH
    scale = np.ones((w.shape[1],), np.float32)
    scale[:NA_WIDTH] = HEAD_DIM ** -0.5 * LOG2E
    scale[n_na:n_na + GLA_QK_WIDTH] = GLA_DK ** -0.5
    w = w * jnp.asarray(scale)
    pad = n_na + GLA_IN_WIDTH - w.shape[1]
    return jnp.pad(w, ((0, 0), (0, pad))).astype(BF16)


def _odd_in_weight(w):
    n_rot = D_MODEL + SWA_KV_WIDTH
    rot = w[:, :n_rot].reshape(D_MODEL, n_rot // HEAD_DIM, HEAD_DIM // 2, 2)
    rot = jnp.swapaxes(rot, 2, 3).reshape(D_MODEL, n_rot)
    q = _swa_head_order(rot[:, :D_MODEL] * (HEAD_DIM ** -0.5 * LOG2E), axis=1)
    return jnp.concatenate([q, rot[:, D_MODEL:], w[:, n_rot:]], axis=1).astype(BF16)


def _swa_head_order(a, axis):
    shape = a.shape
    split = shape[:axis] + (SWA_KV_HEADS // 2, 2, SWA_GROUP, HEAD_DIM) + shape[axis + 1:]
    return jnp.swapaxes(a.reshape(split), axis + 1, axis + 2).reshape(shape)


def kernel(x, c, ctx, c_ctx, ada_w, ada_b, norm_mix, norm_mlp, mlp_w1, mlp_w2, ab_w_in, ab_w_out, na_rpb,
           gla_wa2, gla_ba, gla_gnorm, swa_w_in, swa_w_out, swa_sink, norm_final):
    bsz = x.shape[0]
    assert x.shape == (bsz, SEQ, D_MODEL) and ctx.shape == (bsz, CTX_LEN, D_MODEL) and bsz <= 8

    cvec = jnp.zeros((MOD_ROWS, D_MODEL), F32).at[:bsz].set(c).at[8].set(c_ctx)
    mods = _ada_table(cvec, ada_w, ada_b).reshape(DEPTH, MOD_ROWS, 6, D_MODEL)
    cos_t, sin_t = _rope_tables()
    gain_final = norm_final.reshape(1, D_MODEL)

    xs = jnp.concatenate([x, ctx], axis=1)
    for l in range(DEPTH):
        j = l // 2
        mod = mods[l]
        g_mix = norm_mix[l].reshape(1, D_MODEL)
        g_mlp = norm_mlp[l].reshape(1, D_MODEL)
        if l % 2 == 0:
            na_in, gla_in = _inproj_even(xs, mod, g_mix, _even_in_weight(ab_w_in[j]))
            oa = _na_attention(na_in, _na_bias_table(na_rpb[j]))
            ob = _gla(gla_in, gla_wa2[j], gla_ba[j], gla_gnorm[j])
            ob_block = 0
            wo = ab_w_out[j].astype(BF16)
        else:
            q, qr, kr, v = _inproj_odd(xs, mod, g_mix, _odd_in_weight(swa_w_in[j]), cos_t, sin_t)
            oa = ob = _swa_attention(q, qr, kr, v, swa_sink[j] * LOG2E)
            ob_block = 1
            wo = _swa_head_order(swa_w_out[j], axis=0).astype(BF16)
        xs = _outproj_mlp(xs, oa, ob, ob_block, mod, g_mlp, gain_final, wo,
                          mlp_w1[l].astype(BF16), mlp_w2[l].astype(BF16), l == DEPTH - 1)
    return xs
```

```python
import functools

import numpy as np
import jax
import jax.numpy as jnp
from jax import lax
from jax.experimental import pallas as pl
from jax.experimental.pallas import tpu as pltpu

D_MODEL = 1024
SEQ = 2048
DEPTH = 4
GRID_W = 64
GRID_ROWS = SEQ // GRID_W
CTX_LEN = 256
TOK = SEQ + CTX_LEN
HEAD_DIM = 64
EPS = 1e-6
NEG_INF = -1e30
LOG2E = 1.4426950408889634

NA_HEADS = 8
NA_WIN_ROWS = 8
NA_WIN_COLS = 16
NA_WIDTH = NA_HEADS * HEAD_DIM
NA_Q_ROWS = 4
NA_K_ROWS = NA_Q_ROWS + NA_WIN_ROWS - 1
NA_QN = NA_Q_ROWS * GRID_W
NA_KN = NA_K_ROWS * GRID_W
NA_ROW_BLOCKS = GRID_ROWS // NA_Q_ROWS

GLA_HEADS = 4
GLA_DK = 64
GLA_DV = 128
GLA_RANK = 16
GLA_NORMALIZER = 16.0
GLA_CHUNK = 64
GLA_QK_WIDTH = GLA_HEADS * GLA_DK
GLA_V_WIDTH = GLA_HEADS * GLA_DV
GLA_IN_WIDTH = 2 * GLA_QK_WIDTH + 2 * GLA_V_WIDTH + 128
GLA_CTX_CHUNKS = CTX_LEN // GLA_CHUNK
GLA_CHUNKS = TOK // GLA_CHUNK

SWA_HEADS = 16
SWA_KV_HEADS = 4
SWA_GROUP = SWA_HEADS // SWA_KV_HEADS
SWA_WINDOW = 128
SWA_KV_WIDTH = SWA_KV_HEADS * HEAD_DIM
SWA_TQ = 256
SWA_NLOC = SWA_TQ + 2 * SWA_WINDOW

D_FF = 4 * D_MODEL
FF_CHUNK = 1024
ROPE_THETA = 10000.0

TM = 256
TILES = TOK // TM
LAT_TILES = SEQ // TM
MOD_ROWS = 16
VMEM_LIMIT = 56 * 1024 * 1024

F32 = jnp.float32
BF16 = jnp.bfloat16


def _nt(a, b):
    return lax.dot_general(a, b, (((1,), (1,)), ((), ())), preferred_element_type=F32)


def _tn(a, b):
    return lax.dot_general(a, b, (((0,), (0,)), ((), ())), preferred_element_type=F32)


def _mm(a, b):
    return jnp.dot(a, b, preferred_element_type=F32)


def _params(*sem):
    return pltpu.CompilerParams(dimension_semantics=sem, vmem_limit_bytes=VMEM_LIMIT)


def _mod_row(b, t):
    return jnp.where(t < LAT_TILES, b, 8)


def _ada_kernel(c_ref, w_ref, b_ref, o_ref):
    s = c_ref[...]
    s = s * jax.nn.sigmoid(s)
    o_ref[0] = _mm(s.astype(BF16), w_ref[0].astype(BF16)) + b_ref[0]


def _ada_table(cvec, ada_w, ada_b):
    nb = 6 * D_MODEL // 1024
    return pl.pallas_call(
        _ada_kernel,
        grid=(DEPTH, nb),
        in_specs=[
            pl.BlockSpec((MOD_ROWS, D_MODEL), lambda l, n: (0, 0)),
            pl.BlockSpec((1, D_MODEL, 1024), lambda l, n: (l, 0, n)),
            pl.BlockSpec((1, 1, 1024), lambda l, n: (l, 0, n)),
        ],
        out_specs=pl.BlockSpec((1, MOD_ROWS, 1024), lambda l, n: (l, 0, n)),
        out_shape=jax.ShapeDtypeStruct((DEPTH, MOD_ROWS, 6 * D_MODEL), F32),
        compiler_params=_params("parallel", "parallel"),
    )(cvec, ada_w, ada_b.reshape(DEPTH, 1, 6 * D_MODEL))


def _norm_modulate(x, gain, shift, scale):
    y = x * lax.rsqrt(jnp.mean(x * x, axis=-1, keepdims=True) + EPS) * gain
    return y * (1.0 + scale) + shift


def _inproj_even_kernel(x_ref, mod_ref, g_ref, w_ref, na_ref, gla_ref):
    mod = mod_ref[0]
    h = _norm_modulate(x_ref[0], g_ref[...], mod[0:1], mod[1:2]).astype(BF16)
    n_na = 3 * NA_WIDTH
    na_ref[0] = _mm(h, w_ref[:, :n_na]).astype(BF16)
    gla_ref[0] = _mm(h, w_ref[:, n_na:])


def _inproj_even(xs, mod, gain, w):
    bsz = xs.shape[0]
    n_na = 3 * NA_WIDTH
    return pl.pallas_call(
        _inproj_even_kernel,
        grid=(bsz, TILES),
        in_specs=[
            pl.BlockSpec((1, TM, D_MODEL), lambda b, t: (b, t, 0)),
            pl.BlockSpec((1, 6, D_MODEL), lambda b, t: (_mod_row(b, t), 0, 0)),
            pl.BlockSpec((1, D_MODEL), lambda b, t: (0, 0)),
            pl.BlockSpec((D_MODEL, n_na + GLA_IN_WIDTH), lambda b, t: (0, 0)),
        ],
        out_specs=[
            pl.BlockSpec((1, TM, n_na), lambda b, t: (b, t, 0)),
            pl.BlockSpec((1, TM, GLA_IN_WIDTH), lambda b, t: (b, t, 0)),
        ],
        out_shape=[
            jax.ShapeDtypeStruct((bsz, TOK, n_na), BF16),
            jax.ShapeDtypeStruct((bsz, TOK, GLA_IN_WIDTH), F32),
        ],
        compiler_params=_params("parallel", "parallel"),
    )(xs, mod, gain, w)


def _rope(a, cos, sin, first_half):
    swapped = jnp.where(first_half, pltpu.roll(a, 96, 1), pltpu.roll(a, 32, 1))
    return a * cos + swapped * sin


def _inproj_odd_kernel(x_ref, mod_ref, g_ref, w_ref, cos_ref, sin_ref, q_ref, qr_ref, kr_ref, v_ref):
    mod = mod_ref[0]
    h = _norm_modulate(x_ref[0], g_ref[...], mod[0:1], mod[1:2]).astype(BF16)
    cos = cos_ref[...]
    sin = sin_ref[...]
    first_half = (lax.broadcasted_iota(jnp.int32, (TM, 128), 1) % HEAD_DIM) < HEAD_DIM // 2
    wide = 256
    for j in range(D_MODEL // wide):
        a = _mm(h, w_ref[:, j * wide:(j + 1) * wide])
        q_ref[0, :, j * wide:(j + 1) * wide] = a.astype(BF16)
        for t in range(wide // 128):
            c0 = j * wide + t * 128
            qr_ref[0, :, c0:c0 + 128] = _rope(a[:, t * 128:(t + 1) * 128], cos, sin, first_half).astype(BF16)
    a = _mm(h, w_ref[:, D_MODEL:D_MODEL + SWA_KV_WIDTH])
    for t in range(SWA_KV_WIDTH // 128):
        kr_ref[0, :, t * 128:(t + 1) * 128] = _rope(a[:, t * 128:(t + 1) * 128], cos, sin, first_half).astype(BF16)
    v_ref[0] = _mm(h, w_ref[:, D_MODEL + SWA_KV_WIDTH:]).astype(BF16)


def _inproj_odd(xs, mod, gain, w, cos_t, sin_t):
    bsz = xs.shape[0]
    n_in = D_MODEL + 2 * SWA_KV_WIDTH
    tile = lambda b, t: (b, t, 0)
    return pl.pallas_call(
        _inproj_odd_kernel,
        grid=(bsz, TILES),
        in_specs=[
            pl.BlockSpec((1, TM, D_MODEL), tile),
            pl.BlockSpec((1, 6, D_MODEL), lambda b, t: (_mod_row(b, t), 0, 0)),
            pl.BlockSpec((1, D_MODEL), lambda b, t: (0, 0)),
            pl.BlockSpec((D_MODEL, n_in), lambda b, t: (0, 0)),
            pl.BlockSpec((TM, 128), lambda b, t: (t, 0)),
            pl.BlockSpec((TM, 128), lambda b, t: (t, 0)),
        ],
        out_specs=[
            pl.BlockSpec((1, TM, D_MODEL), tile),
            pl.BlockSpec((1, TM, D_MODEL), tile),
            pl.BlockSpec((1, TM, SWA_KV_WIDTH), tile),
            pl.BlockSpec((1, TM, SWA_KV_WIDTH), tile),
        ],
        out_shape=[
            jax.ShapeDtypeStruct((bsz, TOK, D_MODEL), BF16),
            jax.ShapeDtypeStruct((bsz, TOK, D_MODEL), BF16),
            jax.ShapeDtypeStruct((bsz, TOK, SWA_KV_WIDTH), BF16),
            jax.ShapeDtypeStruct((bsz, TOK, SWA_KV_WIDTH), BF16),
        ],
        compiler_params=_params("parallel", "parallel"),
    )(xs, mod, gain, w, cos_t, sin_t)


def _rope_tables():
    t = np.arange(SEQ)
    n_freq = HEAD_DIM // 4
    inv = jnp.asarray(ROPE_THETA, F32) ** (-jnp.arange(n_freq, dtype=F32) / n_freq)
    row = jnp.asarray(t // GRID_W, F32)
    col = jnp.asarray(t % GRID_W, F32)
    ang = jnp.concatenate([row[:, None] * inv, col[:, None] * inv], axis=-1)
    cos, sin = jnp.cos(ang), jnp.sin(ang)
    cos_t = jnp.tile(cos, (1, 4))
    sin_t = jnp.tile(jnp.concatenate([-sin, sin], axis=-1), (1, 2))
    cos_t = jnp.concatenate([cos_t, jnp.ones((CTX_LEN, 128), F32)], axis=0)
    sin_t = jnp.concatenate([sin_t, jnp.zeros((CTX_LEN, 128), F32)], axis=0)
    return cos_t, sin_t


NA_ROW_OFFSETS = 2 * NA_WIN_ROWS - 1
NA_BIAS_BLOCKS = NA_ROW_OFFSETS + 1


def _na_col_mask():
    c = np.arange(GRID_W)
    w0 = np.clip(c - NA_WIN_COLS // 2, 0, GRID_W - NA_WIN_COLS)
    return (c[None, :] >= w0[:, None]) & (c[None, :] < w0[:, None] + NA_WIN_COLS)


_NA_COL_OK = _na_col_mask()


def _na_bias_table(rpb):
    n = GRID_W
    lead = n - NA_WIN_COLS
    w = jnp.pad(rpb.astype(F32), ((0, 0), (0, 0), (lead, 2 * n - lead - (2 * NA_WIN_COLS - 1))))
    col = jnp.tile(w, (1, 1, n))[..., n - 1:n - 1 + n * (2 * n - 1)]
    col = col.reshape(NA_HEADS, NA_ROW_OFFSETS, n, 2 * n - 1)[..., :n]
    col = jnp.where(jnp.asarray(_NA_COL_OK), col * LOG2E, NEG_INF)
    col = jnp.concatenate([col, jnp.full((NA_HEADS, 1, n, n), NEG_INF, F32)], axis=1)
    return jnp.concatenate([col, col], axis=-1)


def _na_bias(bias_ref, head, rb):
    lane = lax.broadcasted_iota(jnp.int32, (1, 128), 1)
    low = lane < GRID_W
    q_row0 = rb * NA_Q_ROWS
    k_row0 = jnp.clip(q_row0 - NA_WIN_ROWS // 2, 0, GRID_ROWS - NA_K_ROWS)
    rows = []
    for i in range(NA_Q_ROWS):
        r = q_row0 + i
        r0 = jnp.clip(r - NA_WIN_ROWS // 2, 0, GRID_ROWS - NA_WIN_ROWS)
        blocks = []
        for j in range(NA_K_ROWS):
            kr = k_row0 + j
            inside = (kr >= r0) & (kr < r0 + NA_WIN_ROWS)
            idx = jnp.where(inside, kr - r + NA_WIN_ROWS - 1, NA_ROW_OFFSETS)
            blocks.append(bias_ref[0, head * NA_BIAS_BLOCKS + idx])
        pieces = [jnp.where(low, blocks[j], blocks[j + 1]) for j in range(0, NA_K_ROWS - 1, 2)]
        pieces.append(blocks[NA_K_ROWS - 1][:, :GRID_W])
        rows.append(jnp.concatenate(pieces, axis=1))
    return jnp.concatenate(rows, axis=0)


def _pair_softmax_pv(qs, keys, values, biases, floor):
    return _lane_pair_attention([(qs, keys, values, biases, floor)])[0]


def _lane_pair_attention(blocks):
    lane = lax.broadcasted_iota(jnp.int32, (1, 128), 1)
    half = (lane < HEAD_DIM, lane >= HEAD_DIM)

    def scores(j, h):
        qs, keys, _, biases, _ = blocks[j]
        s = [_nt(jnp.where(half[h], q, jnp.zeros_like(q)), k) for q, k in zip(qs, keys)]
        if biases[h] is not None:
            bias = biases[h]() if callable(biases[h]) else biases[h]
            s = [x if b is None else x + b for x, b in zip(s, bias)]
        return s

    def output(j, h, s):
        _, _, values, _, floor = blocks[j]
        m = jnp.max(s[0], axis=-1, keepdims=True)
        for x in s[1:]:
            m = jnp.maximum(m, jnp.max(x, axis=-1, keepdims=True))
        if floor[h] is not None:
            m = jnp.maximum(m, floor[h])
        acc = None
        for x, v in zip(s, values):
            p = jnp.exp2((x - m).astype(BF16))
            va = jnp.where(half[h], v, jnp.ones_like(v))
            pv = _mm(p, va)
            acc = pv if acc is None else acc + pv
        den = pltpu.roll(acc, HEAD_DIM, 1)
        if floor[h] is not None:
            den = den + jnp.exp2(floor[h] - m)
        return acc / den

    heads = [(j, h) for j in range(len(blocks)) for h in range(2)]
    res = {}
    ahead = scores(*heads[0])
    for i, (j, h) in enumerate(heads):
        s = ahead
        if i + 1 < len(heads):
            ahead = scores(*heads[i + 1])
        res[(j, h)] = output(j, h, s)
    return [jnp.where(half[0], res[(j, 0)], res[(j, 1)]) for j in range(len(blocks))]


def _na_kernel(q_ref, k_ref, v_ref, bias_ref, o_ref):
    rb = pl.program_id(2)
    n_blocks = q_ref.shape[2] // 128

    @pl.when(rb < NA_ROW_BLOCKS)
    def _latent():
        k_row0 = jnp.clip(rb * NA_Q_ROWS - NA_WIN_ROWS // 2, 0, GRID_ROWS - NA_K_ROWS)
        start = pl.multiple_of(k_row0 * GRID_W, GRID_W)
        blocks = []
        for j in range(n_blocks):
            ln = slice(j * 128, (j + 1) * 128)
            keys = [k_ref[0, pl.ds(start, NA_KN), ln], k_ref[0, SEQ:, ln]]
            values = [v_ref[0, pl.ds(start, NA_KN), ln], v_ref[0, SEQ:, ln]]
            biases = [functools.partial(lambda head: [_na_bias(bias_ref, head, rb), None], 2 * j + h)
                      for h in range(2)]
            q = q_ref[0, :, ln]
            blocks.append(([q, q], keys, values, biases, [None, None]))
        for j, o in enumerate(_lane_pair_attention(blocks)):
            o_ref[0, :, j * 128:(j + 1) * 128] = o.astype(BF16)

    @pl.when(rb == NA_ROW_BLOCKS)
    def _context():
        blocks = []
        for j in range(n_blocks):
            ln = slice(j * 128, (j + 1) * 128)
            blocks.append(([q_ref[0, :, ln]], [k_ref[0, SEQ:, ln]], [v_ref[0, SEQ:, ln]],
                           [None, None], [None, None]))
        for j, o in enumerate(_lane_pair_attention(blocks)):
            o_ref[0, :, j * 128:(j + 1) * 128] = o.astype(BF16)


def _na_attention(qkv, bias):
    bsz = qkv.shape[0]
    hps = 8
    groups = NA_HEADS // hps
    w = hps * HEAD_DIM
    return pl.pallas_call(
        _na_kernel,
        grid=(groups, bsz, NA_ROW_BLOCKS + 1),
        in_specs=[
            pl.BlockSpec((1, NA_QN, w), lambda p, b, r: (b, r, p)),
            pl.BlockSpec((1, TOK, w), lambda p, b, r: (b, 0, groups + p)),
            pl.BlockSpec((1, TOK, w), lambda p, b, r: (b, 0, 2 * groups + p)),
            pl.BlockSpec((1, hps * NA_BIAS_BLOCKS, GRID_W, 128), lambda p, b, r: (p, 0, 0, 0)),
        ],
        out_specs=pl.BlockSpec((1, NA_QN, w), lambda p, b, r: (b, r, p)),
        out_shape=jax.ShapeDtypeStruct((bsz, TOK, NA_WIDTH), BF16),
        compiler_params=_params("parallel", "parallel", "arbitrary"),
    )(qkv, qkv, qkv, bias.reshape(groups, hps * NA_BIAS_BLOCKS, GRID_W, 128))


def _split3(g):
    hi = g.astype(BF16)
    r1 = g - hi.astype(F32)
    mid = r1.astype(BF16)
    lo = (r1 - mid.astype(F32)).astype(BF16)
    return hi, mid, lo


def _gla_kernel(q_ref, k_ref, v_ref, gate_ref, lr_ref, wa2_ref, ba_ref, gn_ref, o_ref,
                acc_ref, cum_ref, qt_ref, u_ref, dec_ref, sp_ref, st_ref):
    c = GLA_CHUNK
    ii = lax.broadcasted_iota(jnp.int32, (c, c), 0)
    jj = lax.broadcasted_iota(jnp.int32, (c, c), 1)
    incl = (jj <= ii, jj >= ii)
    tri = tuple(jnp.where(m, 1.0, 0.0).astype(BF16) for m in incl)
    head0 = lax.broadcasted_iota(jnp.int32, (1, 128), 1) < GLA_DK
    per_tile = TM // c

    def decays(t, carry):
        rows = pl.ds(pl.multiple_of(t * TM, TM), TM)
        lr = lr_ref[0, rows, :].astype(BF16)
        z2 = _mm(lr, wa2_ref[...]) + ba_ref[...]
        for d in range(2):
            z = z2[:, d * 128:(d + 1) * 128]
            g = (jnp.minimum(z, 0.0) - jnp.log1p(jnp.exp(-jnp.abs(z)))) / GLA_NORMALIZER
            wide = jnp.concatenate([g[i * c:(i + 1) * c] for i in range(per_tile)], axis=1)
            hi, mid, lo = _split3(wide)
            cum = _mm(tri[d], hi) + _mm(tri[d], mid) + _mm(tri[d], lo)
            cum_ref[d, rows, :] = jnp.concatenate([cum[:, i * 128:(i + 1) * 128] for i in range(per_tile)], axis=0)
        return carry

    lax.fori_loop(0, TILES, decays, 0, unroll=3)

    t2 = lax.broadcasted_iota(jnp.int32, (c, 2 * c), 0)
    j2 = lax.broadcasted_iota(jnp.int32, (c, 2 * c), 1) % c
    incl2 = (j2 <= t2, j2 >= t2)

    def by_head(a):
        zero = jnp.zeros_like(a)
        return jnp.concatenate([jnp.where(head0, a, zero), jnp.where(head0, zero, a)], axis=0)

    def intra(ci, carry):
        rows = pl.ds(pl.multiple_of(ci * c, c), c)
        qc = q_ref[0, rows, :]
        kc = k_ref[0, rows, :]
        v2 = v_ref[0, rows, :].astype(BF16)
        zero_v = jnp.zeros((c, GLA_DV), BF16)
        v_diag = jnp.concatenate([jnp.concatenate([v2[:, :GLA_DV], zero_v], axis=1),
                                  jnp.concatenate([zero_v, v2[:, GLA_DV:]], axis=1)], axis=0)
        p_sum = None
        k_ends = []
        for d in range(2):
            cum = cum_ref[d, rows, :]
            tot = cum[c - 1:c, :] if d == 0 else cum[0:1, :]
            q_t = (qc * jnp.exp(cum)).astype(BF16)
            k_t = (kc * jnp.exp(-cum)).astype(BF16)
            k_ends.append((kc * jnp.exp(tot - cum)).astype(BF16))
            dec_ref[d, ci] = jnp.exp(tot)
            qt_ref[ci, :, d * 128:(d + 1) * 128] = by_head(q_t)
            p = jnp.where(incl2[d], _nt(q_t, by_head(k_t)), 0.0)
            p_sum = p if p_sum is None else p_sum + p
        acc_ref[rows, :] = _mm(p_sum.astype(BF16), v_diag)
        uu = _tn(v2, jnp.concatenate(k_ends, axis=1))
        for d in range(2):
            blk = uu[:, d * 128:(d + 1) * 128]
            u_ref[d, ci] = jnp.where(head0, blk[:GLA_DV], blk[GLA_DV:])
        return carry

    lax.fori_loop(0, GLA_CHUNKS, intra, 0, unroll=4)

    st_ref[...] = jnp.zeros_like(st_ref)

    def scan(i, carry):
        order = (jnp.where(i < GLA_CTX_CHUNKS, GLA_CHUNKS - GLA_CTX_CHUNKS + i, i - GLA_CTX_CHUNKS),
                 GLA_CHUNKS - 1 - i)
        for d in range(2):
            s = st_ref[d]
            sp_ref[order[d], :, d * 128:(d + 1) * 128] = s.astype(BF16)
            st_ref[d] = dec_ref[d, order[d]] * s + u_ref[d, order[d]]
        return carry

    lax.fori_loop(0, GLA_CHUNKS, scan, 0)

    def inter(ci, carry):
        rows = pl.ds(pl.multiple_of(ci * c, c), c)
        o = _nt(qt_ref[ci], sp_ref[ci])
        acc_ref[rows, :] = acc_ref[rows, :] + jnp.concatenate([o[:c], o[c:]], axis=1)
        return carry

    lax.fori_loop(0, GLA_CHUNKS, inter, 0, unroll=4)

    def finish(t, carry):
        rows = pl.ds(pl.multiple_of(t * TM, TM), TM)
        gate = gate_ref[0, rows, :]
        sw = gate * jax.nn.sigmoid(gate)
        for h in range(2):
            vs = slice(h * GLA_DV, (h + 1) * GLA_DV)
            o = acc_ref[rows, vs]
            o = o * lax.rsqrt(jnp.mean(o * o, axis=-1, keepdims=True) + EPS)
            o_ref[0, rows, vs] = (o * gn_ref[:, vs] * sw[:, vs]).astype(BF16)
        return carry

    lax.fori_loop(0, TILES, finish, 0)


def _gla(gla_in, wa2, ba, gnorm):
    bsz = gla_in.shape[0]
    pairs = GLA_HEADS // 2
    qk_blocks = GLA_QK_WIDTH // 128
    v_blocks = GLA_V_WIDTH // 256
    v0 = 2 * GLA_QK_WIDTH // 256
    lr_block = (2 * GLA_QK_WIDTH + 2 * GLA_V_WIDTH) // 128
    wa2_rows = jnp.zeros((2, 128, GLA_QK_WIDTH), F32)
    for d in range(2):
        wa2_rows = wa2_rows.at[d, d * GLA_RANK:(d + 1) * GLA_RANK].set(wa2[d])
    wa2 = wa2_rows.reshape(2, 128, pairs, 128).transpose(1, 2, 0, 3).reshape(128, pairs * 256).astype(BF16)
    ba = ba.reshape(2, pairs, 128).transpose(1, 0, 2).reshape(1, pairs * 256)
    return pl.pallas_call(
        _gla_kernel,
        grid=(bsz, pairs),
        in_specs=[
            pl.BlockSpec((1, TOK, 128), lambda b, p: (b, 0, p)),
            pl.BlockSpec((1, TOK, 128), lambda b, p: (b, 0, qk_blocks + p)),
            pl.BlockSpec((1, TOK, 256), lambda b, p: (b, 0, v0 + p)),
            pl.BlockSpec((1, TOK, 256), lambda b, p: (b, 0, v0 + v_blocks + p)),
            pl.BlockSpec((1, TOK, 128), lambda b, p: (b, 0, lr_block)),
            pl.BlockSpec((128, 256), lambda b, p: (0, p)),
            pl.BlockSpec((1, 256), lambda b, p: (0, p)),
            pl.BlockSpec((1, 256), lambda b, p: (0, p)),
        ],
        out_specs=pl.BlockSpec((1, TOK, 256), lambda b, p: (b, 0, p)),
        out_shape=jax.ShapeDtypeStruct((bsz, TOK, GLA_V_WIDTH), BF16),
        scratch_shapes=[
            pltpu.VMEM((TOK, 2 * GLA_DV), F32),
            pltpu.VMEM((2, TOK, 128), F32),
            pltpu.VMEM((GLA_CHUNKS, 2 * GLA_CHUNK, 256), BF16),
            pltpu.VMEM((2, GLA_CHUNKS, GLA_DV, 128), F32),
            pltpu.VMEM((2, GLA_CHUNKS, 1, 128), F32),
            pltpu.VMEM((GLA_CHUNKS, GLA_DV, 256), BF16),
            pltpu.VMEM((2, GLA_DV, 128), F32),
        ],
        compiler_params=_params("parallel", "parallel"),
    )(gla_in, gla_in, gla_in, gla_in, gla_in, wa2, ba, gnorm.reshape(1, GLA_V_WIDTH))


def _swa_kernel(sink_ref, q_ref, qr_ref, k_ref, v_ref, o_ref):
    kp = pl.program_id(1)
    qb = pl.program_id(2)
    kc = k_ref[0, SEQ:, :]
    vc = v_ref[0, SEQ:, :]

    def sinks(j):
        return [sink_ref[kp * 2 * SWA_GROUP + hk * SWA_GROUP + j] for hk in range(2)]

    @pl.when(qb < LAT_TILES)
    def _latent():
        q0 = qb * SWA_TQ
        start = pl.multiple_of(jnp.clip(q0 - SWA_WINDOW, 0, SEQ - SWA_NLOC), SWA_WINDOW)
        kl = k_ref[0, pl.ds(start, SWA_NLOC), :]
        vl = v_ref[0, pl.ds(start, SWA_NLOC), :]
        qpos = q0 + lax.broadcasted_iota(jnp.int32, (SWA_TQ, SWA_NLOC), 0)
        kpos = start + lax.broadcasted_iota(jnp.int32, (SWA_TQ, SWA_NLOC), 1)
        window = jnp.where(jnp.abs(kpos - qpos) <= SWA_WINDOW, 0.0, NEG_INF)
        blocks = []
        for j in range(SWA_GROUP):
            ln = slice(j * 128, (j + 1) * 128)
            blocks.append(([qr_ref[0, :, ln], q_ref[0, :, ln]], [kl, kc], [vl, vc],
                           [[window, None], [window, None]], sinks(j)))
        for j, o in enumerate(_lane_pair_attention(blocks)):
            o_ref[0, :, j * 128:(j + 1) * 128] = o.astype(BF16)

    @pl.when(qb == LAT_TILES)
    def _context():
        blocks = []
        for j in range(SWA_GROUP):
            ln = slice(j * 128, (j + 1) * 128)
            blocks.append(([q_ref[0, :, ln]], [kc], [vc], [None, None], sinks(j)))
        for j, o in enumerate(_lane_pair_attention(blocks)):
            o_ref[0, :, j * 128:(j + 1) * 128] = o.astype(BF16)


def _swa_attention(q, qr, kr, v, sink):
    bsz = q.shape[0]
    pairs = SWA_KV_HEADS // 2
    qw = 2 * SWA_GROUP * HEAD_DIM
    return pl.pallas_call(
        _swa_kernel,
        grid=(bsz, pairs, TILES),
        in_specs=[
            pl.BlockSpec(memory_space=pltpu.SMEM),
            pl.BlockSpec((1, SWA_TQ, qw), lambda b, p, t: (b, t, p)),
            pl.BlockSpec((1, SWA_TQ, qw), lambda b, p, t: (b, t, p)),
            pl.BlockSpec((1, TOK, 128), lambda b, p, t: (b, 0, p)),
            pl.BlockSpec((1, TOK, 128), lambda b, p, t: (b, 0, p)),
        ],
        out_specs=pl.BlockSpec((1, SWA_TQ, qw), lambda b, p, t: (b, t, p)),
        out_shape=jax.ShapeDtypeStruct((bsz, TOK, D_MODEL), BF16),
        compiler_params=_params("parallel", "parallel", "arbitrary"),
    )(sink.astype(F32), q, qr, kr, v)


def _mlp_kernel(x_ref, oa_ref, ob_ref, mod_ref, g_ref, gf_ref, wo_ref, w1_ref, w2_ref, out_ref, *, final_norm):
    mod = mod_ref[0]
    half = wo_ref.shape[0] // 2
    y = _mm(oa_ref[0], wo_ref[:half, :]) + _mm(ob_ref[0], wo_ref[half:, :])
    x1 = x_ref[0] + mod[2:3] * y
    h = _norm_modulate(x1, g_ref[...], mod[3:4], mod[4:5]).astype(BF16)
    acc = jnp.zeros((TM, D_MODEL), F32)
    for c in range(D_FF // FF_CHUNK):
        t = jnp.maximum(_mm(h, w1_ref[:, c * FF_CHUNK:(c + 1) * FF_CHUNK]), 0.0)
        acc = acc + _mm((t * t).astype(BF16), w2_ref[c * FF_CHUNK:(c + 1) * FF_CHUNK, :])
    x2 = x1 + mod[5:6] * acc
    if final_norm:
        x2 = x2 * lax.rsqrt(jnp.mean(x2 * x2, axis=-1, keepdims=True) + EPS) * gf_ref[...]
    out_ref[0] = x2


def _outproj_mlp(xs, oa, ob, ob_block, mod, gain, gain_final, wo, w1, w2, final_norm):
    bsz = xs.shape[0]
    half = D_MODEL // 2
    tiles = LAT_TILES if final_norm else TILES
    tile = lambda b, t: (b, t, 0)
    const = lambda b, t: (0, 0)
    return pl.pallas_call(
        functools.partial(_mlp_kernel, final_norm=final_norm),
        grid=(bsz, tiles),
        in_specs=[
            pl.BlockSpec((1, TM, D_MODEL), tile),
            pl.BlockSpec((1, TM, half), tile),
            pl.BlockSpec((1, TM, half), lambda b, t: (b, t, ob_block)),
            pl.BlockSpec((1, 6, D_MODEL), lambda b, t: (_mod_row(b, t), 0, 0)),
            pl.BlockSpec((1, D_MODEL), const),
            pl.BlockSpec((1, D_MODEL), const),
            pl.BlockSpec((D_MODEL, D_MODEL), const),
            pl.BlockSpec((D_MODEL, D_FF), const),
            pl.BlockSpec((D_FF, D_MODEL), const),
        ],
        out_specs=pl.BlockSpec((1, TM, D_MODEL), tile),
        out_shape=jax.ShapeDtypeStruct((bsz, tiles * TM, D_MODEL), F32),
        compiler_params=_params("parallel", "parallel"),
    )(xs, oa, ob, mod, gain, gain_final, wo, w1, w2)


def _even_in_weight(w):
    n_na = 3 * NA_WIDTH
    scale = np.ones((w.shape[1],), np.float32)
    scale[:NA_WIDTH] = HEAD_DIM ** -0.5 * LOG2E
    scale[n_na:n_na + GLA_QK_WIDTH] = GLA_DK ** -0.5
    w = w * jnp.asarray(scale)
    pad = n_na + GLA_IN_WIDTH - w.shape[1]
    return jnp.pad(w, ((0, 0), (0, pad))).astype(BF16)


def _odd_in_weight(w):
    n_rot = D_MODEL + SWA_KV_WIDTH
    rot = w[:, :n_rot].reshape(D_MODEL, n_rot // HEAD_DIM, HEAD_DIM // 2, 2)
    rot = jnp.swapaxes(rot, 2, 3).reshape(D_MODEL, n_rot)
    q = _swa_head_order(rot[:, :D_MODEL] * (HEAD_DIM ** -0.5 * LOG2E), axis=1)
    return jnp.concatenate([q, rot[:, D_MODEL:], w[:, n_rot:]], axis=1).astype(BF16)


def _swa_head_order(a, axis):
    shape = a.shape
    split = shape[:axis] + (SWA_KV_HEADS // 2, 2, SWA_GROUP, HEAD_DIM) + shape[axis + 1:]
    return jnp.swapaxes(a.reshape(split), axis + 1, axis + 2).reshape(shape)


def kernel(x, c, ctx, c_ctx, ada_w, ada_b, norm_mix, norm_mlp, mlp_w1, mlp_w2, ab_w_in, ab_w_out, na_rpb,
           gla_wa2, gla_ba, gla_gnorm, swa_w_in, swa_w_out, swa_sink, norm_final):
    bsz = x.shape[0]
    assert x.shape == (bsz, SEQ, D_MODEL) and ctx.shape == (bsz, CTX_LEN, D_MODEL) and bsz <= 8

    cvec = jnp.zeros((MOD_ROWS, D_MODEL), F32).at[:bsz].set(c).at[8].set(c_ctx)
    mods = _ada_table(cvec, ada_w, ada_b).reshape(DEPTH, MOD_ROWS, 6, D_MODEL)
    cos_t, sin_t = _rope_tables()
    gain_final = norm_final.reshape(1, D_MODEL)

    xs = jnp.concatenate([x, ctx], axis=1)
    for l in range(DEPTH):
        j = l // 2
        mod = mods[l]
        g_mix = norm_mix[l].reshape(1, D_MODEL)
        g_mlp = norm_mlp[l].reshape(1, D_MODEL)
        if l % 2 == 0:
            na_in, gla_in = _inproj_even(xs, mod, g_mix, _even_in_weight(ab_w_in[j]))
            oa = _na_attention(na_in, _na_bias_table(na_rpb[j]))
            ob = _gla(gla_in, gla_wa2[j], gla_ba[j], gla_gnorm[j])
            ob_block = 0
            wo = ab_w_out[j].astype(BF16)
        else:
            q, qr, kr, v = _inproj_odd(xs, mod, g_mix, _odd_in_weight(swa_w_in[j]), cos_t, sin_t)
            oa = ob = _swa_attention(q, qr, kr, v, swa_sink[j] * LOG2E)
            ob_block = 1
            wo = _swa_head_order(swa_w_out[j], axis=0).astype(BF16)
        xs = _outproj_mlp(xs, oa, ob, ob_block, mod, g_mlp, gain_final, wo,
                          mlp_w1[l].astype(BF16), mlp_w2[l].astype(BF16), l == DEPTH - 1)
    return xs
```

```python
import functools

import numpy as np
import jax
import jax.numpy as jnp
from jax import lax
from jax.experimental import pallas as pl
from jax.experimental.pallas import tpu as pltpu

D_MODEL = 1024
SEQ = 2048
DEPTH = 4
GRID_W = 64
GRID_ROWS = SEQ // GRID_W
CTX_LEN = 256
TOK = SEQ + CTX_LEN
HEAD_DIM = 64
EPS = 1e-6
NEG_INF = -1e30
LOG2E = 1.4426950408889634

NA_HEADS = 8
NA_WIN_ROWS = 8
NA_WIN_COLS = 16
NA_WIDTH = NA_HEADS * HEAD_DIM
NA_Q_ROWS = 4
NA_K_ROWS = NA_Q_ROWS + NA_WIN_ROWS - 1
NA_QN = NA_Q_ROWS * GRID_W
NA_KN = NA_K_ROWS * GRID_W
NA_ROW_BLOCKS = GRID_ROWS // NA_Q_ROWS

GLA_HEADS = 4
GLA_DK = 64
GLA_DV = 128
GLA_RANK = 16
GLA_NORMALIZER = 16.0
GLA_CHUNK = 64
GLA_QK_WIDTH = GLA_HEADS * GLA_DK
GLA_V_WIDTH = GLA_HEADS * GLA_DV
GLA_IN_WIDTH = 2 * GLA_QK_WIDTH + 2 * GLA_V_WIDTH + 128
GLA_CTX_CHUNKS = CTX_LEN // GLA_CHUNK
GLA_CHUNKS = TOK // GLA_CHUNK

SWA_HEADS = 16
SWA_KV_HEADS = 4
SWA_GROUP = SWA_HEADS // SWA_KV_HEADS
SWA_WINDOW = 128
SWA_KV_WIDTH = SWA_KV_HEADS * HEAD_DIM
SWA_TQ = 256
SWA_NLOC = SWA_TQ + 2 * SWA_WINDOW

D_FF = 4 * D_MODEL
FF_CHUNK = 1024
ROPE_THETA = 10000.0

TM = 256
TILES = TOK // TM
LAT_TILES = SEQ // TM
MOD_ROWS = 16
VMEM_LIMIT = 56 * 1024 * 1024

F32 = jnp.float32
BF16 = jnp.bfloat16


def _nt(a, b):
    return lax.dot_general(a, b, (((1,), (1,)), ((), ())), preferred_element_type=F32)


def _tn(a, b):
    return lax.dot_general(a, b, (((0,), (0,)), ((), ())), preferred_element_type=F32)


def _mm(a, b):
    return jnp.dot(a, b, preferred_element_type=F32)


def _params(*sem):
    return pltpu.CompilerParams(dimension_semantics=sem, vmem_limit_bytes=VMEM_LIMIT)


def _mod_row(b, t):
    return jnp.where(t < LAT_TILES, b, 8)


def _mod_spec(l):
    return pl.BlockSpec((None, 1, 6, D_MODEL), lambda b, t: (l, _mod_row(b, t), 0, 0))


def _layer_spec(l, *tail):
    return pl.BlockSpec((None,) + tail, lambda b, t: (l,) + (0,) * len(tail))


def _stream_specs(stream, tm=TM):
    if len(stream) == 1:
        return [pl.BlockSpec((1, tm, D_MODEL), lambda b, t: (b, t, 0))]
    return [
        pl.BlockSpec((1, TM, D_MODEL), lambda b, t: (b, jnp.minimum(t, LAT_TILES - 1), 0)),
        pl.BlockSpec((1, CTX_LEN, D_MODEL), lambda b, t: (b, 0, 0)),
    ]


def _stream_tile(refs):
    if len(refs) == 1:
        return refs[0][0]
    return jnp.where(pl.program_id(1) < LAT_TILES, refs[0][0], refs[1][0])


def _ada_kernel(c_ref, w_ref, b_ref, o_ref):
    s = c_ref[...]
    s = s * jax.nn.sigmoid(s)
    o_ref[0] = _mm(s.astype(BF16), w_ref[0].astype(BF16)) + b_ref[0]


def _ada_table(cvec, ada_w, ada_b):
    nb = 6 * D_MODEL // 1024
    return pl.pallas_call(
        _ada_kernel,
        grid=(DEPTH, nb),
        in_specs=[
            pl.BlockSpec((MOD_ROWS, D_MODEL), lambda l, n: (0, 0)),
            pl.BlockSpec((1, D_MODEL, 1024), lambda l, n: (l, 0, n)),
            pl.BlockSpec((1, 1, 1024), lambda l, n: (l, 0, n)),
        ],
        out_specs=pl.BlockSpec((1, MOD_ROWS, 1024), lambda l, n: (l, 0, n)),
        out_shape=jax.ShapeDtypeStruct((DEPTH, MOD_ROWS, 6 * D_MODEL), F32),
        compiler_params=_params("parallel", "parallel"),
    )(cvec, ada_w, ada_b.reshape(DEPTH, 1, 6 * D_MODEL))


def _norm_modulate(x, gain, shift, scale):
    y = x * lax.rsqrt(jnp.mean(x * x, axis=-1, keepdims=True) + EPS) * gain
    return y * (1.0 + scale) + shift


def _inproj_even_kernel(*refs):
    mod_ref, g_ref, w_ref, na_ref, gla_ref = refs[-5:]
    mod = mod_ref[0]
    h = _norm_modulate(_stream_tile(refs[:-5]), g_ref[...], mod[0:1], mod[1:2]).astype(BF16)
    n_na = 3 * NA_WIDTH
    na_ref[0] = _mm(h, w_ref[:, :n_na]).astype(BF16)
    gla_ref[0] = _mm(h, w_ref[:, n_na:])


def _inproj_even(stream, l, mods, gains, w):
    bsz = stream[0].shape[0]
    n_na = 3 * NA_WIDTH
    return pl.pallas_call(
        _inproj_even_kernel,
        grid=(bsz, TILES),
        in_specs=_stream_specs(stream) + [
            _mod_spec(l),
            _layer_spec(l, 1, D_MODEL),
            _layer_spec(l // 2, D_MODEL, n_na + GLA_IN_WIDTH),
        ],
        out_specs=[
            pl.BlockSpec((1, TM, n_na), lambda b, t: (b, t, 0)),
            pl.BlockSpec((1, TM, GLA_IN_WIDTH), lambda b, t: (b, t, 0)),
        ],
        out_shape=[
            jax.ShapeDtypeStruct((bsz, TOK, n_na), BF16),
            jax.ShapeDtypeStruct((bsz, TOK, GLA_IN_WIDTH), F32),
        ],
        compiler_params=_params("parallel", "parallel"),
    )(*stream, mods, gains, w)


def _rope(a, cos, sin, first_half):
    swapped = jnp.where(first_half, pltpu.roll(a, 96, 1), pltpu.roll(a, 32, 1))
    return a * cos + swapped * sin


def _inproj_odd_kernel(*refs):
    mod_ref, g_ref, w_ref, cos_ref, sin_ref, q_ref, qr_ref, kr_ref, v_ref = refs[-9:]
    mod = mod_ref[0]
    h = _norm_modulate(_stream_tile(refs[:-9]), g_ref[...], mod[0:1], mod[1:2]).astype(BF16)
    cos = cos_ref[...]
    sin = sin_ref[...]
    first_half = (lax.broadcasted_iota(jnp.int32, (TM, 128), 1) % HEAD_DIM) < HEAD_DIM // 2
    wide = 256
    for j in range(D_MODEL // wide):
        a = _mm(h, w_ref[:, j * wide:(j + 1) * wide])
        q_ref[0, :, j * wide:(j + 1) * wide] = a.astype(BF16)
        for t in range(wide // 128):
            c0 = j * wide + t * 128
            qr_ref[0, :, c0:c0 + 128] = _rope(a[:, t * 128:(t + 1) * 128], cos, sin, first_half).astype(BF16)
    a = _mm(h, w_ref[:, D_MODEL:D_MODEL + SWA_KV_WIDTH])
    for t in range(SWA_KV_WIDTH // 128):
        kr_ref[0, :, t * 128:(t + 1) * 128] = _rope(a[:, t * 128:(t + 1) * 128], cos, sin, first_half).astype(BF16)
    v_ref[0] = _mm(h, w_ref[:, D_MODEL + SWA_KV_WIDTH:]).astype(BF16)


def _inproj_odd(stream, l, mods, gains, w, cos_t, sin_t):
    bsz = stream[0].shape[0]
    n_in = D_MODEL + 2 * SWA_KV_WIDTH
    tile = lambda b, t: (b, t, 0)
    return pl.pallas_call(
        _inproj_odd_kernel,
        grid=(bsz, TILES),
        in_specs=_stream_specs(stream) + [
            _mod_spec(l),
            _layer_spec(l, 1, D_MODEL),
            _layer_spec(l // 2, D_MODEL, n_in),
            pl.BlockSpec((TM, 128), lambda b, t: (t, 0)),
            pl.BlockSpec((TM, 128), lambda b, t: (t, 0)),
        ],
        out_specs=[
            pl.BlockSpec((1, TM, D_MODEL), tile),
            pl.BlockSpec((1, TM, D_MODEL), tile),
            pl.BlockSpec((1, TM, SWA_KV_WIDTH), tile),
            pl.BlockSpec((1, TM, SWA_KV_WIDTH), tile),
        ],
        out_shape=[
            jax.ShapeDtypeStruct((bsz, TOK, D_MODEL), BF16),
            jax.ShapeDtypeStruct((bsz, TOK, D_MODEL), BF16),
            jax.ShapeDtypeStruct((bsz, TOK, SWA_KV_WIDTH), BF16),
            jax.ShapeDtypeStruct((bsz, TOK, SWA_KV_WIDTH), BF16),
        ],
        compiler_params=_params("parallel", "parallel"),
    )(*stream, mods, gains, w, cos_t, sin_t)


def _rope_tables():
    t = np.arange(SEQ)
    n_freq = HEAD_DIM // 4
    inv = jnp.asarray(ROPE_THETA, F32) ** (-jnp.arange(n_freq, dtype=F32) / n_freq)
    row = jnp.asarray(t // GRID_W, F32)
    col = jnp.asarray(t % GRID_W, F32)
    ang = jnp.concatenate([row[:, None] * inv, col[:, None] * inv], axis=-1)
    cos, sin = jnp.cos(ang), jnp.sin(ang)
    cos_t = jnp.tile(cos, (1, 4))
    sin_t = jnp.tile(jnp.concatenate([-sin, sin], axis=-1), (1, 2))
    cos_t = jnp.concatenate([cos_t, jnp.ones((CTX_LEN, 128), F32)], axis=0)
    sin_t = jnp.concatenate([sin_t, jnp.zeros((CTX_LEN, 128), F32)], axis=0)
    return cos_t, sin_t


NA_ROW_OFFSETS = 2 * NA_WIN_ROWS - 1
NA_BIAS_BLOCKS = NA_ROW_OFFSETS + 1


def _na_col_mask():
    c = np.arange(GRID_W)
    w0 = np.clip(c - NA_WIN_COLS // 2, 0, GRID_W - NA_WIN_COLS)
    return (c[None, :] >= w0[:, None]) & (c[None, :] < w0[:, None] + NA_WIN_COLS)


_NA_COL_OK = _na_col_mask()


def _na_bias_table(rpb):
    n = GRID_W
    lead = n - NA_WIN_COLS
    layers = rpb.shape[0]
    w = rpb.astype(F32).reshape(layers * NA_HEADS, NA_ROW_OFFSETS, 2 * NA_WIN_COLS - 1)
    w = jnp.pad(w, ((0, 0), (0, 0), (lead, 2 * n - lead - (2 * NA_WIN_COLS - 1))))
    col = jnp.tile(w, (1, 1, n))[..., n - 1:n - 1 + n * (2 * n - 1)]
    col = col.reshape(layers * NA_HEADS, NA_ROW_OFFSETS, n, 2 * n - 1)[..., :n]
    col = jnp.where(jnp.asarray(_NA_COL_OK), col * LOG2E, NEG_INF)
    col = jnp.concatenate([col, jnp.full((layers * NA_HEADS, 1, n, n), NEG_INF, F32)], axis=1)
    return jnp.concatenate([col, col], axis=-1).reshape(layers, NA_HEADS, NA_BIAS_BLOCKS, n, 2 * n)


def _na_bias(bias_ref, head, rb):
    lane = lax.broadcasted_iota(jnp.int32, (1, 128), 1)
    low = lane < GRID_W
    q_row0 = rb * NA_Q_ROWS
    k_row0 = jnp.clip(q_row0 - NA_WIN_ROWS // 2, 0, GRID_ROWS - NA_K_ROWS)
    rows = []
    for i in range(NA_Q_ROWS):
        r = q_row0 + i
        r0 = jnp.clip(r - NA_WIN_ROWS // 2, 0, GRID_ROWS - NA_WIN_ROWS)
        blocks = []
        for j in range(NA_K_ROWS):
            kr = k_row0 + j
            inside = (kr >= r0) & (kr < r0 + NA_WIN_ROWS)
            idx = jnp.where(inside, kr - r + NA_WIN_ROWS - 1, NA_ROW_OFFSETS)
            blocks.append(bias_ref[0, head * NA_BIAS_BLOCKS + idx])
        pieces = [jnp.where(low, blocks[j], blocks[j + 1]) for j in range(0, NA_K_ROWS - 1, 2)]
        pieces.append(blocks[NA_K_ROWS - 1][:, :GRID_W])
        rows.append(jnp.concatenate(pieces, axis=1))
    return jnp.concatenate(rows, axis=0)


def _lane_pair_attention(blocks):
    lane = lax.broadcasted_iota(jnp.int32, (1, 128), 1)
    half = (lane < HEAD_DIM, lane >= HEAD_DIM)

    def scores(j, h):
        qs, keys, _, biases, _ = blocks[j]
        s = [_nt(jnp.where(half[h], q, jnp.zeros_like(q)), k) for q, k in zip(qs, keys)]
        if biases[h] is not None:
            bias = biases[h]() if callable(biases[h]) else biases[h]
            s = [x if b is None else x + b for x, b in zip(s, bias)]
        return s

    def output(j, h, s):
        _, _, values, _, floor = blocks[j]
        m = jnp.max(s[0], axis=-1, keepdims=True)
        for x in s[1:]:
            m = jnp.maximum(m, jnp.max(x, axis=-1, keepdims=True))
        if floor[h] is not None:
            m = jnp.maximum(m, floor[h])
        acc = None
        for x, v in zip(s, values):
            p = jnp.exp2((x - m).astype(BF16))
            va = jnp.where(half[h], v, jnp.ones_like(v))
            pv = _mm(p, va)
            acc = pv if acc is None else acc + pv
        den = pltpu.roll(acc, HEAD_DIM, 1)
        if floor[h] is not None:
            den = den + jnp.exp2(floor[h] - m)
        return acc / den

    heads = [(j, h) for j in range(len(blocks)) for h in range(2)]
    res = {}
    ahead = scores(*heads[0])
    for i, (j, h) in enumerate(heads):
        s = ahead
        if i + 1 < len(heads):
            ahead = scores(*heads[i + 1])
        res[(j, h)] = output(j, h, s)
    return [jnp.where(half[0], res[(j, 0)], res[(j, 1)]) for j in range(len(blocks))]


def _na_kernel(q_ref, k_ref, v_ref, bias_ref, o_ref):
    rb = pl.program_id(2)
    n_blocks = q_ref.shape[2] // 128

    @pl.when(rb < NA_ROW_BLOCKS)
    def _latent():
        k_row0 = jnp.clip(rb * NA_Q_ROWS - NA_WIN_ROWS // 2, 0, GRID_ROWS - NA_K_ROWS)
        start = pl.multiple_of(k_row0 * GRID_W, GRID_W)
        blocks = []
        for j in range(n_blocks):
            ln = slice(j * 128, (j + 1) * 128)
            keys = [k_ref[0, pl.ds(start, NA_KN), ln], k_ref[0, SEQ:, ln]]
            values = [v_ref[0, pl.ds(start, NA_KN), ln], v_ref[0, SEQ:, ln]]
            biases = [functools.partial(lambda head: [_na_bias(bias_ref, head, rb), None], 2 * j + h)
                      for h in range(2)]
            q = q_ref[0, :, ln]
            blocks.append(([q, q], keys, values, biases, [None, None]))
        for j, o in enumerate(_lane_pair_attention(blocks)):
            o_ref[0, :, j * 128:(j + 1) * 128] = o.astype(BF16)

    @pl.when(rb == NA_ROW_BLOCKS)
    def _context():
        blocks = []
        for j in range(n_blocks):
            ln = slice(j * 128, (j + 1) * 128)
            blocks.append(([q_ref[0, :, ln]], [k_ref[0, SEQ:, ln]], [v_ref[0, SEQ:, ln]],
                           [None, None], [None, None]))
        for j, o in enumerate(_lane_pair_attention(blocks)):
            o_ref[0, :, j * 128:(j + 1) * 128] = o.astype(BF16)


def _na_attention(qkv, bias):
    bsz = qkv.shape[0]
    hps = 8
    groups = NA_HEADS // hps
    w = hps * HEAD_DIM
    return pl.pallas_call(
        _na_kernel,
        grid=(groups, bsz, NA_ROW_BLOCKS + 1),
        in_specs=[
            pl.BlockSpec((1, NA_QN, w), lambda p, b, r: (b, r, p)),
            pl.BlockSpec((1, TOK, w), lambda p, b, r: (b, 0, groups + p)),
            pl.BlockSpec((1, TOK, w), lambda p, b, r: (b, 0, 2 * groups + p)),
            pl.BlockSpec((1, hps * NA_BIAS_BLOCKS, GRID_W, 128), lambda p, b, r: (p, 0, 0, 0)),
        ],
        out_specs=pl.BlockSpec((1, NA_QN, w), lambda p, b, r: (b, r, p)),
        out_shape=jax.ShapeDtypeStruct((bsz, TOK, NA_WIDTH), BF16),
        compiler_params=_params("parallel", "parallel", "arbitrary"),
    )(qkv, qkv, qkv, bias.reshape(groups, hps * NA_BIAS_BLOCKS, GRID_W, 128))


def _split3(g):
    hi = g.astype(BF16)
    r1 = g - hi.astype(F32)
    mid = r1.astype(BF16)
    lo = (r1 - mid.astype(F32)).astype(BF16)
    return hi, mid, lo


def _gla_kernel(q_ref, k_ref, v_ref, gate_ref, lr_ref, wa2_ref, ba_ref, gn_ref, o_ref,
                acc_ref, cum_ref, qt_ref, u_ref, dec_ref, sp_ref, st_ref):
    c = GLA_CHUNK
    ii = lax.broadcasted_iota(jnp.int32, (c, c), 0)
    jj = lax.broadcasted_iota(jnp.int32, (c, c), 1)
    incl = (jj <= ii, jj >= ii)
    tri = tuple(jnp.where(m, 1.0, 0.0).astype(BF16) for m in incl)
    head0 = lax.broadcasted_iota(jnp.int32, (1, 128), 1) < GLA_DK
    per_tile = TM // c

    def decays(t, carry):
        rows = pl.ds(pl.multiple_of(t * TM, TM), TM)
        lr = lr_ref[0, rows, :].astype(BF16)
        z2 = _mm(lr, wa2_ref[...]) + ba_ref[...]
        for d in range(2):
            z = z2[:, d * 128:(d + 1) * 128]
            g = (jnp.minimum(z, 0.0) - jnp.log1p(jnp.exp(-jnp.abs(z)))) / GLA_NORMALIZER
            wide = jnp.concatenate([g[i * c:(i + 1) * c] for i in range(per_tile)], axis=1)
            hi, mid, lo = _split3(wide)
            cum = _mm(tri[d], hi) + _mm(tri[d], mid) + _mm(tri[d], lo)
            cum_ref[d, rows, :] = jnp.concatenate([cum[:, i * 128:(i + 1) * 128] for i in range(per_tile)], axis=0)
        return carry

    lax.fori_loop(0, TILES, decays, 0, unroll=3)

    t2 = lax.broadcasted_iota(jnp.int32, (c, 2 * c), 0)
    j2 = lax.broadcasted_iota(jnp.int32, (c, 2 * c), 1) % c
    incl2 = (j2 <= t2, j2 >= t2)

    def by_head(a):
        zero = jnp.zeros_like(a)
        return jnp.concatenate([jnp.where(head0, a, zero), jnp.where(head0, zero, a)], axis=0)

    def intra(ci, carry):
        rows = pl.ds(pl.multiple_of(ci * c, c), c)
        qc = q_ref[0, rows, :]
        kc = k_ref[0, rows, :]
        v2 = v_ref[0, rows, :].astype(BF16)
        zero_v = jnp.zeros((c, GLA_DV), BF16)
        v_diag = jnp.concatenate([jnp.concatenate([v2[:, :GLA_DV], zero_v], axis=1),
                                  jnp.concatenate([zero_v, v2[:, GLA_DV:]], axis=1)], axis=0)
        p_sum = None
        k_ends = []
        for d in range(2):
            cum = cum_ref[d, rows, :]
            tot = cum[c - 1:c, :] if d == 0 else cum[0:1, :]
            q_t = (qc * jnp.exp(cum)).astype(BF16)
            k_t = (kc * jnp.exp(-cum)).astype(BF16)
            k_ends.append((kc * jnp.exp(tot - cum)).astype(BF16))
            dec_ref[d, ci] = jnp.exp(tot)
            qt_ref[ci, :, d * 128:(d + 1) * 128] = by_head(q_t)
            p = jnp.where(incl2[d], _nt(q_t, by_head(k_t)), 0.0)
            p_sum = p if p_sum is None else p_sum + p
        acc_ref[rows, :] = _mm(p_sum.astype(BF16), v_diag)
        uu = _tn(v2, jnp.concatenate(k_ends, axis=1))
        for d in range(2):
            blk = uu[:, d * 128:(d + 1) * 128]
            u_ref[d, ci] = jnp.where(head0, blk[:GLA_DV], blk[GLA_DV:])
        return carry

    lax.fori_loop(0, GLA_CHUNKS, intra, 0, unroll=4)

    st_ref[...] = jnp.zeros_like(st_ref)

    def scan(i, carry):
        order = (jnp.where(i < GLA_CTX_CHUNKS, GLA_CHUNKS - GLA_CTX_CHUNKS + i, i - GLA_CTX_CHUNKS),
                 GLA_CHUNKS - 1 - i)
        for d in range(2):
            s = st_ref[d]
            sp_ref[order[d], :, d * 128:(d + 1) * 128] = s.astype(BF16)
            st_ref[d] = dec_ref[d, order[d]] * s + u_ref[d, order[d]]
        return carry

    lax.fori_loop(0, GLA_CHUNKS, scan, 0)

    def inter(ci, carry):
        rows = pl.ds(pl.multiple_of(ci * c, c), c)
        o = _nt(qt_ref[ci], sp_ref[ci])
        acc_ref[rows, :] = acc_ref[rows, :] + jnp.concatenate([o[:c], o[c:]], axis=1)
        return carry

    lax.fori_loop(0, GLA_CHUNKS, inter, 0, unroll=4)

    def finish(t, carry):
        rows = pl.ds(pl.multiple_of(t * TM, TM), TM)
        gate = gate_ref[0, rows, :]
        sw = gate * jax.nn.sigmoid(gate)
        for h in range(2):
            vs = slice(h * GLA_DV, (h + 1) * GLA_DV)
            o = acc_ref[rows, vs]
            o = o * lax.rsqrt(jnp.mean(o * o, axis=-1, keepdims=True) + EPS)
            o_ref[0, rows, vs] = (o * gn_ref[:, vs] * sw[:, vs]).astype(BF16)
        return carry

    lax.fori_loop(0, TILES, finish, 0)


def _gla(gla_in, wa2, ba, gnorm):
    bsz = gla_in.shape[0]
    pairs = GLA_HEADS // 2
    qk_blocks = GLA_QK_WIDTH // 128
    v_blocks = GLA_V_WIDTH // 256
    v0 = 2 * GLA_QK_WIDTH // 256
    lr_block = (2 * GLA_QK_WIDTH + 2 * GLA_V_WIDTH) // 128
    wa2_rows = jnp.zeros((2, 128, GLA_QK_WIDTH), F32)
    for d in range(2):
        wa2_rows = wa2_rows.at[d, d * GLA_RANK:(d + 1) * GLA_RANK].set(wa2[d])
    wa2 = wa2_rows.reshape(2, 128, pairs, 128).transpose(1, 2, 0, 3).reshape(128, pairs * 256).astype(BF16)
    ba = ba.reshape(2, pairs, 128).transpose(1, 0, 2).reshape(1, pairs * 256)
    return pl.pallas_call(
        _gla_kernel,
        grid=(bsz, pairs),
        in_specs=[
            pl.BlockSpec((1, TOK, 128), lambda b, p: (b, 0, p)),
            pl.BlockSpec((1, TOK, 128), lambda b, p: (b, 0, qk_blocks + p)),
            pl.BlockSpec((1, TOK, 256), lambda b, p: (b, 0, v0 + p)),
            pl.BlockSpec((1, TOK, 256), lambda b, p: (b, 0, v0 + v_blocks + p)),
            pl.BlockSpec((1, TOK, 128), lambda b, p: (b, 0, lr_block)),
            pl.BlockSpec((128, 256), lambda b, p: (0, p)),
            pl.BlockSpec((1, 256), lambda b, p: (0, p)),
            pl.BlockSpec((1, 256), lambda b, p: (0, p)),
        ],
        out_specs=pl.BlockSpec((1, TOK, 256), lambda b, p: (b, 0, p)),
        out_shape=jax.ShapeDtypeStruct((bsz, TOK, GLA_V_WIDTH), BF16),
        scratch_shapes=[
            pltpu.VMEM((TOK, 2 * GLA_DV), F32),
            pltpu.VMEM((2, TOK, 128), F32),
            pltpu.VMEM((GLA_CHUNKS, 2 * GLA_CHUNK, 256), BF16),
            pltpu.VMEM((2, GLA_CHUNKS, GLA_DV, 128), F32),
            pltpu.VMEM((2, GLA_CHUNKS, 1, 128), F32),
            pltpu.VMEM((GLA_CHUNKS, GLA_DV, 256), BF16),
            pltpu.VMEM((2, GLA_DV, 128), F32),
        ],
        compiler_params=_params("parallel", "parallel"),
    )(gla_in, gla_in, gla_in, gla_in, gla_in, wa2, ba, gnorm.reshape(1, GLA_V_WIDTH))


def _swa_kernel(sink_ref, q_ref, qr_ref, k_ref, v_ref, o_ref):
    kp = pl.program_id(1)
    qb = pl.program_id(2)
    kc = k_ref[0, SEQ:, :]
    vc = v_ref[0, SEQ:, :]

    def sinks(j):
        return [sink_ref[kp * 2 * SWA_GROUP + hk * SWA_GROUP + j] for hk in range(2)]

    @pl.when(qb < LAT_TILES)
    def _latent():
        q0 = qb * SWA_TQ
        start = pl.multiple_of(jnp.clip(q0 - SWA_WINDOW, 0, SEQ - SWA_NLOC), SWA_WINDOW)
        kl = k_ref[0, pl.ds(start, SWA_NLOC), :]
        vl = v_ref[0, pl.ds(start, SWA_NLOC), :]
        qpos = q0 + lax.broadcasted_iota(jnp.int32, (SWA_TQ, SWA_NLOC), 0)
        kpos = start + lax.broadcasted_iota(jnp.int32, (SWA_TQ, SWA_NLOC), 1)
        window = jnp.where(jnp.abs(kpos - qpos) <= SWA_WINDOW, 0.0, NEG_INF)
        blocks = []
        for j in range(SWA_GROUP):
            ln = slice(j * 128, (j + 1) * 128)
            blocks.append(([qr_ref[0, :, ln], q_ref[0, :, ln]], [kl, kc], [vl, vc],
                           [[window, None], [window, None]], sinks(j)))
        for j, o in enumerate(_lane_pair_attention(blocks)):
            o_ref[0, :, j * 128:(j + 1) * 128] = o.astype(BF16)

    @pl.when(qb == LAT_TILES)
    def _context():
        blocks = []
        for j in range(SWA_GROUP):
            ln = slice(j * 128, (j + 1) * 128)
            blocks.append(([q_ref[0, :, ln]], [kc], [vc], [None, None], sinks(j)))
        for j, o in enumerate(_lane_pair_attention(blocks)):
            o_ref[0, :, j * 128:(j + 1) * 128] = o.astype(BF16)


def _swa_attention(q, qr, kr, v, sink):
    bsz = q.shape[0]
    pairs = SWA_KV_HEADS // 2
    qw = 2 * SWA_GROUP * HEAD_DIM
    return pl.pallas_call(
        _swa_kernel,
        grid=(bsz, pairs, TILES),
        in_specs=[
            pl.BlockSpec(memory_space=pltpu.SMEM),
            pl.BlockSpec((1, SWA_TQ, qw), lambda b, p, t: (b, t, p)),
            pl.BlockSpec((1, SWA_TQ, qw), lambda b, p, t: (b, t, p)),
            pl.BlockSpec((1, TOK, 128), lambda b, p, t: (b, 0, p)),
            pl.BlockSpec((1, TOK, 128), lambda b, p, t: (b, 0, p)),
        ],
        out_specs=pl.BlockSpec((1, SWA_TQ, qw), lambda b, p, t: (b, t, p)),
        out_shape=jax.ShapeDtypeStruct((bsz, TOK, D_MODEL), BF16),
        compiler_params=_params("parallel", "parallel", "arbitrary"),
    )(sink.astype(F32), q, qr, kr, v)


def _mlp_kernel(*refs, final_norm):
    oa_ref, ob_ref, mod_ref, g_ref, gf_ref, wo_ref, w1_ref, w2_ref, out_ref = refs[-9:]
    mod = mod_ref[0]
    half = wo_ref.shape[0] // 2
    y = _mm(oa_ref[0], wo_ref[:half, :]) + _mm(ob_ref[0], wo_ref[half:, :])
    x = _stream_tile(refs[:-9])
    x1 = x + mod[2:3] * y
    h = _norm_modulate(x1, g_ref[...], mod[3:4], mod[4:5]).astype(BF16)
    acc = jnp.zeros((x.shape[0], D_MODEL), F32)
    for c in range(D_FF // FF_CHUNK):
        t = jnp.maximum(_mm(h, w1_ref[:, c * FF_CHUNK:(c + 1) * FF_CHUNK]), 0.0)
        acc = acc + _mm((t * t).astype(BF16), w2_ref[c * FF_CHUNK:(c + 1) * FF_CHUNK, :])
    x2 = x1 + mod[5:6] * acc
    if final_norm:
        x2 = x2 * lax.rsqrt(jnp.mean(x2 * x2, axis=-1, keepdims=True) + EPS) * gf_ref[...]
    out_ref[0] = x2


def _outproj_mlp(stream, oa, ob, ob_block, l, mods, gains, gain_final, wo, w1, w2):
    bsz = stream[0].shape[0]
    half = D_MODEL // 2
    final_norm = l == DEPTH - 1
    tm = 2 * TM if final_norm else TM
    tiles = (SEQ if final_norm else TOK) // tm
    tile = lambda b, t: (b, t, 0)
    return pl.pallas_call(
        functools.partial(_mlp_kernel, final_norm=final_norm),
        grid=(bsz, tiles),
        in_specs=_stream_specs(stream, tm) + [
            pl.BlockSpec((1, tm, half), tile),
            pl.BlockSpec((1, tm, half), lambda b, t: (b, t, ob_block)),
            _mod_spec(l),
            _layer_spec(l, 1, D_MODEL),
            pl.BlockSpec((1, D_MODEL), lambda b, t: (0, 0)),
            _layer_spec(l // 2, D_MODEL, D_MODEL),
            _layer_spec(l, D_MODEL, D_FF),
            _layer_spec(l, D_FF, D_MODEL),
        ],
        out_specs=pl.BlockSpec((1, tm, D_MODEL), tile),
        out_shape=jax.ShapeDtypeStruct((bsz, tiles * tm, D_MODEL), F32),
        compiler_params=_params("parallel", "parallel"),
    )(*stream, oa, ob, mods, gains, gain_final, wo, w1, w2)


def _even_in_weight(w):
    n_na = 3 * NA_WIDTH
    scale = np.ones((w.shape[-1],), np.float32)
    scale[:NA_WIDTH] = HEAD_DIM ** -0.5 * LOG2E
    scale[n_na:n_na + GLA_QK_WIDTH] = GLA_DK ** -0.5
    w = w * jnp.asarray(scale)
    pad = n_na + GLA_IN_WIDTH - w.shape[-1]
    return jnp.pad(w, ((0, 0), (0, 0), (0, pad))).astype(BF16)


def _odd_in_weight(w):
    n = w.shape[0]
    n_rot = D_MODEL + SWA_KV_WIDTH
    rot = w[..., :n_rot].reshape(n, D_MODEL, n_rot // HEAD_DIM, HEAD_DIM // 2, 2)
    rot = jnp.swapaxes(rot, 3, 4).reshape(n, D_MODEL, n_rot)
    q = _swa_head_order(rot[..., :D_MODEL] * (HEAD_DIM ** -0.5 * LOG2E), axis=2)
    return jnp.concatenate([q, rot[..., D_MODEL:], w[..., n_rot:]], axis=2).astype(BF16)


def _swa_head_order(a, axis):
    shape = a.shape
    split = shape[:axis] + (SWA_KV_HEADS // 2, 2, SWA_GROUP, HEAD_DIM) + shape[axis + 1:]
    return jnp.swapaxes(a.reshape(split), axis + 1, axis + 2).reshape(shape)


def kernel(x, c, ctx, c_ctx, ada_w, ada_b, norm_mix, norm_mlp, mlp_w1, mlp_w2, ab_w_in, ab_w_out, na_rpb,
           gla_wa2, gla_ba, gla_gnorm, swa_w_in, swa_w_out, swa_sink, norm_final):
    bsz = x.shape[0]
    assert x.shape == (bsz, SEQ, D_MODEL) and ctx.shape == (bsz, CTX_LEN, D_MODEL) and bsz <= 8

    cvec = jnp.zeros((MOD_ROWS, D_MODEL), F32).at[:bsz].set(c).at[8].set(c_ctx)
    mods = _ada_table(cvec, ada_w, ada_b).reshape(DEPTH, MOD_ROWS, 6, D_MODEL)
    cos_t, sin_t = _rope_tables()
    gain_final = norm_final.reshape(1, D_MODEL)
    g_mix = norm_mix.reshape(DEPTH, 1, D_MODEL)
    g_mlp = norm_mlp.reshape(DEPTH, 1, D_MODEL)
    w_in_even = _even_in_weight(ab_w_in)
    w_in_odd = _odd_in_weight(swa_w_in)
    w_out_even = ab_w_out.astype(BF16)
    w_out_odd = _swa_head_order(swa_w_out, axis=1).astype(BF16)
    w1 = mlp_w1.astype(BF16)
    w2 = mlp_w2.astype(BF16)
    na_bias = _na_bias_table(na_rpb)

    stream = (x, ctx)
    for l in range(DEPTH):
        j = l // 2
        if l % 2 == 0:
            na_in, gla_in = _inproj_even(stream, l, mods, g_mix, w_in_even)
            oa = _na_attention(na_in, na_bias[j])
            ob = _gla(gla_in, gla_wa2[j], gla_ba[j], gla_gnorm[j])
            ob_block = 0
            wo = w_out_even
        else:
            q, qr, kr, v = _inproj_odd(stream, l, mods, g_mix, w_in_odd, cos_t, sin_t)
            oa = ob = _swa_attention(q, qr, kr, v, swa_sink[j] * LOG2E)
            ob_block = 1
            wo = w_out_odd
        xs = _outproj_mlp(stream, oa, ob, ob_block, l, mods, g_mlp, gain_final, wo, w1, w2)
        stream = (xs,)
    return xs
```

```python
import functools

import numpy as np
import jax
import jax.numpy as jnp
from jax import lax
from jax.experimental import pallas as pl
from jax.experimental.pallas import tpu as pltpu

D_MODEL = 1024
SEQ = 2048
DEPTH = 4
GRID_W = 64
GRID_ROWS = SEQ // GRID_W
CTX_LEN = 256
TOK = SEQ + CTX_LEN
HEAD_DIM = 64
EPS = 1e-6
NEG_INF = -1e30
LOG2E = 1.4426950408889634

NA_HEADS = 8
NA_WIN_ROWS = 8
NA_WIN_COLS = 16
NA_WIDTH = NA_HEADS * HEAD_DIM
NA_Q_ROWS = 4
NA_K_ROWS = NA_Q_ROWS + NA_WIN_ROWS - 1
NA_QN = NA_Q_ROWS * GRID_W
NA_KN = NA_K_ROWS * GRID_W
NA_ROW_BLOCKS = GRID_ROWS // NA_Q_ROWS

GLA_HEADS = 4
GLA_DK = 64
GLA_DV = 128
GLA_RANK = 16
GLA_NORMALIZER = 16.0
GLA_CHUNK = 64
GLA_QK_WIDTH = GLA_HEADS * GLA_DK
GLA_V_WIDTH = GLA_HEADS * GLA_DV
GLA_IN_WIDTH = 2 * GLA_QK_WIDTH + 2 * GLA_V_WIDTH + 128
GLA_CTX_CHUNKS = CTX_LEN // GLA_CHUNK
GLA_CHUNKS = TOK // GLA_CHUNK

SWA_HEADS = 16
SWA_KV_HEADS = 4
SWA_GROUP = SWA_HEADS // SWA_KV_HEADS
SWA_WINDOW = 128
SWA_KV_WIDTH = SWA_KV_HEADS * HEAD_DIM
SWA_TQ = 256
SWA_NLOC = SWA_TQ + 2 * SWA_WINDOW

D_FF = 4 * D_MODEL
FF_CHUNK = 1024
ROPE_THETA = 10000.0

TM = 256
TILES = TOK // TM
LAT_TILES = SEQ // TM
TP = 768
TP_FINAL = 1024
MOD_ROWS = 16
VMEM_LIMIT = 56 * 1024 * 1024

F32 = jnp.float32
BF16 = jnp.bfloat16


def _nt(a, b):
    return lax.dot_general(a, b, (((1,), (1,)), ((), ())), preferred_element_type=F32)


def _tn(a, b):
    return lax.dot_general(a, b, (((0,), (0,)), ((), ())), preferred_element_type=F32)


def _mm(a, b):
    return jnp.dot(a, b, preferred_element_type=F32)


def _params(*sem):
    return pltpu.CompilerParams(dimension_semantics=sem, vmem_limit_bytes=VMEM_LIMIT)


def _mod_specs(l):
    return [
        pl.BlockSpec((None, 1, 6, D_MODEL), lambda b, t: (l, b, 0, 0)),
        pl.BlockSpec((None, 1, 6, D_MODEL), lambda b, t: (l, 8, 0, 0)),
    ]


def _mod_vectors(mod_ref, modc_ref, rows, idx, has_ctx):
    mod = mod_ref[0]
    if not has_ctx:
        return [mod[i:i + 1] for i in idx]
    modc = modc_ref[0]
    row = lax.broadcasted_iota(jnp.int32, (rows, 1), 0)
    is_ctx = (pl.program_id(1) == TOK // rows - 1) & (row >= rows - CTX_LEN)
    return [jnp.where(is_ctx, modc[i:i + 1], mod[i:i + 1]) for i in idx]


def _layer_spec(l, *tail):
    return pl.BlockSpec((None,) + tail, lambda b, t: (l,) + (0,) * len(tail), pipeline_mode=pl.Buffered(1))


def _stream_specs(stream, tm):
    if len(stream) == 1:
        return [pl.BlockSpec((1, tm, D_MODEL), lambda b, t: (b, t, 0))]
    n = tm // CTX_LEN
    last = SEQ // CTX_LEN - 1
    pieces = [pl.BlockSpec((1, CTX_LEN, D_MODEL), functools.partial(
        lambda i, b, t: (b, jnp.minimum(n * t + i, last), 0), i)) for i in range(n)]
    return pieces + [pl.BlockSpec((1, CTX_LEN, D_MODEL), lambda b, t: (b, 0, 0))]


def _stream_args(stream, tm):
    return list(stream) if len(stream) == 1 else [stream[0]] * (tm // CTX_LEN) + [stream[1]]


def _stream_tile(refs):
    if len(refs) == 1:
        return refs[0][0]
    n = len(refs) - 1
    tail = jnp.where(pl.program_id(1) == TOK // (n * CTX_LEN) - 1, refs[n][0], refs[n - 1][0])
    return jnp.concatenate([r[0] for r in refs[:n - 1]] + [tail], axis=0)


def _ada_kernel(c_ref, w_ref, b_ref, o_ref):
    s = c_ref[...]
    s = s * jax.nn.sigmoid(s)
    o_ref[0] = _mm(s.astype(BF16), w_ref[0].astype(BF16)) + b_ref[0]


def _ada_table(cvec, ada_w, ada_b):
    nb = 6 * D_MODEL // 1024
    return pl.pallas_call(
        _ada_kernel,
        grid=(DEPTH, nb),
        in_specs=[
            pl.BlockSpec((MOD_ROWS, D_MODEL), lambda l, n: (0, 0)),
            pl.BlockSpec((1, D_MODEL, 1024), lambda l, n: (l, 0, n)),
            pl.BlockSpec((1, 1, 1024), lambda l, n: (l, 0, n)),
        ],
        out_specs=pl.BlockSpec((1, MOD_ROWS, 1024), lambda l, n: (l, 0, n)),
        out_shape=jax.ShapeDtypeStruct((DEPTH, MOD_ROWS, 6 * D_MODEL), F32),
        compiler_params=_params("parallel", "parallel"),
    )(cvec, ada_w, ada_b.reshape(DEPTH, 1, 6 * D_MODEL))


def _norm_modulate(x, gain, shift, scale):
    y = x * lax.rsqrt(jnp.mean(x * x, axis=-1, keepdims=True) + EPS) * gain
    return y * (1.0 + scale) + shift


def _inproj_even_kernel(*refs):
    mod_ref, modc_ref, g_ref, w_ref, na_ref, gla_ref = refs[-6:]
    shift, scale = _mod_vectors(mod_ref, modc_ref, TP, (0, 1), True)
    h = _norm_modulate(_stream_tile(refs[:-6]), g_ref[...], shift, scale).astype(BF16)
    n_na = 3 * NA_WIDTH
    na_ref[0] = _mm(h, w_ref[:, :n_na]).astype(BF16)
    gla_ref[0] = _mm(h, w_ref[:, n_na:])


def _inproj_even(stream, l, mods, gains, w):
    bsz = stream[0].shape[0]
    n_na = 3 * NA_WIDTH
    return pl.pallas_call(
        _inproj_even_kernel,
        grid=(bsz, TOK // TP),
        in_specs=_stream_specs(stream, TP) + _mod_specs(l) + [
            _layer_spec(l, 1, D_MODEL),
            _layer_spec(l // 2, D_MODEL, n_na + GLA_IN_WIDTH),
        ],
        out_specs=[
            pl.BlockSpec((1, TP, n_na), lambda b, t: (b, t, 0)),
            pl.BlockSpec((1, TP, GLA_IN_WIDTH), lambda b, t: (b, t, 0)),
        ],
        out_shape=[
            jax.ShapeDtypeStruct((bsz, TOK, n_na), BF16),
            jax.ShapeDtypeStruct((bsz, TOK, GLA_IN_WIDTH), F32),
        ],
        compiler_params=_params("parallel", "parallel"),
    )(*_stream_args(stream, TP), mods, mods, gains, w)


def _rope(a, cos, sin, first_half):
    swapped = jnp.where(first_half, pltpu.roll(a, 96, 1), pltpu.roll(a, 32, 1))
    return a * cos + swapped * sin


def _inproj_odd_kernel(*refs):
    mod_ref, modc_ref, g_ref, w_ref, cos_ref, sin_ref, q_ref, qr_ref, kr_ref, v_ref = refs[-10:]
    shift, scale = _mod_vectors(mod_ref, modc_ref, TP, (0, 1), True)
    h = _norm_modulate(_stream_tile(refs[:-10]), g_ref[...], shift, scale).astype(BF16)
    cos = cos_ref[...]
    sin = sin_ref[...]
    first_half = (lax.broadcasted_iota(jnp.int32, (TP, 128), 1) % HEAD_DIM) < HEAD_DIM // 2
    wide = 256
    for j in range(D_MODEL // wide):
        a = _mm(h, w_ref[:, j * wide:(j + 1) * wide])
        q_ref[0, :, j * wide:(j + 1) * wide] = a.astype(BF16)
        for t in range(wide // 128):
            c0 = j * wide + t * 128
            qr_ref[0, :, c0:c0 + 128] = _rope(a[:, t * 128:(t + 1) * 128], cos, sin, first_half).astype(BF16)
    a = _mm(h, w_ref[:, D_MODEL:D_MODEL + SWA_KV_WIDTH])
    for t in range(SWA_KV_WIDTH // 128):
        kr_ref[0, :, t * 128:(t + 1) * 128] = _rope(a[:, t * 128:(t + 1) * 128], cos, sin, first_half).astype(BF16)
    v_ref[0] = _mm(h, w_ref[:, D_MODEL + SWA_KV_WIDTH:]).astype(BF16)


def _inproj_odd(stream, l, mods, gains, w, cos_t, sin_t):
    bsz = stream[0].shape[0]
    n_in = D_MODEL + 2 * SWA_KV_WIDTH
    tile = lambda b, t: (b, t, 0)
    return pl.pallas_call(
        _inproj_odd_kernel,
        grid=(bsz, TOK // TP),
        in_specs=_stream_specs(stream, TP) + _mod_specs(l) + [
            _layer_spec(l, 1, D_MODEL),
            _layer_spec(l // 2, D_MODEL, n_in),
            pl.BlockSpec((TP, 128), lambda b, t: (t, 0)),
            pl.BlockSpec((TP, 128), lambda b, t: (t, 0)),
        ],
        out_specs=[
            pl.BlockSpec((1, TP, D_MODEL), tile),
            pl.BlockSpec((1, TP, D_MODEL), tile),
            pl.BlockSpec((1, TP, SWA_KV_WIDTH), tile),
            pl.BlockSpec((1, TP, SWA_KV_WIDTH), tile),
        ],
        out_shape=[
            jax.ShapeDtypeStruct((bsz, TOK, D_MODEL), BF16),
            jax.ShapeDtypeStruct((bsz, TOK, D_MODEL), BF16),
            jax.ShapeDtypeStruct((bsz, TOK, SWA_KV_WIDTH), BF16),
            jax.ShapeDtypeStruct((bsz, TOK, SWA_KV_WIDTH), BF16),
        ],
        compiler_params=_params("parallel", "parallel"),
    )(*_stream_args(stream, TP), mods, mods, gains, w, cos_t, sin_t)


def _rope_tables():
    t = np.arange(SEQ)
    n_freq = HEAD_DIM // 4
    inv = jnp.asarray(ROPE_THETA, F32) ** (-jnp.arange(n_freq, dtype=F32) / n_freq)
    row = jnp.asarray(t // GRID_W, F32)
    col = jnp.asarray(t % GRID_W, F32)
    ang = jnp.concatenate([row[:, None] * inv, col[:, None] * inv], axis=-1)
    cos, sin = jnp.cos(ang), jnp.sin(ang)
    cos_t = jnp.tile(cos, (1, 4))
    sin_t = jnp.tile(jnp.concatenate([-sin, sin], axis=-1), (1, 2))
    cos_t = jnp.concatenate([cos_t, jnp.ones((CTX_LEN, 128), F32)], axis=0)
    sin_t = jnp.concatenate([sin_t, jnp.zeros((CTX_LEN, 128), F32)], axis=0)
    return cos_t, sin_t


NA_ROW_OFFSETS = 2 * NA_WIN_ROWS - 1
NA_BIAS_BLOCKS = NA_ROW_OFFSETS + 1


def _na_col_mask():
    c = np.arange(GRID_W)
    w0 = np.clip(c - NA_WIN_COLS // 2, 0, GRID_W - NA_WIN_COLS)
    return (c[None, :] >= w0[:, None]) & (c[None, :] < w0[:, None] + NA_WIN_COLS)


_NA_COL_OK = _na_col_mask()


def _na_bias_table(rpb):
    n = GRID_W
    lead = n - NA_WIN_COLS
    layers = rpb.shape[0]
    w = rpb.astype(F32).reshape(layers * NA_HEADS, NA_ROW_OFFSETS, 2 * NA_WIN_COLS - 1)
    w = jnp.pad(w, ((0, 0), (0, 0), (lead, 2 * n - lead - (2 * NA_WIN_COLS - 1))))
    col = jnp.tile(w, (1, 1, n))[..., n - 1:n - 1 + n * (2 * n - 1)]
    col = col.reshape(layers * NA_HEADS, NA_ROW_OFFSETS, n, 2 * n - 1)[..., :n]
    col = jnp.where(jnp.asarray(_NA_COL_OK), col * LOG2E, NEG_INF)
    col = jnp.concatenate([col, jnp.full((layers * NA_HEADS, 1, n, n), NEG_INF, F32)], axis=1)
    return jnp.concatenate([col, col], axis=-1).reshape(layers, NA_HEADS, NA_BIAS_BLOCKS, n, 2 * n)


def _na_bias(bias_ref, head, rb):
    lane = lax.broadcasted_iota(jnp.int32, (1, 128), 1)
    low = lane < GRID_W
    q_row0 = rb * NA_Q_ROWS
    k_row0 = jnp.clip(q_row0 - NA_WIN_ROWS // 2, 0, GRID_ROWS - NA_K_ROWS)
    rows = []
    for i in range(NA_Q_ROWS):
        r = q_row0 + i
        r0 = jnp.clip(r - NA_WIN_ROWS // 2, 0, GRID_ROWS - NA_WIN_ROWS)
        blocks = []
        for j in range(NA_K_ROWS):
            kr = k_row0 + j
            inside = (kr >= r0) & (kr < r0 + NA_WIN_ROWS)
            idx = jnp.where(inside, kr - r + NA_WIN_ROWS - 1, NA_ROW_OFFSETS)
            blocks.append(bias_ref[0, head * NA_BIAS_BLOCKS + idx])
        pieces = [jnp.where(low, blocks[j], blocks[j + 1]) for j in range(0, NA_K_ROWS - 1, 2)]
        pieces.append(blocks[NA_K_ROWS - 1][:, :GRID_W])
        rows.append(jnp.concatenate(pieces, axis=1))
    return jnp.concatenate(rows, axis=0)


def _lane_pair_attention(blocks):
    lane = lax.broadcasted_iota(jnp.int32, (1, 128), 1)
    half = (lane < HEAD_DIM, lane >= HEAD_DIM)

    def scores(j, h):
        qs, keys, _, biases, _ = blocks[j]
        s = [_nt(jnp.where(half[h], q, jnp.zeros_like(q)), k) for q, k in zip(qs, keys)]
        if biases[h] is not None:
            bias = biases[h]() if callable(biases[h]) else biases[h]
            s = [x if b is None else x + b for x, b in zip(s, bias)]
        return s

    def output(j, h, s):
        _, _, values, _, floor = blocks[j]
        m = jnp.max(s[0], axis=-1, keepdims=True)
        for x in s[1:]:
            m = jnp.maximum(m, jnp.max(x, axis=-1, keepdims=True))
        if floor[h] is not None:
            m = jnp.maximum(m, floor[h])
        acc = None
        for x, v in zip(s, values):
            p = jnp.exp2((x - m).astype(BF16))
            va = jnp.where(half[h], v, jnp.ones_like(v))
            pv = _mm(p, va)
            acc = pv if acc is None else acc + pv
        den = pltpu.roll(acc, HEAD_DIM, 1)
        if floor[h] is not None:
            den = den + jnp.exp2(floor[h] - m)
        return acc / den

    heads = [(j, h) for j in range(len(blocks)) for h in range(2)]
    res = {}
    ahead = scores(*heads[0])
    for i, (j, h) in enumerate(heads):
        s = ahead
        if i + 1 < len(heads):
            ahead = scores(*heads[i + 1])
        res[(j, h)] = output(j, h, s)
    return [jnp.where(half[0], res[(j, 0)], res[(j, 1)]) for j in range(len(blocks))]


def _na_kernel(q_ref, k_ref, v_ref, bias_ref, o_ref):
    rb = pl.program_id(2)
    n_blocks = q_ref.shape[2] // 128

    @pl.when(rb < NA_ROW_BLOCKS)
    def _latent():
        k_row0 = jnp.clip(rb * NA_Q_ROWS - NA_WIN_ROWS // 2, 0, GRID_ROWS - NA_K_ROWS)
        start = pl.multiple_of(k_row0 * GRID_W, GRID_W)
        blocks = []
        for j in range(n_blocks):
            ln = slice(j * 128, (j + 1) * 128)
            keys = [k_ref[0, pl.ds(start, NA_KN), ln], k_ref[0, SEQ:, ln]]
            values = [v_ref[0, pl.ds(start, NA_KN), ln], v_ref[0, SEQ:, ln]]
            biases = [functools.partial(lambda head: [_na_bias(bias_ref, head, rb), None], 2 * j + h)
                      for h in range(2)]
            q = q_ref[0, :, ln]
            blocks.append(([q, q], keys, values, biases, [None, None]))
        for j, o in enumerate(_lane_pair_attention(blocks)):
            o_ref[0, :, j * 128:(j + 1) * 128] = o.astype(BF16)

    @pl.when(rb == NA_ROW_BLOCKS)
    def _context():
        blocks = []
        for j in range(n_blocks):
            ln = slice(j * 128, (j + 1) * 128)
            blocks.append(([q_ref[0, :, ln]], [k_ref[0, SEQ:, ln]], [v_ref[0, SEQ:, ln]],
                           [None, None], [None, None]))
        for j, o in enumerate(_lane_pair_attention(blocks)):
            o_ref[0, :, j * 128:(j + 1) * 128] = o.astype(BF16)


def _na_attention(qkv, bias):
    bsz = qkv.shape[0]
    hps = 8
    groups = NA_HEADS // hps
    w = hps * HEAD_DIM
    return pl.pallas_call(
        _na_kernel,
        grid=(groups, bsz, NA_ROW_BLOCKS + 1),
        in_specs=[
            pl.BlockSpec((1, NA_QN, w), lambda p, b, r: (b, r, p)),
            pl.BlockSpec((1, TOK, w), lambda p, b, r: (b, 0, groups + p)),
            pl.BlockSpec((1, TOK, w), lambda p, b, r: (b, 0, 2 * groups + p)),
            pl.BlockSpec((1, hps * NA_BIAS_BLOCKS, GRID_W, 128), lambda p, b, r: (p, 0, 0, 0)),
        ],
        out_specs=pl.BlockSpec((1, NA_QN, w), lambda p, b, r: (b, r, p)),
        out_shape=jax.ShapeDtypeStruct((bsz, TOK, NA_WIDTH), BF16),
        compiler_params=_params("parallel", "parallel", "arbitrary"),
    )(qkv, qkv, qkv, bias.reshape(groups, hps * NA_BIAS_BLOCKS, GRID_W, 128))


def _split3(g):
    hi = g.astype(BF16)
    r1 = g - hi.astype(F32)
    mid = r1.astype(BF16)
    lo = (r1 - mid.astype(F32)).astype(BF16)
    return hi, mid, lo


def _gla_kernel(q_ref, k_ref, v_ref, gate_ref, lr_ref, wa2_ref, ba_ref, gn_ref, o_ref,
                acc_ref, cum_ref, qt_ref, u_ref, dec_ref, sp_ref, st_ref):
    c = GLA_CHUNK
    ii = lax.broadcasted_iota(jnp.int32, (c, c), 0)
    jj = lax.broadcasted_iota(jnp.int32, (c, c), 1)
    incl = (jj <= ii, jj >= ii)
    tri = tuple(jnp.where(m, 1.0, 0.0).astype(BF16) for m in incl)
    head0 = lax.broadcasted_iota(jnp.int32, (1, 128), 1) < GLA_DK
    per_tile = TM // c

    def decays(t, carry):
        rows = pl.ds(pl.multiple_of(t * TM, TM), TM)
        lr = lr_ref[0, rows, :].astype(BF16)
        z2 = _mm(lr, wa2_ref[...]) + ba_ref[...]
        for d in range(2):
            z = z2[:, d * 128:(d + 1) * 128]
            g = (jnp.minimum(z, 0.0) - jnp.log1p(jnp.exp(-jnp.abs(z)))) / GLA_NORMALIZER
            wide = jnp.concatenate([g[i * c:(i + 1) * c] for i in range(per_tile)], axis=1)
            hi, mid, lo = _split3(wide)
            cum = _mm(tri[d], hi) + _mm(tri[d], mid) + _mm(tri[d], lo)
            cum_ref[d, rows, :] = jnp.concatenate([cum[:, i * 128:(i + 1) * 128] for i in range(per_tile)], axis=0)
        return carry

    lax.fori_loop(0, TILES, decays, 0, unroll=3)

    t2 = lax.broadcasted_iota(jnp.int32, (c, 2 * c), 0)
    j2 = lax.broadcasted_iota(jnp.int32, (c, 2 * c), 1) % c
    incl2 = (j2 <= t2, j2 >= t2)

    def by_head(a):
        zero = jnp.zeros_like(a)
        return jnp.concatenate([jnp.where(head0, a, zero), jnp.where(head0, zero, a)], axis=0)

    def weights(ci):
        rows = pl.ds(pl.multiple_of(ci * c, c), c)
        qc = q_ref[0, rows, :]
        kc = k_ref[0, rows, :]
        raw = []
        k_ends = []
        for d in range(2):
            cum = cum_ref[d, rows, :]
            tot = cum[c - 1:c, :] if d == 0 else cum[0:1, :]
            q_t = (qc * jnp.exp(cum)).astype(BF16)
            k_t = (kc * jnp.exp(-cum)).astype(BF16)
            k_ends.append((kc * jnp.exp(tot - cum)).astype(BF16))
            dec_ref[d, ci] = jnp.exp(tot)
            qt_ref[ci, :, d * 128:(d + 1) * 128] = by_head(q_t)
            raw.append(_nt(q_t, by_head(k_t)))
        return raw, k_ends

    def outputs(ci, raw, k_ends):
        rows = pl.ds(pl.multiple_of(ci * c, c), c)
        v2 = v_ref[0, rows, :].astype(BF16)
        zero_v = jnp.zeros((c, GLA_DV), BF16)
        v_diag = jnp.concatenate([jnp.concatenate([v2[:, :GLA_DV], zero_v], axis=1),
                                  jnp.concatenate([zero_v, v2[:, GLA_DV:]], axis=1)], axis=0)
        p_sum = jnp.where(incl2[0], raw[0], 0.0) + jnp.where(incl2[1], raw[1], 0.0)
        acc_ref[rows, :] = _mm(p_sum.astype(BF16), v_diag)
        uu = _tn(v2, jnp.concatenate(k_ends, axis=1))
        for d in range(2):
            blk = uu[:, d * 128:(d + 1) * 128]
            u_ref[d, ci] = jnp.where(head0, blk[:GLA_DV], blk[GLA_DV:])

    group = 4

    def intra(gi, carry):
        ahead = weights(gi * group)
        for i in range(group):
            cur = ahead
            if i + 1 < group:
                ahead = weights(gi * group + i + 1)
            outputs(gi * group + i, *cur)
        return carry

    lax.fori_loop(0, GLA_CHUNKS // group, intra, 0)

    st_ref[...] = jnp.zeros_like(st_ref)

    def scan(i, carry):
        order = (jnp.where(i < GLA_CTX_CHUNKS, GLA_CHUNKS - GLA_CTX_CHUNKS + i, i - GLA_CTX_CHUNKS),
                 GLA_CHUNKS - 1 - i)
        for d in range(2):
            s = st_ref[d]
            sp_ref[order[d], :, d * 128:(d + 1) * 128] = s.astype(BF16)
            st_ref[d] = dec_ref[d, order[d]] * s + u_ref[d, order[d]]
        return carry

    lax.fori_loop(0, GLA_CHUNKS, scan, 0)

    def inter(ci, carry):
        rows = pl.ds(pl.multiple_of(ci * c, c), c)
        o = _nt(qt_ref[ci], sp_ref[ci])
        acc_ref[rows, :] = acc_ref[rows, :] + jnp.concatenate([o[:c], o[c:]], axis=1)
        return carry

    lax.fori_loop(0, GLA_CHUNKS, inter, 0, unroll=4)

    def finish(t, carry):
        rows = pl.ds(pl.multiple_of(t * TM, TM), TM)
        gate = gate_ref[0, rows, :]
        sw = gate * jax.nn.sigmoid(gate)
        for h in range(2):
            vs = slice(h * GLA_DV, (h + 1) * GLA_DV)
            o = acc_ref[rows, vs]
            o = o * lax.rsqrt(jnp.mean(o * o, axis=-1, keepdims=True) + EPS)
            o_ref[0, rows, vs] = (o * gn_ref[:, vs] * sw[:, vs]).astype(BF16)
        return carry

    lax.fori_loop(0, TILES, finish, 0)


def _gla(gla_in, wa2, ba, gnorm):
    bsz = gla_in.shape[0]
    pairs = GLA_HEADS // 2
    qk_blocks = GLA_QK_WIDTH // 128
    v_blocks = GLA_V_WIDTH // 256
    v0 = 2 * GLA_QK_WIDTH // 256
    lr_block = (2 * GLA_QK_WIDTH + 2 * GLA_V_WIDTH) // 128
    wa2_rows = jnp.zeros((2, 128, GLA_QK_WIDTH), F32)
    for d in range(2):
        wa2_rows = wa2_rows.at[d, d * GLA_RANK:(d + 1) * GLA_RANK].set(wa2[d])
    wa2 = wa2_rows.reshape(2, 128, pairs, 128).transpose(1, 2, 0, 3).reshape(128, pairs * 256).astype(BF16)
    ba = ba.reshape(2, pairs, 128).transpose(1, 0, 2).reshape(1, pairs * 256)
    return pl.pallas_call(
        _gla_kernel,
        grid=(bsz, pairs),
        in_specs=[
            pl.BlockSpec((1, TOK, 128), lambda b, p: (b, 0, p)),
            pl.BlockSpec((1, TOK, 128), lambda b, p: (b, 0, qk_blocks + p)),
            pl.BlockSpec((1, TOK, 256), lambda b, p: (b, 0, v0 + p)),
            pl.BlockSpec((1, TOK, 256), lambda b, p: (b, 0, v0 + v_blocks + p)),
            pl.BlockSpec((1, TOK, 128), lambda b, p: (b, 0, lr_block)),
            pl.BlockSpec((128, 256), lambda b, p: (0, p)),
            pl.BlockSpec((1, 256), lambda b, p: (0, p)),
            pl.BlockSpec((1, 256), lambda b, p: (0, p)),
        ],
        out_specs=pl.BlockSpec((1, TOK, 256), lambda b, p: (b, 0, p)),
        out_shape=jax.ShapeDtypeStruct((bsz, TOK, GLA_V_WIDTH), BF16),
        scratch_shapes=[
            pltpu.VMEM((TOK, 2 * GLA_DV), F32),
            pltpu.VMEM((2, TOK, 128), F32),
            pltpu.VMEM((GLA_CHUNKS, 2 * GLA_CHUNK, 256), BF16),
            pltpu.VMEM((2, GLA_CHUNKS, GLA_DV, 128), F32),
            pltpu.VMEM((2, GLA_CHUNKS, 1, 128), F32),
            pltpu.VMEM((GLA_CHUNKS, GLA_DV, 256), BF16),
            pltpu.VMEM((2, GLA_DV, 128), F32),
        ],
        compiler_params=_params("parallel", "parallel"),
    )(gla_in, gla_in, gla_in, gla_in, gla_in, wa2, ba, gnorm.reshape(1, GLA_V_WIDTH))


def _swa_kernel(sink_ref, q_ref, qr_ref, k_ref, v_ref, o_ref):
    kp = pl.program_id(1)
    qb = pl.program_id(2)
    kc = k_ref[0, SEQ:, :]
    vc = v_ref[0, SEQ:, :]

    def sinks(j):
        return [sink_ref[kp * 2 * SWA_GROUP + hk * SWA_GROUP + j] for hk in range(2)]

    @pl.when(qb < LAT_TILES)
    def _latent():
        q0 = qb * SWA_TQ
        start = pl.multiple_of(jnp.clip(q0 - SWA_WINDOW, 0, SEQ - SWA_NLOC), SWA_WINDOW)
        kl = k_ref[0, pl.ds(start, SWA_NLOC), :]
        vl = v_ref[0, pl.ds(start, SWA_NLOC), :]
        qpos = q0 + lax.broadcasted_iota(jnp.int32, (SWA_TQ, SWA_NLOC), 0)
        kpos = start + lax.broadcasted_iota(jnp.int32, (SWA_TQ, SWA_NLOC), 1)
        window = jnp.where(jnp.abs(kpos - qpos) <= SWA_WINDOW, 0.0, NEG_INF)
        blocks = []
        for j in range(SWA_GROUP):
            ln = slice(j * 128, (j + 1) * 128)
            blocks.append(([qr_ref[0, :, ln], q_ref[0, :, ln]], [kl, kc], [vl, vc],
                           [[window, None], [window, None]], sinks(j)))
        for j, o in enumerate(_lane_pair_attention(blocks)):
            o_ref[0, :, j * 128:(j + 1) * 128] = o.astype(BF16)

    @pl.when(qb == LAT_TILES)
    def _context():
        blocks = []
        for j in range(SWA_GROUP):
            ln = slice(j * 128, (j + 1) * 128)
            blocks.append(([q_ref[0, :, ln]], [kc], [vc], [None, None], sinks(j)))
        for j, o in enumerate(_lane_pair_attention(blocks)):
            o_ref[0, :, j * 128:(j + 1) * 128] = o.astype(BF16)


def _swa_attention(q, qr, kr, v, sink):
    bsz = q.shape[0]
    pairs = SWA_KV_HEADS // 2
    qw = 2 * SWA_GROUP * HEAD_DIM
    return pl.pallas_call(
        _swa_kernel,
        grid=(bsz, pairs, TILES),
        in_specs=[
            pl.BlockSpec(memory_space=pltpu.SMEM),
            pl.BlockSpec((1, SWA_TQ, qw), lambda b, p, t: (b, t, p)),
            pl.BlockSpec((1, SWA_TQ, qw), lambda b, p, t: (b, t, p)),
            pl.BlockSpec((1, TOK, 128), lambda b, p, t: (b, 0, p)),
            pl.BlockSpec((1, TOK, 128), lambda b, p, t: (b, 0, p)),
        ],
        out_specs=pl.BlockSpec((1, SWA_TQ, qw), lambda b, p, t: (b, t, p)),
        out_shape=jax.ShapeDtypeStruct((bsz, TOK, D_MODEL), BF16),
        compiler_params=_params("parallel", "parallel", "arbitrary"),
    )(sink.astype(F32), q, qr, kr, v)


def _mlp_kernel(*refs, final_norm):
    oa_ref, ob_ref, mod_ref, modc_ref, g_ref, gf_ref, wo_ref, w1_ref, w2_ref, out_ref = refs[-10:]
    half = wo_ref.shape[0] // 2
    y = _mm(oa_ref[0], wo_ref[:half, :]) + _mm(ob_ref[0], wo_ref[half:, :])
    x = _stream_tile(refs[:-10])
    gate_mix, shift, scale, gate_mlp = _mod_vectors(mod_ref, modc_ref, x.shape[0], (2, 3, 4, 5), not final_norm)
    x1 = x + gate_mix * y
    h = _norm_modulate(x1, g_ref[...], shift, scale).astype(BF16)
    acc = jnp.zeros((x.shape[0], D_MODEL), F32)
    for c in range(D_FF // FF_CHUNK):
        t = jnp.maximum(_mm(h, w1_ref[:, c * FF_CHUNK:(c + 1) * FF_CHUNK]), 0.0)
        acc = acc + _mm((t * t).astype(BF16), w2_ref[c * FF_CHUNK:(c + 1) * FF_CHUNK, :])
    x2 = x1 + gate_mlp * acc
    if final_norm:
        x2 = x2 * lax.rsqrt(jnp.mean(x2 * x2, axis=-1, keepdims=True) + EPS) * gf_ref[...]
    out_ref[0] = x2


def _outproj_mlp(stream, oa, ob, ob_block, l, mods, gains, gain_final, wo, w1, w2):
    bsz = stream[0].shape[0]
    half = D_MODEL // 2
    final_norm = l == DEPTH - 1
    tm = TP_FINAL if final_norm else TP
    tiles = (SEQ if final_norm else TOK) // tm
    tile = lambda b, t: (b, t, 0)
    return pl.pallas_call(
        functools.partial(_mlp_kernel, final_norm=final_norm),
        grid=(bsz, tiles),
        in_specs=_stream_specs(stream, tm) + [
            pl.BlockSpec((1, tm, half), tile),
            pl.BlockSpec((1, tm, half), lambda b, t: (b, t, ob_block)),
        ] + _mod_specs(l) + [
            _layer_spec(l, 1, D_MODEL),
            pl.BlockSpec((1, D_MODEL), lambda b, t: (0, 0)),
            _layer_spec(l // 2, D_MODEL, D_MODEL),
            _layer_spec(l, D_MODEL, D_FF),
            _layer_spec(l, D_FF, D_MODEL),
        ],
        out_specs=pl.BlockSpec((1, tm, D_MODEL), tile),
        out_shape=jax.ShapeDtypeStruct((bsz, tiles * tm, D_MODEL), F32),
        compiler_params=_params("parallel", "parallel"),
    )(*_stream_args(stream, tm), oa, ob, mods, mods, gains, gain_final, wo, w1, w2)


def _even_in_weight(w):
    n_na = 3 * NA_WIDTH
    scale = np.ones((w.shape[-1],), np.float32)
    scale[:NA_WIDTH] = HEAD_DIM ** -0.5 * LOG2E
    scale[n_na:n_na + GLA_QK_WIDTH] = GLA_DK ** -0.5
    w = w * jnp.asarray(scale)
    pad = n_na + GLA_IN_WIDTH - w.shape[-1]
    return jnp.pad(w, ((0, 0), (0, 0), (0, pad))).astype(BF16)


def _odd_in_weight(w):
    n = w.shape[0]
    n_rot = D_MODEL + SWA_KV_WIDTH
    rot = w[..., :n_rot].reshape(n, D_MODEL, n_rot // HEAD_DIM, HEAD_DIM // 2, 2)
    rot = jnp.swapaxes(rot, 3, 4).reshape(n, D_MODEL, n_rot)
    q = _swa_head_order(rot[..., :D_MODEL] * (HEAD_DIM ** -0.5 * LOG2E), axis=2)
    return jnp.concatenate([q, rot[..., D_MODEL:], w[..., n_rot:]], axis=2).astype(BF16)


def _swa_head_order(a, axis):
    shape = a.shape
    split = shape[:axis] + (SWA_KV_HEADS // 2, 2, SWA_GROUP, HEAD_DIM) + shape[axis + 1:]
    return jnp.swapaxes(a.reshape(split), axis + 1, axis + 2).reshape(shape)


def kernel(x, c, ctx, c_ctx, ada_w, ada_b, norm_mix, norm_mlp, mlp_w1, mlp_w2, ab_w_in, ab_w_out, na_rpb,
           gla_wa2, gla_ba, gla_gnorm, swa_w_in, swa_w_out, swa_sink, norm_final):
    bsz = x.shape[0]
    assert x.shape == (bsz, SEQ, D_MODEL) and ctx.shape == (bsz, CTX_LEN, D_MODEL) and bsz <= 8

    cvec = jnp.zeros((MOD_ROWS, D_MODEL), F32).at[:bsz].set(c).at[8].set(c_ctx)
    mods = _ada_table(cvec, ada_w, ada_b).reshape(DEPTH, MOD_ROWS, 6, D_MODEL)
    cos_t, sin_t = _rope_tables()
    gain_final = norm_final.reshape(1, D_MODEL)
    g_mix = norm_mix.reshape(DEPTH, 1, D_MODEL)
    g_mlp = norm_mlp.reshape(DEPTH, 1, D_MODEL)
    w_in_even = _even_in_weight(ab_w_in)
    w_in_odd = _odd_in_weight(swa_w_in)
    w_out_even = ab_w_out.astype(BF16)
    w_out_odd = _swa_head_order(swa_w_out, axis=1).astype(BF16)
    w1 = mlp_w1.astype(BF16)
    w2 = mlp_w2.astype(BF16)
    na_bias = _na_bias_table(na_rpb)

    stream = (x, ctx)
    for l in range(DEPTH):
        j = l // 2
        if l % 2 == 0:
            na_in, gla_in = _inproj_even(stream, l, mods, g_mix, w_in_even)
            oa = _na_attention(na_in, na_bias[j])
            ob = _gla(gla_in, gla_wa2[j], gla_ba[j], gla_gnorm[j])
            ob_block = 0
            wo = w_out_even
        else:
            q, qr, kr, v = _inproj_odd(stream, l, mods, g_mix, w_in_odd, cos_t, sin_t)
            oa = ob = _swa_attention(q, qr, kr, v, swa_sink[j] * LOG2E)
            ob_block = 1
            wo = w_out_odd
        xs = _outproj_mlp(stream, oa, ob, ob_block, l, mods, g_mlp, gain_final, wo, w1, w2)
        stream = (xs,)
    return xs
```

```python
import functools

import numpy as np
import jax
import jax.numpy as jnp
from jax import lax
from jax.experimental import pallas as pl
from jax.experimental.pallas import tpu as pltpu

D_MODEL = 1024
SEQ = 2048
DEPTH = 4
GRID_W = 64
GRID_ROWS = SEQ // GRID_W
CTX_LEN = 256
TOK = SEQ + CTX_LEN
HEAD_DIM = 64
EPS = 1e-6
NEG_INF = -1e30
LOG2E = 1.4426950408889634

NA_HEADS = 8
NA_WIN_ROWS = 8
NA_WIN_COLS = 16
NA_WIDTH = NA_HEADS * HEAD_DIM
NA_Q_ROWS = 4
NA_K_ROWS = NA_Q_ROWS + NA_WIN_ROWS - 1
NA_QN = NA_Q_ROWS * GRID_W
NA_KN = NA_K_ROWS * GRID_W
NA_ROW_BLOCKS = GRID_ROWS // NA_Q_ROWS

GLA_HEADS = 4
GLA_DK = 64
GLA_DV = 128
GLA_RANK = 16
GLA_NORMALIZER = 16.0
GLA_CHUNK = 64
GLA_QK_WIDTH = GLA_HEADS * GLA_DK
GLA_V_WIDTH = GLA_HEADS * GLA_DV
GLA_IN_WIDTH = 2 * GLA_QK_WIDTH + 2 * GLA_V_WIDTH + 128
GLA_CTX_CHUNKS = CTX_LEN // GLA_CHUNK
GLA_CHUNKS = TOK // GLA_CHUNK

SWA_HEADS = 16
SWA_KV_HEADS = 4
SWA_GROUP = SWA_HEADS // SWA_KV_HEADS
SWA_WINDOW = 128
SWA_KV_WIDTH = SWA_KV_HEADS * HEAD_DIM
SWA_TQ = 256
SWA_NLOC = SWA_TQ + 2 * SWA_WINDOW

D_FF = 4 * D_MODEL
FF_CHUNK = 1024
ROPE_THETA = 10000.0

TM = 256
TILES = TOK // TM
LAT_TILES = SEQ // TM
TP = 768
TP_FINAL = 1024
MOD_ROWS = 16
VMEM_LIMIT = 56 * 1024 * 1024

F32 = jnp.float32
BF16 = jnp.bfloat16


def _nt(a, b):
    return lax.dot_general(a, b, (((1,), (1,)), ((), ())), preferred_element_type=F32)


def _tn(a, b):
    return lax.dot_general(a, b, (((0,), (0,)), ((), ())), preferred_element_type=F32)


def _mm(a, b):
    return jnp.dot(a, b, preferred_element_type=F32)


def _params(*sem):
    return pltpu.CompilerParams(dimension_semantics=sem, vmem_limit_bytes=VMEM_LIMIT)


def _mod_specs(l):
    return [
        pl.BlockSpec((None, 1, 6, D_MODEL), lambda b, t: (l, b, 0, 0)),
        pl.BlockSpec((None, 1, 6, D_MODEL), lambda b, t: (l, 8, 0, 0)),
    ]


def _mod_vectors(mod_ref, modc_ref, rows, idx, has_ctx):
    mod = mod_ref[0]
    if not has_ctx:
        return [mod[i:i + 1] for i in idx]
    modc = modc_ref[0]
    row = lax.broadcasted_iota(jnp.int32, (rows, 1), 0)
    is_ctx = (pl.program_id(1) == TOK // rows - 1) & (row >= rows - CTX_LEN)
    return [jnp.where(is_ctx, modc[i:i + 1], mod[i:i + 1]) for i in idx]


def _layer_spec(l, *tail):
    return pl.BlockSpec((None,) + tail, lambda b, t: (l,) + (0,) * len(tail), pipeline_mode=pl.Buffered(1))


def _stream_specs(stream, tm):
    if len(stream) == 1:
        return [pl.BlockSpec((1, tm, D_MODEL), lambda b, t: (b, t, 0))]
    n = tm // CTX_LEN
    last = SEQ // CTX_LEN - 1
    pieces = [pl.BlockSpec((1, CTX_LEN, D_MODEL), functools.partial(
        lambda i, b, t: (b, jnp.minimum(n * t + i, last), 0), i)) for i in range(n)]
    return pieces + [pl.BlockSpec((1, CTX_LEN, D_MODEL), lambda b, t: (b, 0, 0))]


def _stream_args(stream, tm):
    return list(stream) if len(stream) == 1 else [stream[0]] * (tm // CTX_LEN) + [stream[1]]


def _stream_tile(refs):
    if len(refs) == 1:
        return refs[0][0]
    n = len(refs) - 1
    tail = jnp.where(pl.program_id(1) == TOK // (n * CTX_LEN) - 1, refs[n][0], refs[n - 1][0])
    return jnp.concatenate([r[0] for r in refs[:n - 1]] + [tail], axis=0)


def _ada_kernel(c_ref, w_ref, b_ref, o_ref):
    s = c_ref[...]
    s = s * jax.nn.sigmoid(s)
    o_ref[0] = _mm(s.astype(BF16), w_ref[0].astype(BF16)) + b_ref[0]


def _ada_table(cvec, ada_w, ada_b):
    nb = 6 * D_MODEL // 1024
    return pl.pallas_call(
        _ada_kernel,
        grid=(DEPTH, nb),
        in_specs=[
            pl.BlockSpec((MOD_ROWS, D_MODEL), lambda l, n: (0, 0)),
            pl.BlockSpec((1, D_MODEL, 1024), lambda l, n: (l, 0, n)),
            pl.BlockSpec((1, 1, 1024), lambda l, n: (l, 0, n)),
        ],
        out_specs=pl.BlockSpec((1, MOD_ROWS, 1024), lambda l, n: (l, 0, n)),
        out_shape=jax.ShapeDtypeStruct((DEPTH, MOD_ROWS, 6 * D_MODEL), F32),
        compiler_params=_params("parallel", "parallel"),
    )(cvec, ada_w, ada_b.reshape(DEPTH, 1, 6 * D_MODEL))


def _norm_modulate(x, gain, shift, scale):
    y = x * lax.rsqrt(jnp.mean(x * x, axis=-1, keepdims=True) + EPS) * gain
    return y * (1.0 + scale) + shift


def _inproj_even_kernel(*refs):
    mod_ref, modc_ref, g_ref, w_ref, na_ref, gla_ref = refs[-6:]
    shift, scale = _mod_vectors(mod_ref, modc_ref, TP, (0, 1), True)
    h = _norm_modulate(_stream_tile(refs[:-6]), g_ref[...], shift, scale).astype(BF16)
    n_na = 3 * NA_WIDTH
    na_ref[0] = _mm(h, w_ref[:, :n_na]).astype(BF16)
    gla_ref[0] = _mm(h, w_ref[:, n_na:])


def _inproj_even(stream, l, mods, gains, w):
    bsz = stream[0].shape[0]
    n_na = 3 * NA_WIDTH
    return pl.pallas_call(
        _inproj_even_kernel,
        grid=(bsz, TOK // TP),
        in_specs=_stream_specs(stream, TP) + _mod_specs(l) + [
            _layer_spec(l, 1, D_MODEL),
            _layer_spec(l // 2, D_MODEL, n_na + GLA_IN_WIDTH),
        ],
        out_specs=[
            pl.BlockSpec((1, TP, n_na), lambda b, t: (b, t, 0)),
            pl.BlockSpec((1, TP, GLA_IN_WIDTH), lambda b, t: (b, t, 0)),
        ],
        out_shape=[
            jax.ShapeDtypeStruct((bsz, TOK, n_na), BF16),
            jax.ShapeDtypeStruct((bsz, TOK, GLA_IN_WIDTH), F32),
        ],
        compiler_params=_params("parallel", "parallel"),
    )(*_stream_args(stream, TP), mods, mods, gains, w)


def _rope(a, cos, sin, first_half):
    swapped = jnp.where(first_half, pltpu.roll(a, 96, 1), pltpu.roll(a, 32, 1))
    return a * cos + swapped * sin


def _inproj_odd_kernel(*refs):
    mod_ref, modc_ref, g_ref, w_ref, cos_ref, sin_ref, q_ref, qr_ref, kr_ref, v_ref = refs[-10:]
    shift, scale = _mod_vectors(mod_ref, modc_ref, TP, (0, 1), True)
    h = _norm_modulate(_stream_tile(refs[:-10]), g_ref[...], shift, scale).astype(BF16)
    cos = cos_ref[...]
    sin = sin_ref[...]
    first_half = (lax.broadcasted_iota(jnp.int32, (TP, 128), 1) % HEAD_DIM) < HEAD_DIM // 2
    wide = 256
    for j in range(D_MODEL // wide):
        a = _mm(h, w_ref[:, j * wide:(j + 1) * wide])
        q_ref[0, :, j * wide:(j + 1) * wide] = a.astype(BF16)
        for t in range(wide // 128):
            c0 = j * wide + t * 128
            qr_ref[0, :, c0:c0 + 128] = _rope(a[:, t * 128:(t + 1) * 128], cos, sin, first_half).astype(BF16)
    a = _mm(h, w_ref[:, D_MODEL:D_MODEL + SWA_KV_WIDTH])
    for t in range(SWA_KV_WIDTH // 128):
        kr_ref[0, :, t * 128:(t + 1) * 128] = _rope(a[:, t * 128:(t + 1) * 128], cos, sin, first_half).astype(BF16)
    v_ref[0] = _mm(h, w_ref[:, D_MODEL + SWA_KV_WIDTH:]).astype(BF16)


def _inproj_odd(stream, l, mods, gains, w, cos_t, sin_t):
    bsz = stream[0].shape[0]
    n_in = D_MODEL + 2 * SWA_KV_WIDTH
    tile = lambda b, t: (b, t, 0)
    return pl.pallas_call(
        _inproj_odd_kernel,
        grid=(bsz, TOK // TP),
        in_specs=_stream_specs(stream, TP) + _mod_specs(l) + [
            _layer_spec(l, 1, D_MODEL),
            _layer_spec(l // 2, D_MODEL, n_in),
            pl.BlockSpec((TP, 128), lambda b, t: (t, 0)),
            pl.BlockSpec((TP, 128), lambda b, t: (t, 0)),
        ],
        out_specs=[
            pl.BlockSpec((1, TP, D_MODEL), tile),
            pl.BlockSpec((1, TP, D_MODEL), tile),
            pl.BlockSpec((1, TP, SWA_KV_WIDTH), tile),
            pl.BlockSpec((1, TP, SWA_KV_WIDTH), tile),
        ],
        out_shape=[
            jax.ShapeDtypeStruct((bsz, TOK, D_MODEL), BF16),
            jax.ShapeDtypeStruct((bsz, TOK, D_MODEL), BF16),
            jax.ShapeDtypeStruct((bsz, TOK, SWA_KV_WIDTH), BF16),
            jax.ShapeDtypeStruct((bsz, TOK, SWA_KV_WIDTH), BF16),
        ],
        compiler_params=_params("parallel", "parallel"),
    )(*_stream_args(stream, TP), mods, mods, gains, w, cos_t, sin_t)


def _rope_tables():
    t = np.arange(SEQ)
    n_freq = HEAD_DIM // 4
    inv = jnp.asarray(ROPE_THETA, F32) ** (-jnp.arange(n_freq, dtype=F32) / n_freq)
    row = jnp.asarray(t // GRID_W, F32)
    col = jnp.asarray(t % GRID_W, F32)
    ang = jnp.concatenate([row[:, None] * inv, col[:, None] * inv], axis=-1)
    cos, sin = jnp.cos(ang), jnp.sin(ang)
    cos_t = jnp.tile(cos, (1, 4))
    sin_t = jnp.tile(jnp.concatenate([-sin, sin], axis=-1), (1, 2))
    cos_t = jnp.concatenate([cos_t, jnp.ones((CTX_LEN, 128), F32)], axis=0)
    sin_t = jnp.concatenate([sin_t, jnp.zeros((CTX_LEN, 128), F32)], axis=0)
    return cos_t, sin_t


NA_ROW_OFFSETS = 2 * NA_WIN_ROWS - 1
NA_BIAS_BLOCKS = NA_ROW_OFFSETS + 1


def _na_col_mask():
    c = np.arange(GRID_W)
    w0 = np.clip(c - NA_WIN_COLS // 2, 0, GRID_W - NA_WIN_COLS)
    return (c[None, :] >= w0[:, None]) & (c[None, :] < w0[:, None] + NA_WIN_COLS)


_NA_COL_OK = _na_col_mask()


def _na_bias_table(rpb):
    n = GRID_W
    lead = n - NA_WIN_COLS
    layers = rpb.shape[0]
    w = rpb.astype(F32).reshape(layers * NA_HEADS, NA_ROW_OFFSETS, 2 * NA_WIN_COLS - 1)
    w = jnp.pad(w, ((0, 0), (0, 0), (lead, 2 * n - lead - (2 * NA_WIN_COLS - 1))))
    col = jnp.tile(w, (1, 1, n))[..., n - 1:n - 1 + n * (2 * n - 1)]
    col = col.reshape(layers * NA_HEADS, NA_ROW_OFFSETS, n, 2 * n - 1)[..., :n]
    col = jnp.where(jnp.asarray(_NA_COL_OK), col * LOG2E, NEG_INF)
    col = jnp.concatenate([col, jnp.full((layers * NA_HEADS, 1, n, n), NEG_INF, F32)], axis=1)
    return jnp.concatenate([col, col], axis=-1).reshape(layers, NA_HEADS, NA_BIAS_BLOCKS, n, 2 * n)


def _na_bias(bias_ref, head, rb):
    lane = lax.broadcasted_iota(jnp.int32, (1, 128), 1)
    low = lane < GRID_W
    q_row0 = rb * NA_Q_ROWS
    k_row0 = jnp.clip(q_row0 - NA_WIN_ROWS // 2, 0, GRID_ROWS - NA_K_ROWS)
    rows = []
    for i in range(NA_Q_ROWS):
        r = q_row0 + i
        r0 = jnp.clip(r - NA_WIN_ROWS // 2, 0, GRID_ROWS - NA_WIN_ROWS)
        blocks = []
        for j in range(NA_K_ROWS):
            kr = k_row0 + j
            inside = (kr >= r0) & (kr < r0 + NA_WIN_ROWS)
            idx = jnp.where(inside, kr - r + NA_WIN_ROWS - 1, NA_ROW_OFFSETS)
            blocks.append(bias_ref[0, head * NA_BIAS_BLOCKS + idx])
        pieces = [jnp.where(low, blocks[j], blocks[j + 1]) for j in range(0, NA_K_ROWS - 1, 2)]
        pieces.append(blocks[NA_K_ROWS - 1][:, :GRID_W])
        rows.append(jnp.concatenate(pieces, axis=1))
    return jnp.concatenate(rows, axis=0)


def _lane_pair_attention(blocks):
    lane = lax.broadcasted_iota(jnp.int32, (1, 128), 1)
    half = (lane < HEAD_DIM, lane >= HEAD_DIM)

    def scores(j, h):
        qs, keys, _, biases, _ = blocks[j]
        s = [_nt(jnp.where(half[h], q, jnp.zeros_like(q)), k) for q, k in zip(qs, keys)]
        if biases[h] is not None:
            bias = biases[h]() if callable(biases[h]) else biases[h]
            s = [x if b is None else x + b for x, b in zip(s, bias)]
        return s

    def output(j, h, s):
        _, _, values, _, floor = blocks[j]
        m = jnp.max(s[0], axis=-1, keepdims=True)
        for x in s[1:]:
            m = jnp.maximum(m, jnp.max(x, axis=-1, keepdims=True))
        if floor[h] is not None:
            m = jnp.maximum(m, floor[h])
        acc = None
        for x, v in zip(s, values):
            p = jnp.exp2((x - m).astype(BF16))
            va = jnp.where(half[h], v, jnp.ones_like(v))
            pv = _mm(p, va)
            acc = pv if acc is None else acc + pv
        den = pltpu.roll(acc, HEAD_DIM, 1)
        if floor[h] is not None:
            den = den + jnp.exp2(floor[h] - m)
        return acc / den

    heads = [(j, h) for j in range(len(blocks)) for h in range(2)]
    res = {}
    ahead = scores(*heads[0])
    for i, (j, h) in enumerate(heads):
        s = ahead
        if i + 1 < len(heads):
            ahead = scores(*heads[i + 1])
        res[(j, h)] = output(j, h, s)
    return [jnp.where(half[0], res[(j, 0)], res[(j, 1)]) for j in range(len(blocks))]


def _na_kernel(q_ref, k_ref, v_ref, bias_ref, o_ref):
    rb = pl.program_id(2)
    n_blocks = q_ref.shape[2] // 128

    @pl.when(rb < NA_ROW_BLOCKS)
    def _latent():
        k_row0 = jnp.clip(rb * NA_Q_ROWS - NA_WIN_ROWS // 2, 0, GRID_ROWS - NA_K_ROWS)
        start = pl.multiple_of(k_row0 * GRID_W, GRID_W)
        blocks = []
        for j in range(n_blocks):
            ln = slice(j * 128, (j + 1) * 128)
            keys = [k_ref[0, pl.ds(start, NA_KN), ln], k_ref[0, SEQ:, ln]]
            values = [v_ref[0, pl.ds(start, NA_KN), ln], v_ref[0, SEQ:, ln]]
            biases = [functools.partial(lambda head: [_na_bias(bias_ref, head, rb), None], 2 * j + h)
                      for h in range(2)]
            q = q_ref[0, :, ln]
            blocks.append(([q, q], keys, values, biases, [None, None]))
        for j, o in enumerate(_lane_pair_attention(blocks)):
            o_ref[0, :, j * 128:(j + 1) * 128] = o.astype(BF16)

    @pl.when(rb == NA_ROW_BLOCKS)
    def _context():
        blocks = []
        for j in range(n_blocks):
            ln = slice(j * 128, (j + 1) * 128)
            blocks.append(([q_ref[0, :, ln]], [k_ref[0, SEQ:, ln]], [v_ref[0, SEQ:, ln]],
                           [None, None], [None, None]))
        for j, o in enumerate(_lane_pair_attention(blocks)):
            o_ref[0, :, j * 128:(j + 1) * 128] = o.astype(BF16)


def _na_attention(qkv, bias):
    bsz = qkv.shape[0]
    hps = 8
    groups = NA_HEADS // hps
    w = hps * HEAD_DIM
    return pl.pallas_call(
        _na_kernel,
        grid=(groups, bsz, NA_ROW_BLOCKS + 1),
        in_specs=[
            pl.BlockSpec((1, NA_QN, w), lambda p, b, r: (b, r, p)),
            pl.BlockSpec((1, TOK, w), lambda p, b, r: (b, 0, groups + p)),
            pl.BlockSpec((1, TOK, w), lambda p, b, r: (b, 0, 2 * groups + p)),
            pl.BlockSpec((1, hps * NA_BIAS_BLOCKS, GRID_W, 128), lambda p, b, r: (p, 0, 0, 0)),
        ],
        out_specs=pl.BlockSpec((1, NA_QN, w), lambda p, b, r: (b, r, p)),
        out_shape=jax.ShapeDtypeStruct((bsz, TOK, NA_WIDTH), BF16),
        compiler_params=_params("parallel", "parallel", "arbitrary"),
    )(qkv, qkv, qkv, bias.reshape(groups, hps * NA_BIAS_BLOCKS, GRID_W, 128))


def _block_rows(i, size):
    start = i * size
    return pl.ds(start if isinstance(start, int) else pl.multiple_of(start, size), size)


def _split3(g):
    hi = g.astype(BF16)
    r1 = g - hi.astype(F32)
    mid = r1.astype(BF16)
    lo = (r1 - mid.astype(F32)).astype(BF16)
    return hi, mid, lo


def _gla_kernel(q_ref, k_ref, v_ref, gate_ref, lr_ref, wa2_ref, ba_ref, gn_ref, o_ref,
                acc_ref, cum_ref, qt_ref, u_ref, dec_ref, sp_ref, st_ref):
    c = GLA_CHUNK
    ii = lax.broadcasted_iota(jnp.int32, (c, c), 0)
    jj = lax.broadcasted_iota(jnp.int32, (c, c), 1)
    incl = (jj <= ii, jj >= ii)
    tri = tuple(jnp.where(m, 1.0, 0.0).astype(BF16) for m in incl)
    head0 = lax.broadcasted_iota(jnp.int32, (1, 128), 1) < GLA_DK
    per_tile = TM // c

    def decays(t, carry):
        rows = _block_rows(t, TM)
        lr = lr_ref[0, rows, :].astype(BF16)
        z2 = _mm(lr, wa2_ref[...]) + ba_ref[...]
        for d in range(2):
            z = z2[:, d * 128:(d + 1) * 128]
            g = (jnp.minimum(z, 0.0) - jnp.log1p(jnp.exp(-jnp.abs(z)))) / GLA_NORMALIZER
            wide = jnp.concatenate([g[i * c:(i + 1) * c] for i in range(per_tile)], axis=1)
            hi, mid, lo = _split3(wide)
            cum = _mm(tri[d], hi) + _mm(tri[d], mid) + _mm(tri[d], lo)
            cum_ref[d, rows, :] = jnp.concatenate([cum[:, i * 128:(i + 1) * 128] for i in range(per_tile)], axis=0)
        return carry

    t2 = lax.broadcasted_iota(jnp.int32, (c, 2 * c), 0)
    j2 = lax.broadcasted_iota(jnp.int32, (c, 2 * c), 1) % c
    incl2 = (j2 <= t2, j2 >= t2)

    def by_head(a):
        zero = jnp.zeros_like(a)
        return jnp.concatenate([jnp.where(head0, a, zero), jnp.where(head0, zero, a)], axis=0)

    def weights(ci):
        rows = _block_rows(ci, c)
        qc = q_ref[0, rows, :]
        kc = k_ref[0, rows, :]
        raw = []
        k_ends = []
        for d in range(2):
            cum = cum_ref[d, rows, :]
            tot = cum[c - 1:c, :] if d == 0 else cum[0:1, :]
            q_t = (qc * jnp.exp(cum)).astype(BF16)
            k_t = (kc * jnp.exp(-cum)).astype(BF16)
            k_ends.append((kc * jnp.exp(tot - cum)).astype(BF16))
            dec_ref[d, ci] = jnp.exp(tot)
            qt_ref[ci, :, d * 128:(d + 1) * 128] = by_head(q_t)
            raw.append(_nt(q_t, by_head(k_t)))
        return raw, k_ends

    def outputs(ci, raw, k_ends):
        rows = _block_rows(ci, c)
        v2 = v_ref[0, rows, :].astype(BF16)
        zero_v = jnp.zeros((c, GLA_DV), BF16)
        v_diag = jnp.concatenate([jnp.concatenate([v2[:, :GLA_DV], zero_v], axis=1),
                                  jnp.concatenate([zero_v, v2[:, GLA_DV:]], axis=1)], axis=0)
        p_sum = jnp.where(incl2[0], raw[0], 0.0) + jnp.where(incl2[1], raw[1], 0.0)
        acc_ref[rows, :] = _mm(p_sum.astype(BF16), v_diag)
        uu = _tn(v2, jnp.concatenate(k_ends, axis=1))
        for d in range(2):
            blk = uu[:, d * 128:(d + 1) * 128]
            u_ref[d, ci] = jnp.where(head0, blk[:GLA_DV], blk[GLA_DV:])

    def intra(t):
        ahead = weights(t * per_tile)
        for i in range(per_tile):
            cur = ahead
            if i + 1 < per_tile:
                ahead = weights(t * per_tile + i + 1)
            outputs(t * per_tile + i, *cur)

    def decays_then_intra(t, carry):
        decays(t + 1, carry)
        intra(t)
        return carry

    decays(0, 0)
    lax.fori_loop(0, TILES - 1, decays_then_intra, 0)
    intra(TILES - 1)

    st_ref[...] = jnp.zeros_like(st_ref)

    def scan(i, carry):
        order = (jnp.where(i < GLA_CTX_CHUNKS, GLA_CHUNKS - GLA_CTX_CHUNKS + i, i - GLA_CTX_CHUNKS),
                 GLA_CHUNKS - 1 - i)
        for d in range(2):
            s = st_ref[d]
            sp_ref[order[d], :, d * 128:(d + 1) * 128] = s.astype(BF16)
            st_ref[d] = dec_ref[d, order[d]] * s + u_ref[d, order[d]]
        return carry

    lax.fori_loop(0, GLA_CHUNKS, scan, 0)

    def inter(t):
        for i in range(per_tile):
            ci = t * per_tile + i
            rows = _block_rows(ci, c)
            o = _nt(qt_ref[ci], sp_ref[ci])
            acc_ref[rows, :] = acc_ref[rows, :] + jnp.concatenate([o[:c], o[c:]], axis=1)

    def finish(t):
        rows = _block_rows(t, TM)
        gate = gate_ref[0, rows, :]
        sw = gate * jax.nn.sigmoid(gate)
        for h in range(2):
            vs = slice(h * GLA_DV, (h + 1) * GLA_DV)
            o = acc_ref[rows, vs]
            o = o * lax.rsqrt(jnp.mean(o * o, axis=-1, keepdims=True) + EPS)
            o_ref[0, rows, vs] = (o * gn_ref[:, vs] * sw[:, vs]).astype(BF16)

    def inter_then_finish(t, carry):
        inter(t)
        finish(t - 1)
        return carry

    inter(0)
    lax.fori_loop(1, TILES, inter_then_finish, 0)
    finish(TILES - 1)


def _gla(gla_in, wa2, ba, gnorm):
    bsz = gla_in.shape[0]
    pairs = GLA_HEADS // 2
    qk_blocks = GLA_QK_WIDTH // 128
    v_blocks = GLA_V_WIDTH // 256
    v0 = 2 * GLA_QK_WIDTH // 256
    lr_block = (2 * GLA_QK_WIDTH + 2 * GLA_V_WIDTH) // 128
    wa2_rows = jnp.zeros((2, 128, GLA_QK_WIDTH), F32)
    for d in range(2):
        wa2_rows = wa2_rows.at[d, d * GLA_RANK:(d + 1) * GLA_RANK].set(wa2[d])
    wa2 = wa2_rows.reshape(2, 128, pairs, 128).transpose(1, 2, 0, 3).reshape(128, pairs * 256).astype(BF16)
    ba = ba.reshape(2, pairs, 128).transpose(1, 0, 2).reshape(1, pairs * 256)
    return pl.pallas_call(
        _gla_kernel,
        grid=(bsz, pairs),
        in_specs=[
            pl.BlockSpec((1, TOK, 128), lambda b, p: (b, 0, p)),
            pl.BlockSpec((1, TOK, 128), lambda b, p: (b, 0, qk_blocks + p)),
            pl.BlockSpec((1, TOK, 256), lambda b, p: (b, 0, v0 + p)),
            pl.BlockSpec((1, TOK, 256), lambda b, p: (b, 0, v0 + v_blocks + p)),
            pl.BlockSpec((1, TOK, 128), lambda b, p: (b, 0, lr_block)),
            pl.BlockSpec((128, 256), lambda b, p: (0, p)),
            pl.BlockSpec((1, 256), lambda b, p: (0, p)),
            pl.BlockSpec((1, 256), lambda b, p: (0, p)),
        ],
        out_specs=pl.BlockSpec((1, TOK, 256), lambda b, p: (b, 0, p)),
        out_shape=jax.ShapeDtypeStruct((bsz, TOK, GLA_V_WIDTH), BF16),
        scratch_shapes=[
            pltpu.VMEM((TOK, 2 * GLA_DV), F32),
            pltpu.VMEM((2, TOK, 128), F32),
            pltpu.VMEM((GLA_CHUNKS, 2 * GLA_CHUNK, 256), BF16),
            pltpu.VMEM((2, GLA_CHUNKS, GLA_DV, 128), F32),
            pltpu.VMEM((2, GLA_CHUNKS, 1, 128), F32),
            pltpu.VMEM((GLA_CHUNKS, GLA_DV, 256), BF16),
            pltpu.VMEM((2, GLA_DV, 128), F32),
        ],
        compiler_params=_params("parallel", "parallel"),
    )(gla_in, gla_in, gla_in, gla_in, gla_in, wa2, ba, gnorm.reshape(1, GLA_V_WIDTH))


def _swa_kernel(sink_ref, q_ref, qr_ref, k_ref, v_ref, o_ref):
    qb = pl.program_id(1)
    pairs = SWA_KV_HEADS // 2

    def sinks(kp, j):
        return [sink_ref[(2 * kp + hk) * SWA_GROUP + j] for hk in range(2)]

    def lanes(kp, j):
        return slice((kp * SWA_GROUP + j) * 128, (kp * SWA_GROUP + j + 1) * 128)

    @pl.when(qb < LAT_TILES)
    def _latent():
        q0 = qb * SWA_TQ
        start = pl.multiple_of(jnp.clip(q0 - SWA_WINDOW, 0, SEQ - SWA_NLOC), SWA_WINDOW)
        qpos = q0 + lax.broadcasted_iota(jnp.int32, (SWA_TQ, SWA_NLOC), 0)
        kpos = start + lax.broadcasted_iota(jnp.int32, (SWA_TQ, SWA_NLOC), 1)
        window = jnp.where(jnp.abs(kpos - qpos) <= SWA_WINDOW, 0.0, NEG_INF)
        blocks = []
        for kp in range(pairs):
            kv = slice(kp * 128, (kp + 1) * 128)
            keys = [k_ref[0, pl.ds(start, SWA_NLOC), kv], k_ref[0, SEQ:, kv]]
            values = [v_ref[0, pl.ds(start, SWA_NLOC), kv], v_ref[0, SEQ:, kv]]
            for j in range(SWA_GROUP):
                ln = lanes(kp, j)
                blocks.append(([qr_ref[0, :, ln], q_ref[0, :, ln]], keys, values,
                               [[window, None], [window, None]], sinks(kp, j)))
        for i, o in enumerate(_lane_pair_attention(blocks)):
            o_ref[0, :, i * 128:(i + 1) * 128] = o.astype(BF16)

    @pl.when(qb == LAT_TILES)
    def _context():
        blocks = []
        for kp in range(pairs):
            kv = slice(kp * 128, (kp + 1) * 128)
            for j in range(SWA_GROUP):
                blocks.append(([q_ref[0, :, lanes(kp, j)]], [k_ref[0, SEQ:, kv]], [v_ref[0, SEQ:, kv]],
                               [None, None], sinks(kp, j)))
        for i, o in enumerate(_lane_pair_attention(blocks)):
            o_ref[0, :, i * 128:(i + 1) * 128] = o.astype(BF16)


def _swa_attention(q, qr, kr, v, sink):
    bsz = q.shape[0]
    tile = lambda b, t: (b, t, 0)
    whole = lambda b, t: (b, 0, 0)
    return pl.pallas_call(
        _swa_kernel,
        grid=(bsz, TILES),
        in_specs=[
            pl.BlockSpec(memory_space=pltpu.SMEM),
            pl.BlockSpec((1, SWA_TQ, D_MODEL), tile),
            pl.BlockSpec((1, SWA_TQ, D_MODEL), tile),
            pl.BlockSpec((1, TOK, SWA_KV_WIDTH), whole),
            pl.BlockSpec((1, TOK, SWA_KV_WIDTH), whole),
        ],
        out_specs=pl.BlockSpec((1, SWA_TQ, D_MODEL), tile),
        out_shape=jax.ShapeDtypeStruct((bsz, TOK, D_MODEL), BF16),
        compiler_params=_params("parallel", "arbitrary"),
    )(sink.astype(F32), q, qr, kr, v)


def _mlp_kernel(*refs, final_norm):
    oa_ref, ob_ref, mod_ref, modc_ref, g_ref, gf_ref, wo_ref, w1_ref, w2_ref, out_ref = refs[-10:]
    half = wo_ref.shape[0] // 2
    y = _mm(oa_ref[0], wo_ref[:half, :]) + _mm(ob_ref[0], wo_ref[half:, :])
    x = _stream_tile(refs[:-10])
    gate_mix, shift, scale, gate_mlp = _mod_vectors(mod_ref, modc_ref, x.shape[0], (2, 3, 4, 5), not final_norm)
    x1 = x + gate_mix * y
    h = _norm_modulate(x1, g_ref[...], shift, scale).astype(BF16)
    acc = jnp.zeros((x.shape[0], D_MODEL), F32)
    for c in range(D_FF // FF_CHUNK):
        t = jnp.maximum(_mm(h, w1_ref[:, c * FF_CHUNK:(c + 1) * FF_CHUNK]), 0.0)
        acc = acc + _mm((t * t).astype(BF16), w2_ref[c * FF_CHUNK:(c + 1) * FF_CHUNK, :])
    x2 = x1 + gate_mlp * acc
    if final_norm:
        x2 = x2 * lax.rsqrt(jnp.mean(x2 * x2, axis=-1, keepdims=True) + EPS) * gf_ref[...]
    out_ref[0] = x2


def _outproj_mlp(stream, oa, ob, ob_block, l, mods, gains, gain_final, wo, w1, w2):
    bsz = stream[0].shape[0]
    half = D_MODEL // 2
    final_norm = l == DEPTH - 1
    tm = TP_FINAL if final_norm else TP
    tiles = (SEQ if final_norm else TOK) // tm
    tile = lambda b, t: (b, t, 0)
    return pl.pallas_call(
        functools.partial(_mlp_kernel, final_norm=final_norm),
        grid=(bsz, tiles),
        in_specs=_stream_specs(stream, tm) + [
            pl.BlockSpec((1, tm, half), tile),
            pl.BlockSpec((1, tm, half), lambda b, t: (b, t, ob_block)),
        ] + _mod_specs(l) + [
            _layer_spec(l, 1, D_MODEL),
            pl.BlockSpec((1, D_MODEL), lambda b, t: (0, 0)),
            _layer_spec(l // 2, D_MODEL, D_MODEL),
            _layer_spec(l, D_MODEL, D_FF),
            _layer_spec(l, D_FF, D_MODEL),
        ],
        out_specs=pl.BlockSpec((1, tm, D_MODEL), tile),
        out_shape=jax.ShapeDtypeStruct((bsz, tiles * tm, D_MODEL), F32),
        compiler_params=_params("parallel", "parallel"),
    )(*_stream_args(stream, tm), oa, ob, mods, mods, gains, gain_final, wo, w1, w2)


def _even_in_weight(w):
    n_na = 3 * NA_WIDTH
    scale = np.ones((w.shape[-1],), np.float32)
    scale[:NA_WIDTH] = HEAD_DIM ** -0.5 * LOG2E
    scale[n_na:n_na + GLA_QK_WIDTH] = GLA_DK ** -0.5
    w = w * jnp.asarray(scale)
    pad = n_na + GLA_IN_WIDTH - w.shape[-1]
    return jnp.pad(w, ((0, 0), (0, 0), (0, pad))).astype(BF16)


def _odd_in_weight(w):
    n = w.shape[0]
    n_rot = D_MODEL + SWA_KV_WIDTH
    rot = w[..., :n_rot].reshape(n, D_MODEL, n_rot // HEAD_DIM, HEAD_DIM // 2, 2)
    rot = jnp.swapaxes(rot, 3, 4).reshape(n, D_MODEL, n_rot)
    q = _swa_head_order(rot[..., :D_MODEL] * (HEAD_DIM ** -0.5 * LOG2E), axis=2)
    return jnp.concatenate([q, rot[..., D_MODEL:], w[..., n_rot:]], axis=2).astype(BF16)


def _swa_head_order(a, axis):
    shape = a.shape
    split = shape[:axis] + (SWA_KV_HEADS // 2, 2, SWA_GROUP, HEAD_DIM) + shape[axis + 1:]
    return jnp.swapaxes(a.reshape(split), axis + 1, axis + 2).reshape(shape)


def kernel(x, c, ctx, c_ctx, ada_w, ada_b, norm_mix, norm_mlp, mlp_w1, mlp_w2, ab_w_in, ab_w_out, na_rpb,
           gla_wa2, gla_ba, gla_gnorm, swa_w_in, swa_w_out, swa_sink, norm_final):
    bsz = x.shape[0]
    assert x.shape == (bsz, SEQ, D_MODEL) and ctx.shape == (bsz, CTX_LEN, D_MODEL) and bsz <= 8

    cvec = jnp.zeros((MOD_ROWS, D_MODEL), F32).at[:bsz].set(c).at[8].set(c_ctx)
    mods = _ada_table(cvec, ada_w, ada_b).reshape(DEPTH, MOD_ROWS, 6, D_MODEL)
    cos_t, sin_t = _rope_tables()
    gain_final = norm_final.reshape(1, D_MODEL)
    g_mix = norm_mix.reshape(DEPTH, 1, D_MODEL)
    g_mlp = norm_mlp.reshape(DEPTH, 1, D_MODEL)
    w_in_even = _even_in_weight(ab_w_in)
    w_in_odd = _odd_in_weight(swa_w_in)
    w_out_even = ab_w_out.astype(BF16)
    w_out_odd = _swa_head_order(swa_w_out, axis=1).astype(BF16)
    w1 = mlp_w1.astype(BF16)
    w2 = mlp_w2.astype(BF16)
    na_bias = _na_bias_table(na_rpb)

    stream = (x, ctx)
    for l in range(DEPTH):
        j = l // 2
        if l % 2 == 0:
            na_in, gla_in = _inproj_even(stream, l, mods, g_mix, w_in_even)
            oa = _na_attention(na_in, na_bias[j])
            ob = _gla(gla_in, gla_wa2[j], gla_ba[j], gla_gnorm[j])
            ob_block = 0
            wo = w_out_even
        else:
            q, qr, kr, v = _inproj_odd(stream, l, mods, g_mix, w_in_odd, cos_t, sin_t)
            oa = ob = _swa_attention(q, qr, kr, v, swa_sink[j] * LOG2E)
            ob_block = 1
            wo = w_out_odd
        xs = _outproj_mlp(stream, oa, ob, ob_block, l, mods, g_mlp, gain_final, wo, w1, w2)
        stream = (xs,)
    return xs
```

```python
import functools

import numpy as np
import jax
import jax.numpy as jnp
from jax import lax
from jax.experimental import pallas as pl
from jax.experimental.pallas import tpu as pltpu

D_MODEL = 1024
SEQ = 2048
DEPTH = 4
GRID_W = 64
GRID_ROWS = SEQ // GRID_W
CTX_LEN = 256
TOK = SEQ + CTX_LEN
HEAD_DIM = 64
EPS = 1e-6
NEG_INF = -1e30
LOG2E = 1.4426950408889634

NA_HEADS = 8
NA_WIN_ROWS = 8
NA_WIN_COLS = 16
NA_WIDTH = NA_HEADS * HEAD_DIM
NA_Q_ROWS = 4
NA_K_ROWS = NA_Q_ROWS + NA_WIN_ROWS - 1
NA_QN = NA_Q_ROWS * GRID_W
NA_KN = NA_K_ROWS * GRID_W
NA_ROW_BLOCKS = GRID_ROWS // NA_Q_ROWS

GLA_HEADS = 4
GLA_DK = 64
GLA_DV = 128
GLA_RANK = 16
GLA_NORMALIZER = 16.0
GLA_CHUNK = 64
GLA_QK_WIDTH = GLA_HEADS * GLA_DK
GLA_V_WIDTH = GLA_HEADS * GLA_DV
GLA_IN_WIDTH = 2 * GLA_QK_WIDTH + 2 * GLA_V_WIDTH + 128
GLA_CTX_CHUNKS = CTX_LEN // GLA_CHUNK
GLA_CHUNKS = TOK // GLA_CHUNK

SWA_HEADS = 16
SWA_KV_HEADS = 4
SWA_GROUP = SWA_HEADS // SWA_KV_HEADS
SWA_WINDOW = 128
SWA_KV_WIDTH = SWA_KV_HEADS * HEAD_DIM
SWA_TQ = 256
SWA_NLOC = SWA_TQ + 2 * SWA_WINDOW

D_FF = 4 * D_MODEL
FF_CHUNK = 1024
ROPE_THETA = 10000.0

TM = 256
TILES = TOK // TM
LAT_TILES = SEQ // TM
TP = 768
TP_FINAL = 1024
MOD_ROWS = 16
VMEM_LIMIT = 56 * 1024 * 1024

F32 = jnp.float32
BF16 = jnp.bfloat16


def _nt(a, b):
    return lax.dot_general(a, b, (((1,), (1,)), ((), ())), preferred_element_type=F32)


def _tn(a, b):
    return lax.dot_general(a, b, (((0,), (0,)), ((), ())), preferred_element_type=F32)


def _mm(a, b):
    return jnp.dot(a, b, preferred_element_type=F32)


def _params(*sem):
    return pltpu.CompilerParams(dimension_semantics=sem, vmem_limit_bytes=VMEM_LIMIT)


def _mod_specs(l):
    return [
        pl.BlockSpec((None, 1, 6, D_MODEL), lambda b, t: (l, b, 0, 0)),
        pl.BlockSpec((None, 1, 6, D_MODEL), lambda b, t: (l, 8, 0, 0)),
    ]


def _mod_vectors(mod_ref, modc_ref, rows, idx, has_ctx):
    mod = mod_ref[0]
    if not has_ctx:
        return [mod[i:i + 1] for i in idx]
    modc = modc_ref[0]
    row = lax.broadcasted_iota(jnp.int32, (rows, 1), 0)
    is_ctx = (pl.program_id(1) == TOK // rows - 1) & (row >= rows - CTX_LEN)
    return [jnp.where(is_ctx, modc[i:i + 1], mod[i:i + 1]) for i in idx]


def _layer_spec(l, *tail):
    return pl.BlockSpec((None,) + tail, lambda b, t: (l,) + (0,) * len(tail), pipeline_mode=pl.Buffered(1))


def _stream_specs(stream, tm):
    if len(stream) == 1:
        return [pl.BlockSpec((1, tm, D_MODEL), lambda b, t: (b, t, 0))]
    n = tm // CTX_LEN
    last = SEQ // CTX_LEN - 1
    pieces = [pl.BlockSpec((1, CTX_LEN, D_MODEL), functools.partial(
        lambda i, b, t: (b, jnp.minimum(n * t + i, last), 0), i)) for i in range(n)]
    return pieces + [pl.BlockSpec((1, CTX_LEN, D_MODEL), lambda b, t: (b, 0, 0))]


def _stream_args(stream, tm):
    return list(stream) if len(stream) == 1 else [stream[0]] * (tm // CTX_LEN) + [stream[1]]


def _stream_tile(refs):
    if len(refs) == 1:
        return refs[0][0]
    n = len(refs) - 1
    tail = jnp.where(pl.program_id(1) == TOK // (n * CTX_LEN) - 1, refs[n][0], refs[n - 1][0])
    return jnp.concatenate([r[0] for r in refs[:n - 1]] + [tail], axis=0)


def _ada_kernel(c_ref, w_ref, b_ref, o_ref):
    s = c_ref[...]
    s = s * jax.nn.sigmoid(s)
    o_ref[0] = _mm(s.astype(BF16), w_ref[0].astype(BF16)) + b_ref[0]


def _ada_table(cvec, ada_w, ada_b):
    nb = 6 * D_MODEL // 1024
    return pl.pallas_call(
        _ada_kernel,
        grid=(DEPTH, nb),
        in_specs=[
            pl.BlockSpec((MOD_ROWS, D_MODEL), lambda l, n: (0, 0)),
            pl.BlockSpec((1, D_MODEL, 1024), lambda l, n: (l, 0, n)),
            pl.BlockSpec((1, 1, 1024), lambda l, n: (l, 0, n)),
        ],
        out_specs=pl.BlockSpec((1, MOD_ROWS, 1024), lambda l, n: (l, 0, n)),
        out_shape=jax.ShapeDtypeStruct((DEPTH, MOD_ROWS, 6 * D_MODEL), F32),
        compiler_params=_params("parallel", "parallel"),
    )(cvec, ada_w, ada_b.reshape(DEPTH, 1, 6 * D_MODEL))


def _norm_modulate(x, gain, shift, scale):
    y = x * lax.rsqrt(jnp.mean(x * x, axis=-1, keepdims=True) + EPS) * gain
    return y * (1.0 + scale) + shift


def _inproj_even_kernel(*refs):
    mod_ref, modc_ref, g_ref, w_ref, na_ref, gla_ref = refs[-6:]
    shift, scale = _mod_vectors(mod_ref, modc_ref, TP, (0, 1), True)
    h = _norm_modulate(_stream_tile(refs[:-6]), g_ref[...], shift, scale).astype(BF16)
    n_na = 3 * NA_WIDTH
    na_ref[0] = _mm(h, w_ref[:, :n_na]).astype(BF16)
    gla_ref[0] = _mm(h, w_ref[:, n_na:])


def _inproj_even(stream, l, mods, gains, w):
    bsz = stream[0].shape[0]
    n_na = 3 * NA_WIDTH
    return pl.pallas_call(
        _inproj_even_kernel,
        grid=(bsz, TOK // TP),
        in_specs=_stream_specs(stream, TP) + _mod_specs(l) + [
            _layer_spec(l, 1, D_MODEL),
            _layer_spec(l // 2, D_MODEL, n_na + GLA_IN_WIDTH),
        ],
        out_specs=[
            pl.BlockSpec((1, TP, n_na), lambda b, t: (b, t, 0)),
            pl.BlockSpec((1, TP, GLA_IN_WIDTH), lambda b, t: (b, t, 0)),
        ],
        out_shape=[
            jax.ShapeDtypeStruct((bsz, TOK, n_na), BF16),
            jax.ShapeDtypeStruct((bsz, TOK, GLA_IN_WIDTH), F32),
        ],
        compiler_params=_params("parallel", "parallel"),
    )(*_stream_args(stream, TP), mods, mods, gains, w)


def _rope(a, cos, sin, first_half):
    swapped = jnp.where(first_half, pltpu.roll(a, 96, 1), pltpu.roll(a, 32, 1))
    return a * cos + swapped * sin


def _inproj_odd_kernel(*refs):
    mod_ref, modc_ref, g_ref, w_ref, cos_ref, sin_ref, q_ref, qr_ref, kr_ref, v_ref = refs[-10:]
    shift, scale = _mod_vectors(mod_ref, modc_ref, TP, (0, 1), True)
    h = _norm_modulate(_stream_tile(refs[:-10]), g_ref[...], shift, scale).astype(BF16)
    cos = cos_ref[...]
    sin = sin_ref[...]
    first_half = (lax.broadcasted_iota(jnp.int32, (TP, 128), 1) % HEAD_DIM) < HEAD_DIM // 2
    wide = 256
    for j in range(D_MODEL // wide):
        a = _mm(h, w_ref[:, j * wide:(j + 1) * wide])
        q_ref[0, :, j * wide:(j + 1) * wide] = a.astype(BF16)
        for t in range(wide // 128):
            c0 = j * wide + t * 128
            qr_ref[0, :, c0:c0 + 128] = _rope(a[:, t * 128:(t + 1) * 128], cos, sin, first_half).astype(BF16)
    a = _mm(h, w_ref[:, D_MODEL:D_MODEL + SWA_KV_WIDTH])
    for t in range(SWA_KV_WIDTH // 128):
        kr_ref[0, :, t * 128:(t + 1) * 128] = _rope(a[:, t * 128:(t + 1) * 128], cos, sin, first_half).astype(BF16)
    v_ref[0] = _mm(h, w_ref[:, D_MODEL + SWA_KV_WIDTH:]).astype(BF16)


def _inproj_odd(stream, l, mods, gains, w, cos_t, sin_t):
    bsz = stream[0].shape[0]
    n_in = D_MODEL + 2 * SWA_KV_WIDTH
    tile = lambda b, t: (b, t, 0)
    return pl.pallas_call(
        _inproj_odd_kernel,
        grid=(bsz, TOK // TP),
        in_specs=_stream_specs(stream, TP) + _mod_specs(l) + [
            _layer_spec(l, 1, D_MODEL),
            _layer_spec(l // 2, D_MODEL, n_in),
            pl.BlockSpec((TP, 128), lambda b, t: (t, 0)),
            pl.BlockSpec((TP, 128), lambda b, t: (t, 0)),
        ],
        out_specs=[
            pl.BlockSpec((1, TP, D_MODEL), tile),
            pl.BlockSpec((1, TP, D_MODEL), tile),
            pl.BlockSpec((1, TP, SWA_KV_WIDTH), tile),
            pl.BlockSpec((1, TP, SWA_KV_WIDTH), tile),
        ],
        out_shape=[
            jax.ShapeDtypeStruct((bsz, TOK, D_MODEL), BF16),
            jax.ShapeDtypeStruct((bsz, TOK, D_MODEL), BF16),
            jax.ShapeDtypeStruct((bsz, TOK, SWA_KV_WIDTH), BF16),
            jax.ShapeDtypeStruct((bsz, TOK, SWA_KV_WIDTH), BF16),
        ],
        compiler_params=_params("parallel", "parallel"),
    )(*_stream_args(stream, TP), mods, mods, gains, w, cos_t, sin_t)


def _rope_tables():
    t = np.arange(SEQ)
    n_freq = HEAD_DIM // 4
    inv = jnp.asarray(ROPE_THETA, F32) ** (-jnp.arange(n_freq, dtype=F32) / n_freq)
    row = jnp.asarray(t // GRID_W, F32)
    col = jnp.asarray(t % GRID_W, F32)
    ang = jnp.concatenate([row[:, None] * inv, col[:, None] * inv], axis=-1)
    cos, sin = jnp.cos(ang), jnp.sin(ang)
    cos_t = jnp.tile(cos, (1, 4))
    sin_t = jnp.tile(jnp.concatenate([-sin, sin], axis=-1), (1, 2))
    cos_t = jnp.concatenate([cos_t, jnp.ones((CTX_LEN, 128), F32)], axis=0)
    sin_t = jnp.concatenate([sin_t, jnp.zeros((CTX_LEN, 128), F32)], axis=0)
    return cos_t, sin_t


NA_ROW_OFFSETS = 2 * NA_WIN_ROWS - 1
NA_BIAS_BLOCKS = NA_ROW_OFFSETS + 1


def _na_col_select():
    kc = np.arange(GRID_W)[:, None]
    c = np.arange(128)[None, :] % GRID_W
    w0 = np.clip(c - NA_WIN_COLS // 2, 0, GRID_W - NA_WIN_COLS)
    ok = (kc >= w0) & (kc < w0 + NA_WIN_COLS)
    offset = kc - c + NA_WIN_COLS - 1
    sel = (offset[None] == np.arange(2 * NA_WIN_COLS - 1)[:, None, None]) & ok[None]
    return sel.astype(np.float32), ok


_NA_COL_SELECT, _NA_COL_OK = _na_col_select()


def _na_bias_table(rpb):
    layers = rpb.shape[0]
    sel = jnp.asarray(_NA_COL_SELECT.reshape(2 * NA_WIN_COLS - 1, GRID_W * 128))
    col = jnp.dot(rpb.astype(F32).reshape(-1, 2 * NA_WIN_COLS - 1), sel, precision=lax.Precision.HIGHEST)
    col = col.reshape(layers * NA_HEADS, NA_ROW_OFFSETS, GRID_W, 128)
    col = jnp.where(jnp.asarray(_NA_COL_OK), col * LOG2E, NEG_INF)
    col = jnp.concatenate([col, jnp.full((layers * NA_HEADS, 1, GRID_W, 128), NEG_INF, F32)], axis=1)
    return col.reshape(layers, NA_HEADS, NA_BIAS_BLOCKS, GRID_W, 128)


def _na_bias(bias_ref, head, rb):
    lane = lax.broadcasted_iota(jnp.int32, (1, 128), 1)
    low = lane < GRID_W
    q_row0 = rb * NA_Q_ROWS
    k_row0 = jnp.clip(q_row0 - NA_WIN_ROWS // 2, 0, GRID_ROWS - NA_K_ROWS)
    key_rows = []
    for j in range(NA_K_ROWS):
        kr = k_row0 + j
        blocks = []
        for i in range(NA_Q_ROWS):
            r = q_row0 + i
            r0 = jnp.clip(r - NA_WIN_ROWS // 2, 0, GRID_ROWS - NA_WIN_ROWS)
            inside = (kr >= r0) & (kr < r0 + NA_WIN_ROWS)
            idx = jnp.where(inside, kr - r + NA_WIN_ROWS - 1, NA_ROW_OFFSETS)
            blocks.append(bias_ref[0, head * NA_BIAS_BLOCKS + idx])
        pieces = [jnp.where(low, blocks[i], blocks[i + 1]) for i in range(0, NA_Q_ROWS, 2)]
        key_rows.append(jnp.concatenate(pieces, axis=1))
    return jnp.concatenate(key_rows, axis=0)


def _lane_pair_attention(blocks, lookahead=4):
    lane = lax.broadcasted_iota(jnp.int32, (1, 128), 1)
    half = (lane < HEAD_DIM, lane >= HEAD_DIM)
    row = lax.broadcasted_iota(jnp.int32, (128, 1), 0)
    row_half = (row < HEAD_DIM, row >= HEAD_DIM)

    def scores(j, h):
        qs, keys, _, biases, _ = blocks[j]
        s = [_nt(k, jnp.where(half[h], q, jnp.zeros_like(q))) for q, k in zip(qs, keys)]
        if biases[h] is not None:
            bias = biases[h]() if callable(biases[h]) else biases[h]
            s = [x if b is None else x + b for x, b in zip(s, bias)]
        return s

    def output(j, h, s):
        _, _, values, _, floor = blocks[j]
        m = jnp.max(s[0], axis=0, keepdims=True)
        for x in s[1:]:
            m = jnp.maximum(m, jnp.max(x, axis=0, keepdims=True))
        if floor[h] is not None:
            m = jnp.maximum(m, floor[h])
        acc = None
        for x, v in zip(s, values):
            p = jnp.exp2((x - m).astype(BF16))
            pv = _tn(jnp.where(half[h], v, jnp.ones_like(v)), p)
            acc = pv if acc is None else acc + pv
        den = acc[HEAD_DIM:HEAD_DIM + 1] if h == 0 else acc[0:1]
        if floor[h] is not None:
            den = den + jnp.exp2(floor[h] - m)
        return acc / den

    heads = [(j, h) for j in range(len(blocks)) for h in range(2)]
    res = {}
    ahead = [scores(*head) for head in heads[:lookahead]]
    for i, (j, h) in enumerate(heads):
        if i + lookahead < len(heads):
            ahead.append(scores(*heads[i + lookahead]))
        res[(j, h)] = output(j, h, ahead.pop(0))
    return [jnp.where(row_half[0], res[(j, 0)], res[(j, 1)]).T for j in range(len(blocks))]


def _na_kernel(q_ref, k_ref, v_ref, bias_ref, o_ref):
    rb = pl.program_id(2)
    n_blocks = q_ref.shape[2] // 128

    @pl.when(rb < NA_ROW_BLOCKS)
    def _latent():
        k_row0 = jnp.clip(rb * NA_Q_ROWS - NA_WIN_ROWS // 2, 0, GRID_ROWS - NA_K_ROWS)
        start = pl.multiple_of(k_row0 * GRID_W, GRID_W)
        blocks = []
        for j in range(n_blocks):
            ln = slice(j * 128, (j + 1) * 128)
            keys = [k_ref[0, pl.ds(start, NA_KN), ln], k_ref[0, SEQ:, ln]]
            values = [v_ref[0, pl.ds(start, NA_KN), ln], v_ref[0, SEQ:, ln]]
            biases = [functools.partial(lambda head: [_na_bias(bias_ref, head, rb), None], 2 * j + h)
                      for h in range(2)]
            q = q_ref[0, :, ln]
            blocks.append(([q, q], keys, values, biases, [None, None]))
        for j, o in enumerate(_lane_pair_attention(blocks)):
            o_ref[0, :, j * 128:(j + 1) * 128] = o.astype(BF16)

    @pl.when(rb == NA_ROW_BLOCKS)
    def _context():
        blocks = []
        for j in range(n_blocks):
            ln = slice(j * 128, (j + 1) * 128)
            blocks.append(([q_ref[0, :, ln]], [k_ref[0, SEQ:, ln]], [v_ref[0, SEQ:, ln]],
                           [None, None], [None, None]))
        for j, o in enumerate(_lane_pair_attention(blocks)):
            o_ref[0, :, j * 128:(j + 1) * 128] = o.astype(BF16)


def _na_attention(qkv, bias):
    bsz = qkv.shape[0]
    hps = 8
    groups = NA_HEADS // hps
    w = hps * HEAD_DIM
    return pl.pallas_call(
        _na_kernel,
        grid=(groups, bsz, NA_ROW_BLOCKS + 1),
        in_specs=[
            pl.BlockSpec((1, NA_QN, w), lambda p, b, r: (b, r, p)),
            pl.BlockSpec((1, TOK, w), lambda p, b, r: (b, 0, groups + p)),
            pl.BlockSpec((1, TOK, w), lambda p, b, r: (b, 0, 2 * groups + p)),
            pl.BlockSpec((1, hps * NA_BIAS_BLOCKS, GRID_W, 128), lambda p, b, r: (p, 0, 0, 0)),
        ],
        out_specs=pl.BlockSpec((1, NA_QN, w), lambda p, b, r: (b, r, p)),
        out_shape=jax.ShapeDtypeStruct((bsz, TOK, NA_WIDTH), BF16),
        compiler_params=_params("parallel", "parallel", "arbitrary"),
    )(qkv, qkv, qkv, bias.reshape(groups, hps * NA_BIAS_BLOCKS, GRID_W, 128))


def _block_rows(i, size):
    start = i * size
    return pl.ds(start if isinstance(start, int) else pl.multiple_of(start, size), size)


def _split3(g):
    hi = g.astype(BF16)
    r1 = g - hi.astype(F32)
    mid = r1.astype(BF16)
    lo = (r1 - mid.astype(F32)).astype(BF16)
    return hi, mid, lo


def _gla_kernel(q_ref, k_ref, v_ref, gate_ref, lr_ref, wa2_ref, ba_ref, gn_ref, o_ref,
                acc_ref, cum_ref, qt_ref, u_ref, dec_ref, sp_ref, st_ref):
    c = GLA_CHUNK
    ii = lax.broadcasted_iota(jnp.int32, (c, c), 0)
    jj = lax.broadcasted_iota(jnp.int32, (c, c), 1)
    incl = (jj <= ii, jj >= ii)
    tri = tuple(jnp.where(m, 1.0, 0.0).astype(BF16) for m in incl)
    head0 = lax.broadcasted_iota(jnp.int32, (1, 128), 1) < GLA_DK
    per_tile = TM // c

    def decays(t, carry):
        rows = _block_rows(t, TM)
        lr = lr_ref[0, rows, :].astype(BF16)
        z2 = _mm(lr, wa2_ref[...]) + ba_ref[...]
        for d in range(2):
            z = z2[:, d * 128:(d + 1) * 128]
            g = (jnp.minimum(z, 0.0) - jnp.log1p(jnp.exp(-jnp.abs(z)))) / GLA_NORMALIZER
            wide = jnp.concatenate([g[i * c:(i + 1) * c] for i in range(per_tile)], axis=1)
            hi, mid, lo = _split3(wide)
            cum = _mm(tri[d], hi) + _mm(tri[d], mid) + _mm(tri[d], lo)
            cum_ref[d, rows, :] = jnp.concatenate([cum[:, i * 128:(i + 1) * 128] for i in range(per_tile)], axis=0)
        return carry

    t2 = lax.broadcasted_iota(jnp.int32, (c, 2 * c), 0)
    j2 = lax.broadcasted_iota(jnp.int32, (c, 2 * c), 1) % c
    incl2 = (j2 <= t2, j2 >= t2)

    def by_head(a):
        zero = jnp.zeros_like(a)
        return jnp.concatenate([jnp.where(head0, a, zero), jnp.where(head0, zero, a)], axis=0)

    def weights(ci):
        rows = _block_rows(ci, c)
        qc = q_ref[0, rows, :]
        kc = k_ref[0, rows, :]
        raw = []
        k_ends = []
        for d in range(2):
            cum = cum_ref[d, rows, :]
            tot = cum[c - 1:c, :] if d == 0 else cum[0:1, :]
            q_t = (qc * jnp.exp(cum)).astype(BF16)
            k_t = (kc * jnp.exp(-cum)).astype(BF16)
            k_ends.append((kc * jnp.exp(tot - cum)).astype(BF16))
            dec_ref[d, ci] = jnp.exp(tot)
            qt_ref[ci, :, d * 128:(d + 1) * 128] = by_head(q_t)
            raw.append(_nt(q_t, by_head(k_t)))
        return raw, k_ends

    def outputs(ci, raw, k_ends):
        rows = _block_rows(ci, c)
        v2 = v_ref[0, rows, :].astype(BF16)
        zero_v = jnp.zeros((c, GLA_DV), BF16)
        v_diag = jnp.concatenate([jnp.concatenate([v2[:, :GLA_DV], zero_v], axis=1),
                                  jnp.concatenate([zero_v, v2[:, GLA_DV:]], axis=1)], axis=0)
        p_sum = jnp.where(incl2[0], raw[0], 0.0) + jnp.where(incl2[1], raw[1], 0.0)
        acc_ref[rows, :] = _mm(p_sum.astype(BF16), v_diag)
        uu = _tn(v2, jnp.concatenate(k_ends, axis=1))
        for d in range(2):
            blk = uu[:, d * 128:(d + 1) * 128]
            u_ref[d, ci] = jnp.where(head0, blk[:GLA_DV], blk[GLA_DV:])

    def intra(t):
        ahead = weights(t * per_tile)
        for i in range(per_tile):
            cur = ahead
            if i + 1 < per_tile:
                ahead = weights(t * per_tile + i + 1)
            outputs(t * per_tile + i, *cur)

    def decays_then_intra(t, carry):
        decays(t + 1, carry)
        intra(t)
        return carry

    decays(0, 0)
    lax.fori_loop(0, TILES - 1, decays_then_intra, 0)
    intra(TILES - 1)

    st_ref[...] = jnp.zeros_like(st_ref)

    def scan(i, carry):
        order = (jnp.where(i < GLA_CTX_CHUNKS, GLA_CHUNKS - GLA_CTX_CHUNKS + i, i - GLA_CTX_CHUNKS),
                 GLA_CHUNKS - 1 - i)
        for d in range(2):
            s = st_ref[d]
            sp_ref[order[d], :, d * 128:(d + 1) * 128] = s.astype(BF16)
            st_ref[d] = dec_ref[d, order[d]] * s + u_ref[d, order[d]]
        return carry

    lax.fori_loop(0, GLA_CHUNKS, scan, 0)

    def inter(t):
        for i in range(per_tile):
            ci = t * per_tile + i
            rows = _block_rows(ci, c)
            o = _nt(qt_ref[ci], sp_ref[ci])
            acc_ref[rows, :] = acc_ref[rows, :] + jnp.concatenate([o[:c], o[c:]], axis=1)

    def finish(t):
        rows = _block_rows(t, TM)
        gate = gate_ref[0, rows, :]
        sw = gate * jax.nn.sigmoid(gate)
        for h in range(2):
            vs = slice(h * GLA_DV, (h + 1) * GLA_DV)
            o = acc_ref[rows, vs]
            o = o * lax.rsqrt(jnp.mean(o * o, axis=-1, keepdims=True) + EPS)
            o_ref[0, rows, vs] = (o * gn_ref[:, vs] * sw[:, vs]).astype(BF16)

    def inter_then_finish(t, carry):
        inter(t)
        finish(t - 1)
        return carry

    inter(0)
    lax.fori_loop(1, TILES, inter_then_finish, 0)
    finish(TILES - 1)


def _gla(gla_in, wa2, ba, gnorm):
    bsz = gla_in.shape[0]
    pairs = GLA_HEADS // 2
    qk_blocks = GLA_QK_WIDTH // 128
    v_blocks = GLA_V_WIDTH // 256
    v0 = 2 * GLA_QK_WIDTH // 256
    lr_block = (2 * GLA_QK_WIDTH + 2 * GLA_V_WIDTH) // 128
    wa2_rows = jnp.zeros((2, 128, GLA_QK_WIDTH), F32)
    for d in range(2):
        wa2_rows = wa2_rows.at[d, d * GLA_RANK:(d + 1) * GLA_RANK].set(wa2[d])
    wa2 = wa2_rows.reshape(2, 128, pairs, 128).transpose(1, 2, 0, 3).reshape(128, pairs * 256).astype(BF16)
    ba = ba.reshape(2, pairs, 128).transpose(1, 0, 2).reshape(1, pairs * 256)
    return pl.pallas_call(
        _gla_kernel,
        grid=(bsz, pairs),
        in_specs=[
            pl.BlockSpec((1, TOK, 128), lambda b, p: (b, 0, p)),
            pl.BlockSpec((1, TOK, 128), lambda b, p: (b, 0, qk_blocks + p)),
            pl.BlockSpec((1, TOK, 256), lambda b, p: (b, 0, v0 + p)),
            pl.BlockSpec((1, TOK, 256), lambda b, p: (b, 0, v0 + v_blocks + p)),
            pl.BlockSpec((1, TOK, 128), lambda b, p: (b, 0, lr_block)),
            pl.BlockSpec((128, 256), lambda b, p: (0, p)),
            pl.BlockSpec((1, 256), lambda b, p: (0, p)),
            pl.BlockSpec((1, 256), lambda b, p: (0, p)),
        ],
        out_specs=pl.BlockSpec((1, TOK, 256), lambda b, p: (b, 0, p)),
        out_shape=jax.ShapeDtypeStruct((bsz, TOK, GLA_V_WIDTH), BF16),
        scratch_shapes=[
            pltpu.VMEM((TOK, 2 * GLA_DV), F32),
            pltpu.VMEM((2, TOK, 128), F32),
            pltpu.VMEM((GLA_CHUNKS, 2 * GLA_CHUNK, 256), BF16),
            pltpu.VMEM((2, GLA_CHUNKS, GLA_DV, 128), F32),
            pltpu.VMEM((2, GLA_CHUNKS, 1, 128), F32),
            pltpu.VMEM((GLA_CHUNKS, GLA_DV, 256), BF16),
            pltpu.VMEM((2, GLA_DV, 128), F32),
        ],
        compiler_params=_params("parallel", "parallel"),
    )(gla_in, gla_in, gla_in, gla_in, gla_in, wa2, ba, gnorm.reshape(1, GLA_V_WIDTH))


def _swa_kernel(sink_ref, q_ref, qr_ref, k_ref, v_ref, o_ref):
    qb = pl.program_id(1)
    pairs = SWA_KV_HEADS // 2

    def sinks(kp, j):
        return [sink_ref[(2 * kp + hk) * SWA_GROUP + j] for hk in range(2)]

    def lanes(kp, j):
        return slice((kp * SWA_GROUP + j) * 128, (kp * SWA_GROUP + j + 1) * 128)

    @pl.when(qb < LAT_TILES)
    def _latent():
        q0 = qb * SWA_TQ
        start = pl.multiple_of(jnp.clip(q0 - SWA_WINDOW, 0, SEQ - SWA_NLOC), SWA_WINDOW)
        kpos = start + lax.broadcasted_iota(jnp.int32, (SWA_NLOC, SWA_TQ), 0)
        qpos = q0 + lax.broadcasted_iota(jnp.int32, (SWA_NLOC, SWA_TQ), 1)
        window = jnp.where(jnp.abs(kpos - qpos) <= SWA_WINDOW, 0.0, NEG_INF)
        blocks = []
        for kp in range(pairs):
            kv = slice(kp * 128, (kp + 1) * 128)
            keys = [k_ref[0, pl.ds(start, SWA_NLOC), kv], k_ref[0, SEQ:, kv]]
            values = [v_ref[0, pl.ds(start, SWA_NLOC), kv], v_ref[0, SEQ:, kv]]
            for j in range(SWA_GROUP):
                ln = lanes(kp, j)
                blocks.append(([qr_ref[0, :, ln], q_ref[0, :, ln]], keys, values,
                               [[window, None], [window, None]], sinks(kp, j)))
        for i, o in enumerate(_lane_pair_attention(blocks)):
            o_ref[0, :, i * 128:(i + 1) * 128] = o.astype(BF16)

    @pl.when(qb == LAT_TILES)
    def _context():
        blocks = []
        for kp in range(pairs):
            kv = slice(kp * 128, (kp + 1) * 128)
            for j in range(SWA_GROUP):
                blocks.append(([q_ref[0, :, lanes(kp, j)]], [k_ref[0, SEQ:, kv]], [v_ref[0, SEQ:, kv]],
                               [None, None], sinks(kp, j)))
        for i, o in enumerate(_lane_pair_attention(blocks)):
            o_ref[0, :, i * 128:(i + 1) * 128] = o.astype(BF16)


def _swa_attention(q, qr, kr, v, sink):
    bsz = q.shape[0]
    tile = lambda b, t: (b, t, 0)
    whole = lambda b, t: (b, 0, 0)
    return pl.pallas_call(
        _swa_kernel,
        grid=(bsz, TILES),
        in_specs=[
            pl.BlockSpec(memory_space=pltpu.SMEM),
            pl.BlockSpec((1, SWA_TQ, D_MODEL), tile),
            pl.BlockSpec((1, SWA_TQ, D_MODEL), tile),
            pl.BlockSpec((1, TOK, SWA_KV_WIDTH), whole),
            pl.BlockSpec((1, TOK, SWA_KV_WIDTH), whole),
        ],
        out_specs=pl.BlockSpec((1, SWA_TQ, D_MODEL), tile),
        out_shape=jax.ShapeDtypeStruct((bsz, TOK, D_MODEL), BF16),
        compiler_params=_params("parallel", "arbitrary"),
    )(sink.astype(F32), q, qr, kr, v)


def _mlp_kernel(*refs, final_norm):
    oa_ref, ob_ref, mod_ref, modc_ref, g_ref, gf_ref, wo_ref, w1_ref, w2_ref, out_ref = refs[-10:]
    half = wo_ref.shape[0] // 2
    y = _mm(oa_ref[0], wo_ref[:half, :]) + _mm(ob_ref[0], wo_ref[half:, :])
    x = _stream_tile(refs[:-10])
    gate_mix, shift, scale, gate_mlp = _mod_vectors(mod_ref, modc_ref, x.shape[0], (2, 3, 4, 5), not final_norm)
    x1 = x + gate_mix * y
    h = _norm_modulate(x1, g_ref[...], shift, scale).astype(BF16)
    acc = jnp.zeros((x.shape[0], D_MODEL), F32)
    for c in range(D_FF // FF_CHUNK):
        t = jnp.maximum(_mm(h, w1_ref[:, c * FF_CHUNK:(c + 1) * FF_CHUNK]), 0.0)
        acc = acc + _mm((t * t).astype(BF16), w2_ref[c * FF_CHUNK:(c + 1) * FF_CHUNK, :])
    x2 = x1 + gate_mlp * acc
    if final_norm:
        x2 = x2 * lax.rsqrt(jnp.mean(x2 * x2, axis=-1, keepdims=True) + EPS) * gf_ref[...]
    out_ref[0] = x2


def _outproj_mlp(stream, oa, ob, ob_block, l, mods, gains, gain_final, wo, w1, w2):
    bsz = stream[0].shape[0]
    half = D_MODEL // 2
    final_norm = l == DEPTH - 1
    tm = TP_FINAL if final_norm else TP
    tiles = (SEQ if final_norm else TOK) // tm
    tile = lambda b, t: (b, t, 0)
    return pl.pallas_call(
        functools.partial(_mlp_kernel, final_norm=final_norm),
        grid=(bsz, tiles),
        in_specs=_stream_specs(stream, tm) + [
            pl.BlockSpec((1, tm, half), tile),
            pl.BlockSpec((1, tm, half), lambda b, t: (b, t, ob_block)),
        ] + _mod_specs(l) + [
            _layer_spec(l, 1, D_MODEL),
            pl.BlockSpec((1, D_MODEL), lambda b, t: (0, 0)),
            _layer_spec(l // 2, D_MODEL, D_MODEL),
            _layer_spec(l, D_MODEL, D_FF),
            _layer_spec(l, D_FF, D_MODEL),
        ],
        out_specs=pl.BlockSpec((1, tm, D_MODEL), tile),
        out_shape=jax.ShapeDtypeStruct((bsz, tiles * tm, D_MODEL), F32),
        compiler_params=_params("parallel", "parallel"),
    )(*_stream_args(stream, tm), oa, ob, mods, mods, gains, gain_final, wo, w1, w2)


def _even_in_weight(w):
    n_na = 3 * NA_WIDTH
    scale = np.ones((w.shape[-1],), np.float32)
    scale[:NA_WIDTH] = HEAD_DIM ** -0.5 * LOG2E
    scale[n_na:n_na + GLA_QK_WIDTH] = GLA_DK ** -0.5
    w = w * jnp.asarray(scale)
    pad = n_na + GLA_IN_WIDTH - w.shape[-1]
    return jnp.pad(w, ((0, 0), (0, 0), (0, pad))).astype(BF16)


def _odd_in_weight(w):
    n = w.shape[0]
    n_rot = D_MODEL + SWA_KV_WIDTH
    rot = w[..., :n_rot].reshape(n, D_MODEL, n_rot // HEAD_DIM, HEAD_DIM // 2, 2)
    rot = jnp.swapaxes(rot, 3, 4).reshape(n, D_MODEL, n_rot)
    q = _swa_head_order(rot[..., :D_MODEL] * (HEAD_DIM ** -0.5 * LOG2E), axis=2)
    return jnp.concatenate([q, rot[..., D_MODEL:], w[..., n_rot:]], axis=2).astype(BF16)


def _swa_head_order(a, axis):
    shape = a.shape
    split = shape[:axis] + (SWA_KV_HEADS // 2, 2, SWA_GROUP, HEAD_DIM) + shape[axis + 1:]
    return jnp.swapaxes(a.reshape(split), axis + 1, axis + 2).reshape(shape)


def kernel(x, c, ctx, c_ctx, ada_w, ada_b, norm_mix, norm_mlp, mlp_w1, mlp_w2, ab_w_in, ab_w_out, na_rpb,
           gla_wa2, gla_ba, gla_gnorm, swa_w_in, swa_w_out, swa_sink, norm_final):
    bsz = x.shape[0]
    assert x.shape == (bsz, SEQ, D_MODEL) and ctx.shape == (bsz, CTX_LEN, D_MODEL) and bsz <= 8

    cvec = jnp.zeros((MOD_ROWS, D_MODEL), F32).at[:bsz].set(c).at[8].set(c_ctx)
    mods = _ada_table(cvec, ada_w, ada_b).reshape(DEPTH, MOD_ROWS, 6, D_MODEL)
    cos_t, sin_t = _rope_tables()
    gain_final = norm_final.reshape(1, D_MODEL)
    g_mix = norm_mix.reshape(DEPTH, 1, D_MODEL)
    g_mlp = norm_mlp.reshape(DEPTH, 1, D_MODEL)
    w_in_even = _even_in_weight(ab_w_in)
    w_in_odd = _odd_in_weight(swa_w_in)
    w_out_even = ab_w_out.astype(BF16)
    w_out_odd = _swa_head_order(swa_w_out, axis=1).astype(BF16)
    w1 = mlp_w1.astype(BF16)
    w2 = mlp_w2.astype(BF16)
    na_bias = _na_bias_table(na_rpb)

    stream = (x, ctx)
    for l in range(DEPTH):
        j = l // 2
        if l % 2 == 0:
            na_in, gla_in = _inproj_even(stream, l, mods, g_mix, w_in_even)
            oa = _na_attention(na_in, na_bias[j])
            ob = _gla(gla_in, gla_wa2[j], gla_ba[j], gla_gnorm[j])
            ob_block = 0
            wo = w_out_even
        else:
            q, qr, kr, v = _inproj_odd(stream, l, mods, g_mix, w_in_odd, cos_t, sin_t)
            oa = ob = _swa_attention(q, qr, kr, v, swa_sink[j] * LOG2E)
            ob_block = 1
            wo = w_out_odd
        xs = _outproj_mlp(stream, oa, ob, ob_block, l, mods, g_mlp, gain_final, wo, w1, w2)
        stream = (xs,)
    return xs
```

```python
import functools

import numpy as np
import jax
import jax.numpy as jnp
from jax import lax
from jax.experimental import pallas as pl
from jax.experimental.pallas import tpu as pltpu

D_MODEL = 1024
SEQ = 2048
DEPTH = 4
GRID_W = 64
GRID_ROWS = SEQ // GRID_W
CTX_LEN = 256
TOK = SEQ + CTX_LEN
HEAD_DIM = 64
EPS = 1e-6
NEG_INF = -1e30
LOG2E = 1.4426950408889634

NA_HEADS = 8
NA_WIN_ROWS = 8
NA_WIN_COLS = 16
NA_WIDTH = NA_HEADS * HEAD_DIM
NA_Q_ROWS = 4
NA_K_ROWS = NA_Q_ROWS + NA_WIN_ROWS - 1
NA_QN = NA_Q_ROWS * GRID_W
NA_KN = NA_K_ROWS * GRID_W
NA_ROW_BLOCKS = GRID_ROWS // NA_Q_ROWS

GLA_HEADS = 4
GLA_DK = 64
GLA_DV = 128
GLA_RANK = 16
GLA_NORMALIZER = 16.0
GLA_CHUNK = 64
GLA_QK_WIDTH = GLA_HEADS * GLA_DK
GLA_V_WIDTH = GLA_HEADS * GLA_DV
GLA_IN_WIDTH = 2 * GLA_QK_WIDTH + 2 * GLA_V_WIDTH + 128
GLA_CTX_CHUNKS = CTX_LEN // GLA_CHUNK
GLA_CHUNKS = TOK // GLA_CHUNK

SWA_HEADS = 16
SWA_KV_HEADS = 4
SWA_GROUP = SWA_HEADS // SWA_KV_HEADS
SWA_WINDOW = 128
SWA_KV_WIDTH = SWA_KV_HEADS * HEAD_DIM
SWA_TQ = 256
SWA_NLOC = SWA_TQ + 2 * SWA_WINDOW

D_FF = 4 * D_MODEL
FF_CHUNK = 1024
ROPE_THETA = 10000.0

TM = 256
TILES = TOK // TM
LAT_TILES = SEQ // TM
TP = 768
TP_FINAL = 1024
MOD_ROWS = 16
VMEM_LIMIT = 56 * 1024 * 1024

F32 = jnp.float32
BF16 = jnp.bfloat16


def _nt(a, b):
    return lax.dot_general(a, b, (((1,), (1,)), ((), ())), preferred_element_type=F32)


def _tn(a, b):
    return lax.dot_general(a, b, (((0,), (0,)), ((), ())), preferred_element_type=F32)


def _mm(a, b):
    return jnp.dot(a, b, preferred_element_type=F32)


def _params(*sem):
    return pltpu.CompilerParams(dimension_semantics=sem, vmem_limit_bytes=VMEM_LIMIT)


def _mod_specs(l):
    return [
        pl.BlockSpec((None, 1, 6, D_MODEL), lambda b, t: (l, b, 0, 0)),
        pl.BlockSpec((None, 1, 6, D_MODEL), lambda b, t: (l, 8, 0, 0)),
    ]


def _mod_vectors(mod_ref, modc_ref, rows, idx, has_ctx):
    mod = mod_ref[0]
    if not has_ctx:
        return [mod[i:i + 1] for i in idx]
    modc = modc_ref[0]
    row = lax.broadcasted_iota(jnp.int32, (rows, 1), 0)
    is_ctx = (pl.program_id(1) == TOK // rows - 1) & (row >= rows - CTX_LEN)
    return [jnp.where(is_ctx, modc[i:i + 1], mod[i:i + 1]) for i in idx]


def _layer_spec(l, *tail):
    return pl.BlockSpec((None,) + tail, lambda b, t: (l,) + (0,) * len(tail), pipeline_mode=pl.Buffered(1))


def _stream_specs(stream, tm):
    if len(stream) == 1:
        return [pl.BlockSpec((1, tm, D_MODEL), lambda b, t: (b, t, 0))]
    n = tm // CTX_LEN
    last = SEQ // CTX_LEN - 1
    pieces = [pl.BlockSpec((1, CTX_LEN, D_MODEL), functools.partial(
        lambda i, b, t: (b, jnp.minimum(n * t + i, last), 0), i)) for i in range(n)]
    return pieces + [pl.BlockSpec((1, CTX_LEN, D_MODEL), lambda b, t: (b, 0, 0))]


def _stream_args(stream, tm):
    return list(stream) if len(stream) == 1 else [stream[0]] * (tm // CTX_LEN) + [stream[1]]


def _stream_tile(refs):
    if len(refs) == 1:
        return refs[0][0]
    n = len(refs) - 1
    tail = jnp.where(pl.program_id(1) == TOK // (n * CTX_LEN) - 1, refs[n][0], refs[n - 1][0])
    return jnp.concatenate([r[0] for r in refs[:n - 1]] + [tail], axis=0)


def _ada_kernel(c_ref, w_ref, b_ref, o_ref):
    s = c_ref[...]
    s = s * jax.nn.sigmoid(s)
    o_ref[0] = _mm(s.astype(BF16), w_ref[0].astype(BF16)) + b_ref[0]


def _ada_table(cvec, ada_w, ada_b):
    nb = 6 * D_MODEL // 1024
    return pl.pallas_call(
        _ada_kernel,
        grid=(DEPTH, nb),
        in_specs=[
            pl.BlockSpec((MOD_ROWS, D_MODEL), lambda l, n: (0, 0)),
            pl.BlockSpec((1, D_MODEL, 1024), lambda l, n: (l, 0, n)),
            pl.BlockSpec((1, 1, 1024), lambda l, n: (l, 0, n)),
        ],
        out_specs=pl.BlockSpec((1, MOD_ROWS, 1024), lambda l, n: (l, 0, n)),
        out_shape=jax.ShapeDtypeStruct((DEPTH, MOD_ROWS, 6 * D_MODEL), F32),
        compiler_params=_params("parallel", "parallel"),
    )(cvec, ada_w, ada_b.reshape(DEPTH, 1, 6 * D_MODEL))


def _norm_modulate(x, gain, shift, scale):
    y = x * lax.rsqrt(jnp.mean(x * x, axis=-1, keepdims=True) + EPS) * gain
    return y * (1.0 + scale) + shift


def _inproj_even_kernel(*refs):
    mod_ref, modc_ref, g_ref, w_ref, na_ref, gla_ref = refs[-6:]
    shift, scale = _mod_vectors(mod_ref, modc_ref, TP, (0, 1), True)
    h = _norm_modulate(_stream_tile(refs[:-6]), g_ref[...], shift, scale).astype(BF16)
    n_na = 3 * NA_WIDTH
    na_ref[0] = _mm(h, w_ref[:, :n_na]).astype(BF16)
    gla_ref[0] = _mm(h, w_ref[:, n_na:])


def _inproj_even(stream, l, mods, gains, w):
    bsz = stream[0].shape[0]
    n_na = 3 * NA_WIDTH
    return pl.pallas_call(
        _inproj_even_kernel,
        grid=(bsz, TOK // TP),
        in_specs=_stream_specs(stream, TP) + _mod_specs(l) + [
            _layer_spec(l, 1, D_MODEL),
            _layer_spec(l // 2, D_MODEL, n_na + GLA_IN_WIDTH),
        ],
        out_specs=[
            pl.BlockSpec((1, TP, n_na), lambda b, t: (b, t, 0)),
            pl.BlockSpec((1, TP, GLA_IN_WIDTH), lambda b, t: (b, t, 0)),
        ],
        out_shape=[
            jax.ShapeDtypeStruct((bsz, TOK, n_na), BF16),
            jax.ShapeDtypeStruct((bsz, TOK, GLA_IN_WIDTH), F32),
        ],
        compiler_params=_params("parallel", "parallel"),
    )(*_stream_args(stream, TP), mods, mods, gains, w)


def _rope(a, cos, sin, first_half):
    swapped = jnp.where(first_half, pltpu.roll(a, 96, 1), pltpu.roll(a, 32, 1))
    return a * cos + swapped * sin


def _inproj_odd_kernel(*refs):
    mod_ref, modc_ref, g_ref, w_ref, cos_ref, sin_ref, q_ref, qr_ref, kr_ref, v_ref = refs[-10:]
    shift, scale = _mod_vectors(mod_ref, modc_ref, TP, (0, 1), True)
    h = _norm_modulate(_stream_tile(refs[:-10]), g_ref[...], shift, scale).astype(BF16)
    cos = cos_ref[...]
    sin = sin_ref[...]
    first_half = (lax.broadcasted_iota(jnp.int32, (TP, 128), 1) % HEAD_DIM) < HEAD_DIM // 2
    wide = 256
    for j in range(D_MODEL // wide):
        a = _mm(h, w_ref[:, j * wide:(j + 1) * wide])
        q_ref[0, :, j * wide:(j + 1) * wide] = a.astype(BF16)
        for t in range(wide // 128):
            c0 = j * wide + t * 128
            qr_ref[0, :, c0:c0 + 128] = _rope(a[:, t * 128:(t + 1) * 128], cos, sin, first_half).astype(BF16)
    a = _mm(h, w_ref[:, D_MODEL:D_MODEL + SWA_KV_WIDTH])
    for t in range(SWA_KV_WIDTH // 128):
        kr_ref[0, :, t * 128:(t + 1) * 128] = _rope(a[:, t * 128:(t + 1) * 128], cos, sin, first_half).astype(BF16)
    v_ref[0] = _mm(h, w_ref[:, D_MODEL + SWA_KV_WIDTH:]).astype(BF16)


def _inproj_odd(stream, l, mods, gains, w, cos_t, sin_t):
    bsz = stream[0].shape[0]
    n_in = D_MODEL + 2 * SWA_KV_WIDTH
    tile = lambda b, t: (b, t, 0)
    return pl.pallas_call(
        _inproj_odd_kernel,
        grid=(bsz, TOK // TP),
        in_specs=_stream_specs(stream, TP) + _mod_specs(l) + [
            _layer_spec(l, 1, D_MODEL),
            _layer_spec(l // 2, D_MODEL, n_in),
            pl.BlockSpec((TP, 128), lambda b, t: (t, 0)),
            pl.BlockSpec((TP, 128), lambda b, t: (t, 0)),
        ],
        out_specs=[
            pl.BlockSpec((1, TP, D_MODEL), tile),
            pl.BlockSpec((1, TP, D_MODEL), tile),
            pl.BlockSpec((1, TP, SWA_KV_WIDTH), tile),
            pl.BlockSpec((1, TP, SWA_KV_WIDTH), tile),
        ],
        out_shape=[
            jax.ShapeDtypeStruct((bsz, TOK, D_MODEL), BF16),
            jax.ShapeDtypeStruct((bsz, TOK, D_MODEL), BF16),
            jax.ShapeDtypeStruct((bsz, TOK, SWA_KV_WIDTH), BF16),
            jax.ShapeDtypeStruct((bsz, TOK, SWA_KV_WIDTH), BF16),
        ],
        compiler_params=_params("parallel", "parallel"),
    )(*_stream_args(stream, TP), mods, mods, gains, w, cos_t, sin_t)


def _rope_tables():
    t = np.arange(SEQ)
    n_freq = HEAD_DIM // 4
    inv = jnp.asarray(ROPE_THETA, F32) ** (-jnp.arange(n_freq, dtype=F32) / n_freq)
    row = jnp.asarray(t // GRID_W, F32)
    col = jnp.asarray(t % GRID_W, F32)
    ang = jnp.concatenate([row[:, None] * inv, col[:, None] * inv], axis=-1)
    cos, sin = jnp.cos(ang), jnp.sin(ang)
    cos_t = jnp.tile(cos, (1, 4))
    sin_t = jnp.tile(jnp.concatenate([-sin, sin], axis=-1), (1, 2))
    cos_t = jnp.concatenate([cos_t, jnp.ones((CTX_LEN, 128), F32)], axis=0)
    sin_t = jnp.concatenate([sin_t, jnp.zeros((CTX_LEN, 128), F32)], axis=0)
    return cos_t, sin_t


NA_ROW_OFFSETS = 2 * NA_WIN_ROWS - 1
NA_BIAS_BLOCKS = NA_ROW_OFFSETS + 1


def _na_col_select():
    kc = np.arange(GRID_W)[:, None]
    c = np.arange(128)[None, :] % GRID_W
    w0 = np.clip(c - NA_WIN_COLS // 2, 0, GRID_W - NA_WIN_COLS)
    ok = (kc >= w0) & (kc < w0 + NA_WIN_COLS)
    offset = kc - c + NA_WIN_COLS - 1
    sel = (offset[None] == np.arange(2 * NA_WIN_COLS - 1)[:, None, None]) & ok[None]
    return sel.astype(np.float32), ok


_NA_COL_SELECT, _NA_COL_OK = _na_col_select()


def _na_bias_table(rpb):
    layers = rpb.shape[0]
    sel = jnp.asarray(_NA_COL_SELECT.reshape(2 * NA_WIN_COLS - 1, GRID_W * 128))
    col = jnp.dot(rpb.astype(F32).reshape(-1, 2 * NA_WIN_COLS - 1), sel, precision=lax.Precision.HIGHEST)
    col = col.reshape(layers * NA_HEADS, NA_ROW_OFFSETS, GRID_W, 128)
    col = jnp.where(jnp.asarray(_NA_COL_OK), col * LOG2E, NEG_INF)
    col = jnp.concatenate([col, jnp.full((layers * NA_HEADS, 1, GRID_W, 128), NEG_INF, F32)], axis=1)
    return col.reshape(layers, NA_HEADS, NA_BIAS_BLOCKS, GRID_W, 128)


def _na_bias(bias_ref, head, rb):
    lane = lax.broadcasted_iota(jnp.int32, (1, 128), 1)
    low = lane < GRID_W
    q_row0 = rb * NA_Q_ROWS
    k_row0 = jnp.clip(q_row0 - NA_WIN_ROWS // 2, 0, GRID_ROWS - NA_K_ROWS)
    key_rows = []
    for j in range(NA_K_ROWS):
        kr = k_row0 + j
        blocks = []
        for i in range(NA_Q_ROWS):
            r = q_row0 + i
            r0 = jnp.clip(r - NA_WIN_ROWS // 2, 0, GRID_ROWS - NA_WIN_ROWS)
            inside = (kr >= r0) & (kr < r0 + NA_WIN_ROWS)
            idx = jnp.where(inside, kr - r + NA_WIN_ROWS - 1, NA_ROW_OFFSETS)
            blocks.append(bias_ref[0, head * NA_BIAS_BLOCKS + idx])
        pieces = [jnp.where(low, blocks[i], blocks[i + 1]) for i in range(0, NA_Q_ROWS, 2)]
        key_rows.append(jnp.concatenate(pieces, axis=1))
    return jnp.concatenate(key_rows, axis=0)


def _lane_pair_attention(blocks, lookahead=4):
    lane = lax.broadcasted_iota(jnp.int32, (1, 128), 1)
    half = (lane < HEAD_DIM, lane >= HEAD_DIM)
    row = lax.broadcasted_iota(jnp.int32, (128, 1), 0)
    row_half = (row < HEAD_DIM, row >= HEAD_DIM)

    augmented = {}

    def with_ones(v, h):
        if (id(v), h) not in augmented:
            augmented[(id(v), h)] = jnp.where(half[h], v, jnp.ones_like(v))
        return augmented[(id(v), h)]

    def scores(j, h):
        qs, keys, _, biases, _ = blocks[j]
        s = [_nt(k, jnp.where(half[h], q, jnp.zeros_like(q))) for q, k in zip(qs, keys)]
        if biases[h] is not None:
            bias = biases[h]() if callable(biases[h]) else biases[h]
            s = [x if b is None else x + b for x, b in zip(s, bias)]
        return s

    def output(j, h, s):
        _, _, values, _, floor = blocks[j]
        m = jnp.max(s[0], axis=0, keepdims=True)
        for x in s[1:]:
            m = jnp.maximum(m, jnp.max(x, axis=0, keepdims=True))
        if floor[h] is not None:
            m = jnp.maximum(m, floor[h])
        acc = None
        for x, v in zip(s, values):
            p = jnp.exp2((x - m).astype(BF16))
            pv = _tn(with_ones(v, h), p)
            acc = pv if acc is None else acc + pv
        den = acc[HEAD_DIM:HEAD_DIM + 1] if h == 0 else acc[0:1]
        if floor[h] is not None:
            den = den + jnp.exp2(floor[h] - m)
        return acc / den

    heads = [(j, h) for j in range(len(blocks)) for h in range(2)]
    res = {}
    ahead = [scores(*head) for head in heads[:lookahead]]
    for i, (j, h) in enumerate(heads):
        if i + lookahead < len(heads):
            ahead.append(scores(*heads[i + lookahead]))
        res[(j, h)] = output(j, h, ahead.pop(0))
    return [jnp.where(row_half[0], res[(j, 0)], res[(j, 1)]).T for j in range(len(blocks))]


def _na_kernel(q_ref, k_ref, v_ref, bias_ref, o_ref, full_ref):
    rb = pl.program_id(2)
    n_blocks = q_ref.shape[2] // 128
    last = NA_ROW_BLOCKS - 1

    @pl.when(rb < NA_ROW_BLOCKS)
    def _latent():
        k_row0 = jnp.clip(rb * NA_Q_ROWS - NA_WIN_ROWS // 2, 0, GRID_ROWS - NA_K_ROWS)
        start = pl.multiple_of(k_row0 * GRID_W, GRID_W)
        cls = jnp.where(rb == 0, 0, jnp.where(rb == last, 2, 1))

        @pl.when((pl.program_id(1) == 0) & ((rb <= 1) | (rb == last)))
        def _assemble():
            for head in range(2 * n_blocks):
                full_ref[head * 3 + cls] = _na_bias(bias_ref, head, rb)

        blocks = []
        for j in range(n_blocks):
            ln = slice(j * 128, (j + 1) * 128)
            keys = [k_ref[0, pl.ds(start, NA_KN), ln], k_ref[0, SEQ:, ln]]
            values = [v_ref[0, pl.ds(start, NA_KN), ln], v_ref[0, SEQ:, ln]]
            biases = [functools.partial(lambda head: [full_ref[head * 3 + cls], None], 2 * j + h)
                      for h in range(2)]
            q = q_ref[0, :, ln]
            blocks.append(([q, q], keys, values, biases, [None, None]))
        for j, o in enumerate(_lane_pair_attention(blocks)):
            o_ref[0, :, j * 128:(j + 1) * 128] = o.astype(BF16)

    @pl.when(rb == NA_ROW_BLOCKS)
    def _context():
        blocks = []
        for j in range(n_blocks):
            ln = slice(j * 128, (j + 1) * 128)
            blocks.append(([q_ref[0, :, ln]], [k_ref[0, SEQ:, ln]], [v_ref[0, SEQ:, ln]],
                           [None, None], [None, None]))
        for j, o in enumerate(_lane_pair_attention(blocks)):
            o_ref[0, :, j * 128:(j + 1) * 128] = o.astype(BF16)


def _na_attention(qkv, bias):
    bsz = qkv.shape[0]
    hps = 8
    groups = NA_HEADS // hps
    w = hps * HEAD_DIM
    return pl.pallas_call(
        _na_kernel,
        grid=(groups, bsz, NA_ROW_BLOCKS + 1),
        in_specs=[
            pl.BlockSpec((1, NA_QN, w), lambda p, b, r: (b, r, p)),
            pl.BlockSpec((1, TOK, w), lambda p, b, r: (b, 0, groups + p)),
            pl.BlockSpec((1, TOK, w), lambda p, b, r: (b, 0, 2 * groups + p)),
            pl.BlockSpec((1, hps * NA_BIAS_BLOCKS, GRID_W, 128), lambda p, b, r: (p, 0, 0, 0)),
        ],
        out_specs=pl.BlockSpec((1, NA_QN, w), lambda p, b, r: (b, r, p)),
        out_shape=jax.ShapeDtypeStruct((bsz, TOK, NA_WIDTH), BF16),
        scratch_shapes=[pltpu.VMEM((3 * hps, NA_KN, NA_QN), F32)],
        compiler_params=_params("arbitrary", "arbitrary", "arbitrary"),
    )(qkv, qkv, qkv, bias.reshape(groups, hps * NA_BIAS_BLOCKS, GRID_W, 128))


def _block_rows(i, size):
    start = i * size
    return pl.ds(start if isinstance(start, int) else pl.multiple_of(start, size), size)


def _split3(g):
    hi = g.astype(BF16)
    r1 = g - hi.astype(F32)
    mid = r1.astype(BF16)
    lo = (r1 - mid.astype(F32)).astype(BF16)
    return hi, mid, lo


def _gla_kernel(q_ref, k_ref, v_ref, gate_ref, lr_ref, wa2_ref, ba_ref, gn_ref, o_ref,
                acc_ref, cum_ref, qt_ref, u_ref, dec_ref, sp_ref, st_ref):
    c = GLA_CHUNK
    ii = lax.broadcasted_iota(jnp.int32, (c, c), 0)
    jj = lax.broadcasted_iota(jnp.int32, (c, c), 1)
    incl = (jj <= ii, jj >= ii)
    tri = tuple(jnp.where(m, 1.0, 0.0).astype(BF16) for m in incl)
    head0 = lax.broadcasted_iota(jnp.int32, (1, 128), 1) < GLA_DK
    per_tile = TM // c

    def decays(t, carry):
        rows = _block_rows(t, TM)
        lr = lr_ref[0, rows, :].astype(BF16)
        z2 = _mm(lr, wa2_ref[...]) + ba_ref[...]
        for d in range(2):
            z = z2[:, d * 128:(d + 1) * 128]
            g = (jnp.minimum(z, 0.0) - jnp.log1p(jnp.exp(-jnp.abs(z)))) / GLA_NORMALIZER
            wide = jnp.concatenate([g[i * c:(i + 1) * c] for i in range(per_tile)], axis=1)
            hi, mid, lo = _split3(wide)
            cum = _mm(tri[d], hi) + _mm(tri[d], mid) + _mm(tri[d], lo)
            cum_ref[d, rows, :] = jnp.concatenate([cum[:, i * 128:(i + 1) * 128] for i in range(per_tile)], axis=0)
        return carry

    t2 = lax.broadcasted_iota(jnp.int32, (c, 2 * c), 0)
    j2 = lax.broadcasted_iota(jnp.int32, (c, 2 * c), 1) % c
    incl2 = (j2 <= t2, j2 >= t2)

    def by_head(a):
        zero = jnp.zeros_like(a)
        return jnp.concatenate([jnp.where(head0, a, zero), jnp.where(head0, zero, a)], axis=0)

    def weights(ci):
        rows = _block_rows(ci, c)
        qc = q_ref[0, rows, :]
        kc = k_ref[0, rows, :]
        raw = []
        k_ends = []
        for d in range(2):
            cum = cum_ref[d, rows, :]
            tot = cum[c - 1:c, :] if d == 0 else cum[0:1, :]
            q_t = (qc * jnp.exp(cum)).astype(BF16)
            k_t = (kc * jnp.exp(-cum)).astype(BF16)
            k_ends.append((kc * jnp.exp(tot - cum)).astype(BF16))
            dec_ref[d, ci] = jnp.exp(tot)
            qt_ref[ci, :, d * 128:(d + 1) * 128] = by_head(q_t)
            raw.append(_nt(q_t, by_head(k_t)))
        return raw, k_ends

    def outputs(ci, raw, k_ends):
        rows = _block_rows(ci, c)
        v2 = v_ref[0, rows, :].astype(BF16)
        zero_v = jnp.zeros((c, GLA_DV), BF16)
        v_diag = jnp.concatenate([jnp.concatenate([v2[:, :GLA_DV], zero_v], axis=1),
                                  jnp.concatenate([zero_v, v2[:, GLA_DV:]], axis=1)], axis=0)
        p_sum = jnp.where(incl2[0], raw[0], 0.0) + jnp.where(incl2[1], raw[1], 0.0)
        acc_ref[rows, :] = _mm(p_sum.astype(BF16), v_diag)
        uu = _tn(v2, jnp.concatenate(k_ends, axis=1))
        for d in range(2):
            blk = uu[:, d * 128:(d + 1) * 128]
            u_ref[d, ci] = jnp.where(head0, blk[:GLA_DV], blk[GLA_DV:])

    def intra(t):
        ahead = weights(t * per_tile)
        for i in range(per_tile):
            cur = ahead
            if i + 1 < per_tile:
                ahead = weights(t * per_tile + i + 1)
            outputs(t * per_tile + i, *cur)

    def decays_then_intra(t, carry):
        decays(t + 1, carry)
        intra(t)
        return carry

    decays(0, 0)
    lax.fori_loop(0, TILES - 1, decays_then_intra, 0)
    intra(TILES - 1)

    st_ref[...] = jnp.zeros_like(st_ref)

    def scan(i, carry):
        order = (jnp.where(i < GLA_CTX_CHUNKS, GLA_CHUNKS - GLA_CTX_CHUNKS + i, i - GLA_CTX_CHUNKS),
                 GLA_CHUNKS - 1 - i)
        for d in range(2):
            s = st_ref[d]
            sp_ref[order[d], :, d * 128:(d + 1) * 128] = s.astype(BF16)
            st_ref[d] = dec_ref[d, order[d]] * s + u_ref[d, order[d]]
        return carry

    lax.fori_loop(0, GLA_CHUNKS, scan, 0)

    def inter(t):
        for i in range(per_tile):
            ci = t * per_tile + i
            rows = _block_rows(ci, c)
            o = _nt(qt_ref[ci], sp_ref[ci])
            acc_ref[rows, :] = acc_ref[rows, :] + jnp.concatenate([o[:c], o[c:]], axis=1)

    def finish(t):
        rows = _block_rows(t, TM)
        gate = gate_ref[0, rows, :]
        sw = gate * jax.nn.sigmoid(gate)
        for h in range(2):
            vs = slice(h * GLA_DV, (h + 1) * GLA_DV)
            o = acc_ref[rows, vs]
            o = o * lax.rsqrt(jnp.mean(o * o, axis=-1, keepdims=True) + EPS)
            o_ref[0, rows, vs] = (o * gn_ref[:, vs] * sw[:, vs]).astype(BF16)

    def inter_then_finish(t, carry):
        inter(t)
        finish(t - 1)
        return carry

    inter(0)
    lax.fori_loop(1, TILES, inter_then_finish, 0)
    finish(TILES - 1)


def _gla(gla_in, wa2, ba, gnorm):
    bsz = gla_in.shape[0]
    pairs = GLA_HEADS // 2
    qk_blocks = GLA_QK_WIDTH // 128
    v_blocks = GLA_V_WIDTH // 256
    v0 = 2 * GLA_QK_WIDTH // 256
    lr_block = (2 * GLA_QK_WIDTH + 2 * GLA_V_WIDTH) // 128
    wa2_rows = jnp.zeros((2, 128, GLA_QK_WIDTH), F32)
    for d in range(2):
        wa2_rows = wa2_rows.at[d, d * GLA_RANK:(d + 1) * GLA_RANK].set(wa2[d])
    wa2 = wa2_rows.reshape(2, 128, pairs, 128).transpose(1, 2, 0, 3).reshape(128, pairs * 256).astype(BF16)
    ba = ba.reshape(2, pairs, 128).transpose(1, 0, 2).reshape(1, pairs * 256)
    return pl.pallas_call(
        _gla_kernel,
        grid=(bsz, pairs),
        in_specs=[
            pl.BlockSpec((1, TOK, 128), lambda b, p: (b, 0, p)),
            pl.BlockSpec((1, TOK, 128), lambda b, p: (b, 0, qk_blocks + p)),
            pl.BlockSpec((1, TOK, 256), lambda b, p: (b, 0, v0 + p)),
            pl.BlockSpec((1, TOK, 256), lambda b, p: (b, 0, v0 + v_blocks + p)),
            pl.BlockSpec((1, TOK, 128), lambda b, p: (b, 0, lr_block)),
            pl.BlockSpec((128, 256), lambda b, p: (0, p)),
            pl.BlockSpec((1, 256), lambda b, p: (0, p)),
            pl.BlockSpec((1, 256), lambda b, p: (0, p)),
        ],
        out_specs=pl.BlockSpec((1, TOK, 256), lambda b, p: (b, 0, p)),
        out_shape=jax.ShapeDtypeStruct((bsz, TOK, GLA_V_WIDTH), BF16),
        scratch_shapes=[
            pltpu.VMEM((TOK, 2 * GLA_DV), F32),
            pltpu.VMEM((2, TOK, 128), F32),
            pltpu.VMEM((GLA_CHUNKS, 2 * GLA_CHUNK, 256), BF16),
            pltpu.VMEM((2, GLA_CHUNKS, GLA_DV, 128), F32),
            pltpu.VMEM((2, GLA_CHUNKS, 1, 128), F32),
            pltpu.VMEM((GLA_CHUNKS, GLA_DV, 256), BF16),
            pltpu.VMEM((2, GLA_DV, 128), F32),
        ],
        compiler_params=_params("parallel", "parallel"),
    )(gla_in, gla_in, gla_in, gla_in, gla_in, wa2, ba, gnorm.reshape(1, GLA_V_WIDTH))


def _swa_kernel(sink_ref, q_ref, qr_ref, k_ref, v_ref, o_ref):
    qb = pl.program_id(1)
    pairs = SWA_KV_HEADS // 2

    def sinks(kp, j):
        return [sink_ref[(2 * kp + hk) * SWA_GROUP + j] for hk in range(2)]

    def lanes(kp, j):
        return slice((kp * SWA_GROUP + j) * 128, (kp * SWA_GROUP + j + 1) * 128)

    @pl.when(qb < LAT_TILES)
    def _latent():
        q0 = qb * SWA_TQ
        start = pl.multiple_of(jnp.clip(q0 - SWA_WINDOW, 0, SEQ - SWA_NLOC), SWA_WINDOW)
        kpos = start + lax.broadcasted_iota(jnp.int32, (SWA_NLOC, SWA_TQ), 0)
        qpos = q0 + lax.broadcasted_iota(jnp.int32, (SWA_NLOC, SWA_TQ), 1)
        window = jnp.where(jnp.abs(kpos - qpos) <= SWA_WINDOW, 0.0, NEG_INF)
        blocks = []
        for kp in range(pairs):
            kv = slice(kp * 128, (kp + 1) * 128)
            keys = [k_ref[0, pl.ds(start, SWA_NLOC), kv], k_ref[0, SEQ:, kv]]
            values = [v_ref[0, pl.ds(start, SWA_NLOC), kv], v_ref[0, SEQ:, kv]]
            for j in range(SWA_GROUP):
                ln = lanes(kp, j)
                blocks.append(([qr_ref[0, :, ln], q_ref[0, :, ln]], keys, values,
                               [[window, None], [window, None]], sinks(kp, j)))
        for i, o in enumerate(_lane_pair_attention(blocks)):
            o_ref[0, :, i * 128:(i + 1) * 128] = o.astype(BF16)

    @pl.when(qb == LAT_TILES)
    def _context():
        blocks = []
        for kp in range(pairs):
            kv = slice(kp * 128, (kp + 1) * 128)
            for j in range(SWA_GROUP):
                blocks.append(([q_ref[0, :, lanes(kp, j)]], [k_ref[0, SEQ:, kv]], [v_ref[0, SEQ:, kv]],
                               [None, None], sinks(kp, j)))
        for i, o in enumerate(_lane_pair_attention(blocks)):
            o_ref[0, :, i * 128:(i + 1) * 128] = o.astype(BF16)


def _swa_attention(q, qr, kr, v, sink):
    bsz = q.shape[0]
    tile = lambda b, t: (b, t, 0)
    whole = lambda b, t: (b, 0, 0)
    return pl.pallas_call(
        _swa_kernel,
        grid=(bsz, TILES),
        in_specs=[
            pl.BlockSpec(memory_space=pltpu.SMEM),
            pl.BlockSpec((1, SWA_TQ, D_MODEL), tile),
            pl.BlockSpec((1, SWA_TQ, D_MODEL), tile),
            pl.BlockSpec((1, TOK, SWA_KV_WIDTH), whole),
            pl.BlockSpec((1, TOK, SWA_KV_WIDTH), whole),
        ],
        out_specs=pl.BlockSpec((1, SWA_TQ, D_MODEL), tile),
        out_shape=jax.ShapeDtypeStruct((bsz, TOK, D_MODEL), BF16),
        compiler_params=_params("parallel", "arbitrary"),
    )(sink.astype(F32), q, qr, kr, v)


def _mlp_kernel(*refs, final_norm):
    oa_ref, ob_ref, mod_ref, modc_ref, g_ref, gf_ref, wo_ref, w1_ref, w2_ref, out_ref = refs[-10:]
    half = wo_ref.shape[0] // 2
    y = _mm(oa_ref[0], wo_ref[:half, :]) + _mm(ob_ref[0], wo_ref[half:, :])
    x = _stream_tile(refs[:-10])
    gate_mix, shift, scale, gate_mlp = _mod_vectors(mod_ref, modc_ref, x.shape[0], (2, 3, 4, 5), not final_norm)
    x1 = x + gate_mix * y
    h = _norm_modulate(x1, g_ref[...], shift, scale).astype(BF16)
    acc = jnp.zeros((x.shape[0], D_MODEL), F32)
    for c in range(D_FF // FF_CHUNK):
        t = jnp.maximum(_mm(h, w1_ref[:, c * FF_CHUNK:(c + 1) * FF_CHUNK]), 0.0)
        acc = acc + _mm((t * t).astype(BF16), w2_ref[c * FF_CHUNK:(c + 1) * FF_CHUNK, :])
    x2 = x1 + gate_mlp * acc
    if final_norm:
        x2 = x2 * lax.rsqrt(jnp.mean(x2 * x2, axis=-1, keepdims=True) + EPS) * gf_ref[...]
    out_ref[0] = x2


def _outproj_mlp(stream, oa, ob, ob_block, l, mods, gains, gain_final, wo, w1, w2):
    bsz = stream[0].shape[0]
    half = D_MODEL // 2
    final_norm = l == DEPTH - 1
    tm = TP_FINAL if final_norm else TP
    tiles = (SEQ if final_norm else TOK) // tm
    tile = lambda b, t: (b, t, 0)
    return pl.pallas_call(
        functools.partial(_mlp_kernel, final_norm=final_norm),
        grid=(bsz, tiles),
        in_specs=_stream_specs(stream, tm) + [
            pl.BlockSpec((1, tm, half), tile),
            pl.BlockSpec((1, tm, half), lambda b, t: (b, t, ob_block)),
        ] + _mod_specs(l) + [
            _layer_spec(l, 1, D_MODEL),
            pl.BlockSpec((1, D_MODEL), lambda b, t: (0, 0)),
            _layer_spec(l // 2, D_MODEL, D_MODEL),
            _layer_spec(l, D_MODEL, D_FF),
            _layer_spec(l, D_FF, D_MODEL),
        ],
        out_specs=pl.BlockSpec((1, tm, D_MODEL), tile),
        out_shape=jax.ShapeDtypeStruct((bsz, tiles * tm, D_MODEL), F32),
        compiler_params=_params("parallel", "parallel"),
    )(*_stream_args(stream, tm), oa, ob, mods, mods, gains, gain_final, wo, w1, w2)


def _even_in_weight(w):
    n_na = 3 * NA_WIDTH
    scale = np.ones((w.shape[-1],), np.float32)
    scale[:NA_WIDTH] = HEAD_DIM ** -0.5 * LOG2E
    scale[n_na:n_na + GLA_QK_WIDTH] = GLA_DK ** -0.5
    w = w * jnp.asarray(scale)
    pad = n_na + GLA_IN_WIDTH - w.shape[-1]
    return jnp.pad(w, ((0, 0), (0, 0), (0, pad))).astype(BF16)


def _odd_in_weight(w):
    n = w.shape[0]
    n_rot = D_MODEL + SWA_KV_WIDTH
    rot = w[..., :n_rot].reshape(n, D_MODEL, n_rot // HEAD_DIM, HEAD_DIM // 2, 2)
    rot = jnp.swapaxes(rot, 3, 4).reshape(n, D_MODEL, n_rot)
    q = _swa_head_order(rot[..., :D_MODEL] * (HEAD_DIM ** -0.5 * LOG2E), axis=2)
    return jnp.concatenate([q, rot[..., D_MODEL:], w[..., n_rot:]], axis=2).astype(BF16)


def _swa_head_order(a, axis):
    shape = a.shape
    split = shape[:axis] + (SWA_KV_HEADS // 2, 2, SWA_GROUP, HEAD_DIM) + shape[axis + 1:]
    return jnp.swapaxes(a.reshape(split), axis + 1, axis + 2).reshape(shape)


def kernel(x, c, ctx, c_ctx, ada_w, ada_b, norm_mix, norm_mlp, mlp_w1, mlp_w2, ab_w_in, ab_w_out, na_rpb,
           gla_wa2, gla_ba, gla_gnorm, swa_w_in, swa_w_out, swa_sink, norm_final):
    bsz = x.shape[0]
    assert x.shape == (bsz, SEQ, D_MODEL) and ctx.shape == (bsz, CTX_LEN, D_MODEL) and bsz <= 8

    cvec = jnp.zeros((MOD_ROWS, D_MODEL), F32).at[:bsz].set(c).at[8].set(c_ctx)
    mods = _ada_table(cvec, ada_w, ada_b).reshape(DEPTH, MOD_ROWS, 6, D_MODEL)
    cos_t, sin_t = _rope_tables()
    gain_final = norm_final.reshape(1, D_MODEL)
    g_mix = norm_mix.reshape(DEPTH, 1, D_MODEL)
    g_mlp = norm_mlp.reshape(DEPTH, 1, D_MODEL)
    w_in_even = _even_in_weight(ab_w_in)
    w_in_odd = _odd_in_weight(swa_w_in)
    w_out_even = ab_w_out.astype(BF16)
    w_out_odd = _swa_head_order(swa_w_out, axis=1).astype(BF16)
    w1 = mlp_w1.astype(BF16)
    w2 = mlp_w2.astype(BF16)
    na_bias = _na_bias_table(na_rpb)

    stream = (x, ctx)
    for l in range(DEPTH):
        j = l // 2
        if l % 2 == 0:
            na_in, gla_in = _inproj_even(stream, l, mods, g_mix, w_in_even)
            oa = _na_attention(na_in, na_bias[j])
            ob = _gla(gla_in, gla_wa2[j], gla_ba[j], gla_gnorm[j])
            ob_block = 0
            wo = w_out_even
        else:
            q, qr, kr, v = _inproj_odd(stream, l, mods, g_mix, w_in_odd, cos_t, sin_t)
            oa = ob = _swa_attention(q, qr, kr, v, swa_sink[j] * LOG2E)
            ob_block = 1
            wo = w_out_odd
        xs = _outproj_mlp(stream, oa, ob, ob_block, l, mods, g_mlp, gain_final, wo, w1, w2)
        stream = (xs,)
    return xs
```

```python
import functools

import numpy as np
import jax
import jax.numpy as jnp
from jax import lax
from jax.experimental import pallas as pl
from jax.experimental.pallas import tpu as pltpu

D_MODEL = 1024
SEQ = 2048
DEPTH = 4
GRID_W = 64
GRID_ROWS = SEQ // GRID_W
CTX_LEN = 256
TOK = SEQ + CTX_LEN
HEAD_DIM = 64
EPS = 1e-6
NEG_INF = -1e30
LOG2E = 1.4426950408889634

NA_HEADS = 8
NA_WIN_ROWS = 8
NA_WIN_COLS = 16
NA_WIDTH = NA_HEADS * HEAD_DIM
NA_Q_ROWS = 4
NA_K_ROWS = NA_Q_ROWS + NA_WIN_ROWS - 1
NA_QN = NA_Q_ROWS * GRID_W
NA_KN = NA_K_ROWS * GRID_W
NA_ROW_BLOCKS = GRID_ROWS // NA_Q_ROWS

GLA_HEADS = 4
GLA_DK = 64
GLA_DV = 128
GLA_RANK = 16
GLA_NORMALIZER = 16.0
GLA_CHUNK = 64
GLA_QK_WIDTH = GLA_HEADS * GLA_DK
GLA_V_WIDTH = GLA_HEADS * GLA_DV
GLA_IN_WIDTH = 2 * GLA_QK_WIDTH + 2 * GLA_V_WIDTH + 128
GLA_CTX_CHUNKS = CTX_LEN // GLA_CHUNK
GLA_CHUNKS = TOK // GLA_CHUNK

SWA_HEADS = 16
SWA_KV_HEADS = 4
SWA_GROUP = SWA_HEADS // SWA_KV_HEADS
SWA_WINDOW = 128
SWA_KV_WIDTH = SWA_KV_HEADS * HEAD_DIM
SWA_TQ = 256
SWA_NLOC = SWA_TQ + 2 * SWA_WINDOW

D_FF = 4 * D_MODEL
FF_CHUNK = 1024
ROPE_THETA = 10000.0

TM = 256
TILES = TOK // TM
LAT_TILES = SEQ // TM
TP = 768
TP_FINAL = 1024
_ROW_HALVES = (slice(0, TP // 2), slice(TP // 2, TP))
MOD_ROWS = 16
VMEM_LIMIT = 56 * 1024 * 1024

F32 = jnp.float32
BF16 = jnp.bfloat16


def _nt(a, b):
    return lax.dot_general(a, b, (((1,), (1,)), ((), ())), preferred_element_type=F32)


def _tn(a, b):
    return lax.dot_general(a, b, (((0,), (0,)), ((), ())), preferred_element_type=F32)


def _mm(a, b):
    return jnp.dot(a, b, preferred_element_type=F32)


def _params(*sem):
    return pltpu.CompilerParams(dimension_semantics=sem, vmem_limit_bytes=VMEM_LIMIT)


def _mod_specs(l):
    return [
        pl.BlockSpec((None, 1, 6, D_MODEL), lambda b, t: (l, b, 0, 0)),
        pl.BlockSpec((None, 1, 6, D_MODEL), lambda b, t: (l, 8, 0, 0)),
    ]


def _mod_vectors(mod_ref, modc_ref, rows, idx, has_ctx):
    mod = mod_ref[0]
    if not has_ctx:
        return [mod[i:i + 1] for i in idx]
    modc = modc_ref[0]
    row = lax.broadcasted_iota(jnp.int32, (rows, 1), 0)
    is_ctx = (pl.program_id(1) == TOK // rows - 1) & (row >= rows - CTX_LEN)
    return [jnp.where(is_ctx, modc[i:i + 1], mod[i:i + 1]) for i in idx]


def _layer_spec(l, *tail):
    return pl.BlockSpec((None,) + tail, lambda b, t: (l,) + (0,) * len(tail), pipeline_mode=pl.Buffered(1))


def _stream_specs(stream, tm):
    if len(stream) == 1:
        return [pl.BlockSpec((1, tm, D_MODEL), lambda b, t: (b, t, 0))]
    n = tm // CTX_LEN
    last = SEQ // CTX_LEN - 1
    pieces = [pl.BlockSpec((1, CTX_LEN, D_MODEL), functools.partial(
        lambda i, b, t: (b, jnp.minimum(n * t + i, last), 0), i)) for i in range(n)]
    return pieces + [pl.BlockSpec((1, CTX_LEN, D_MODEL), lambda b, t: (b, 0, 0))]


def _stream_args(stream, tm):
    return list(stream) if len(stream) == 1 else [stream[0]] * (tm // CTX_LEN) + [stream[1]]


def _stream_tile(refs):
    if len(refs) == 1:
        return refs[0][0]
    n = len(refs) - 1
    tail = jnp.where(pl.program_id(1) == TOK // (n * CTX_LEN) - 1, refs[n][0], refs[n - 1][0])
    return jnp.concatenate([r[0] for r in refs[:n - 1]] + [tail], axis=0)


def _ada_kernel(c_ref, w_ref, b_ref, o_ref):
    s = c_ref[...]
    s = s * jax.nn.sigmoid(s)
    o_ref[0] = _mm(s.astype(BF16), w_ref[0].astype(BF16)) + b_ref[0]


def _ada_table(cvec, ada_w, ada_b):
    nb = 6 * D_MODEL // 1024
    return pl.pallas_call(
        _ada_kernel,
        grid=(DEPTH, nb),
        in_specs=[
            pl.BlockSpec((MOD_ROWS, D_MODEL), lambda l, n: (0, 0)),
            pl.BlockSpec((1, D_MODEL, 1024), lambda l, n: (l, 0, n)),
            pl.BlockSpec((1, 1, 1024), lambda l, n: (l, 0, n)),
        ],
        out_specs=pl.BlockSpec((1, MOD_ROWS, 1024), lambda l, n: (l, 0, n)),
        out_shape=jax.ShapeDtypeStruct((DEPTH, MOD_ROWS, 6 * D_MODEL), F32),
        compiler_params=_params("parallel", "parallel"),
    )(cvec, ada_w, ada_b.reshape(DEPTH, 1, 6 * D_MODEL))


def _norm_modulate(x, gain, shift, scale):
    y = x * lax.rsqrt(jnp.mean(x * x, axis=-1, keepdims=True) + EPS) * gain
    return y * (1.0 + scale) + shift


def _inproj_even_kernel(*refs):
    mod_ref, modc_ref, g_ref, w_ref, na_ref, gla_ref = refs[-6:]
    shift, scale = _mod_vectors(mod_ref, modc_ref, TP, (0, 1), True)
    x = _stream_tile(refs[:-6])
    n_na = 3 * NA_WIDTH
    hs = [_norm_modulate(x[r], g_ref[...], shift[r], scale[r]).astype(BF16) for r in _ROW_HALVES]
    for r, h in zip(_ROW_HALVES, hs):
        na_ref[0, r, :] = _mm(h, w_ref[:, :n_na]).astype(BF16)
        gla_ref[0, r, :] = _mm(h, w_ref[:, n_na:])


def _inproj_even(stream, l, mods, gains, w):
    bsz = stream[0].shape[0]
    n_na = 3 * NA_WIDTH
    return pl.pallas_call(
        _inproj_even_kernel,
        grid=(bsz, TOK // TP),
        in_specs=_stream_specs(stream, TP) + _mod_specs(l) + [
            _layer_spec(l, 1, D_MODEL),
            _layer_spec(l // 2, D_MODEL, n_na + GLA_IN_WIDTH),
        ],
        out_specs=[
            pl.BlockSpec((1, TP, n_na), lambda b, t: (b, t, 0)),
            pl.BlockSpec((1, TP, GLA_IN_WIDTH), lambda b, t: (b, t, 0)),
        ],
        out_shape=[
            jax.ShapeDtypeStruct((bsz, TOK, n_na), BF16),
            jax.ShapeDtypeStruct((bsz, TOK, GLA_IN_WIDTH), F32),
        ],
        compiler_params=_params("parallel", "parallel"),
    )(*_stream_args(stream, TP), mods, mods, gains, w)


def _rope(a, cos, sin, first_half):
    swapped = jnp.where(first_half, pltpu.roll(a, 96, 1), pltpu.roll(a, 32, 1))
    return a * cos + swapped * sin


def _inproj_odd_kernel(*refs):
    mod_ref, modc_ref, g_ref, w_ref, cos_ref, sin_ref, q_ref, qr_ref, kr_ref, v_ref = refs[-10:]
    shift, scale = _mod_vectors(mod_ref, modc_ref, TP, (0, 1), True)
    x = _stream_tile(refs[:-10])
    first_half = (lax.broadcasted_iota(jnp.int32, (TP // 2, 128), 1) % HEAD_DIM) < HEAD_DIM // 2
    wide = 256
    hs = [_norm_modulate(x[r], g_ref[...], shift[r], scale[r]).astype(BF16) for r in _ROW_HALVES]
    for r, h in zip(_ROW_HALVES, hs):
        cos = cos_ref[r, :]
        sin = sin_ref[r, :]
        for j in range(D_MODEL // wide):
            a = _mm(h, w_ref[:, j * wide:(j + 1) * wide])
            q_ref[0, r, j * wide:(j + 1) * wide] = a.astype(BF16)
            for t in range(wide // 128):
                c0 = j * wide + t * 128
                qr_ref[0, r, c0:c0 + 128] = _rope(a[:, t * 128:(t + 1) * 128], cos, sin, first_half).astype(BF16)
        a = _mm(h, w_ref[:, D_MODEL:D_MODEL + SWA_KV_WIDTH])
        for t in range(SWA_KV_WIDTH // 128):
            kr_ref[0, r, t * 128:(t + 1) * 128] = _rope(a[:, t * 128:(t + 1) * 128], cos, sin,
                                                        first_half).astype(BF16)
        v_ref[0, r, :] = _mm(h, w_ref[:, D_MODEL + SWA_KV_WIDTH:]).astype(BF16)


def _inproj_odd(stream, l, mods, gains, w, cos_t, sin_t):
    bsz = stream[0].shape[0]
    n_in = D_MODEL + 2 * SWA_KV_WIDTH
    tile = lambda b, t: (b, t, 0)
    return pl.pallas_call(
        _inproj_odd_kernel,
        grid=(bsz, TOK // TP),
        in_specs=_stream_specs(stream, TP) + _mod_specs(l) + [
            _layer_spec(l, 1, D_MODEL),
            _layer_spec(l // 2, D_MODEL, n_in),
            pl.BlockSpec((TP, 128), lambda b, t: (t, 0)),
            pl.BlockSpec((TP, 128), lambda b, t: (t, 0)),
        ],
        out_specs=[
            pl.BlockSpec((1, TP, D_MODEL), tile),
            pl.BlockSpec((1, TP, D_MODEL), tile),
            pl.BlockSpec((1, TP, SWA_KV_WIDTH), tile),
            pl.BlockSpec((1, TP, SWA_KV_WIDTH), tile),
        ],
        out_shape=[
            jax.ShapeDtypeStruct((bsz, TOK, D_MODEL), BF16),
            jax.ShapeDtypeStruct((bsz, TOK, D_MODEL), BF16),
            jax.ShapeDtypeStruct((bsz, TOK, SWA_KV_WIDTH), BF16),
            jax.ShapeDtypeStruct((bsz, TOK, SWA_KV_WIDTH), BF16),
        ],
        compiler_params=_params("parallel", "parallel"),
    )(*_stream_args(stream, TP), mods, mods, gains, w, cos_t, sin_t)


def _rope_tables():
    t = np.arange(SEQ)
    n_freq = HEAD_DIM // 4
    inv = jnp.asarray(ROPE_THETA, F32) ** (-jnp.arange(n_freq, dtype=F32) / n_freq)
    row = jnp.asarray(t // GRID_W, F32)
    col = jnp.asarray(t % GRID_W, F32)
    ang = jnp.concatenate([row[:, None] * inv, col[:, None] * inv], axis=-1)
    cos, sin = jnp.cos(ang), jnp.sin(ang)
    cos_t = jnp.tile(cos, (1, 4))
    sin_t = jnp.tile(jnp.concatenate([-sin, sin], axis=-1), (1, 2))
    cos_t = jnp.concatenate([cos_t, jnp.ones((CTX_LEN, 128), F32)], axis=0)
    sin_t = jnp.concatenate([sin_t, jnp.zeros((CTX_LEN, 128), F32)], axis=0)
    return cos_t, sin_t


NA_ROW_OFFSETS = 2 * NA_WIN_ROWS - 1
NA_BIAS_BLOCKS = NA_ROW_OFFSETS + 1


def _na_col_select():
    kc = np.arange(GRID_W)[:, None]
    c = np.arange(128)[None, :] % GRID_W
    w0 = np.clip(c - NA_WIN_COLS // 2, 0, GRID_W - NA_WIN_COLS)
    ok = (kc >= w0) & (kc < w0 + NA_WIN_COLS)
    offset = kc - c + NA_WIN_COLS - 1
    sel = (offset[None] == np.arange(2 * NA_WIN_COLS - 1)[:, None, None]) & ok[None]
    return sel.astype(np.float32), ok


_NA_COL_SELECT, _NA_COL_OK = _na_col_select()


def _na_bias_table(rpb):
    layers = rpb.shape[0]
    sel = jnp.asarray(_NA_COL_SELECT.reshape(2 * NA_WIN_COLS - 1, GRID_W * 128))
    col = jnp.dot(rpb.astype(F32).reshape(-1, 2 * NA_WIN_COLS - 1), sel, precision=lax.Precision.HIGHEST)
    col = col.reshape(layers * NA_HEADS, NA_ROW_OFFSETS, GRID_W, 128)
    col = jnp.where(jnp.asarray(_NA_COL_OK), col * LOG2E, NEG_INF)
    col = jnp.concatenate([col, jnp.full((layers * NA_HEADS, 1, GRID_W, 128), NEG_INF, F32)], axis=1)
    return col.reshape(layers, NA_HEADS, NA_BIAS_BLOCKS, GRID_W, 128)


def _na_bias(bias_ref, head, rb):
    lane = lax.broadcasted_iota(jnp.int32, (1, 128), 1)
    low = lane < GRID_W
    q_row0 = rb * NA_Q_ROWS
    k_row0 = jnp.clip(q_row0 - NA_WIN_ROWS // 2, 0, GRID_ROWS - NA_K_ROWS)
    key_rows = []
    for j in range(NA_K_ROWS):
        kr = k_row0 + j
        blocks = []
        for i in range(NA_Q_ROWS):
            r = q_row0 + i
            r0 = jnp.clip(r - NA_WIN_ROWS // 2, 0, GRID_ROWS - NA_WIN_ROWS)
            inside = (kr >= r0) & (kr < r0 + NA_WIN_ROWS)
            idx = jnp.where(inside, kr - r + NA_WIN_ROWS - 1, NA_ROW_OFFSETS)
            blocks.append(bias_ref[0, head * NA_BIAS_BLOCKS + idx])
        pieces = [jnp.where(low, blocks[i], blocks[i + 1]) for i in range(0, NA_Q_ROWS, 2)]
        key_rows.append(jnp.concatenate(pieces, axis=1))
    return jnp.concatenate(key_rows, axis=0)


def _lane_pair_attention(blocks, lookahead=4):
    lane = lax.broadcasted_iota(jnp.int32, (1, 128), 1)
    half = (lane < HEAD_DIM, lane >= HEAD_DIM)
    row = lax.broadcasted_iota(jnp.int32, (128, 1), 0)
    row_half = (row < HEAD_DIM, row >= HEAD_DIM)

    augmented = {}

    def with_ones(v, h):
        if (id(v), h) not in augmented:
            augmented[(id(v), h)] = jnp.where(half[h], v, jnp.ones_like(v))
        return augmented[(id(v), h)]

    def scores(j, h):
        qs, keys, _, biases, _ = blocks[j]
        s = [_nt(k, jnp.where(half[h], q, jnp.zeros_like(q))) for q, k in zip(qs, keys)]
        if biases[h] is not None:
            bias = biases[h]() if callable(biases[h]) else biases[h]
            s = [x if b is None else x + b for x, b in zip(s, bias)]
        return s

    def output(j, h, s):
        _, _, values, _, floor = blocks[j]
        m = jnp.max(s[0], axis=0, keepdims=True)
        for x in s[1:]:
            m = jnp.maximum(m, jnp.max(x, axis=0, keepdims=True))
        if floor[h] is not None:
            m = jnp.maximum(m, floor[h])
        acc = None
        for x, v in zip(s, values):
            p = jnp.exp2((x - m).astype(BF16))
            pv = _tn(with_ones(v, h), p)
            acc = pv if acc is None else acc + pv
        den = acc[HEAD_DIM:HEAD_DIM + 1] if h == 0 else acc[0:1]
        if floor[h] is not None:
            den = den + jnp.exp2(floor[h] - m)
        return acc / den

    heads = [(j, h) for j in range(len(blocks)) for h in range(2)]
    res = {}
    ahead = [scores(*head) for head in heads[:lookahead]]
    for i, (j, h) in enumerate(heads):
        if i + lookahead < len(heads):
            ahead.append(scores(*heads[i + lookahead]))
        res[(j, h)] = output(j, h, ahead.pop(0))
    return [jnp.where(row_half[0], res[(j, 0)], res[(j, 1)]).T for j in range(len(blocks))]


def _na_kernel(q_ref, k_ref, v_ref, bias_ref, o_ref, full_ref):
    rb = pl.program_id(2)
    n_blocks = q_ref.shape[2] // 128
    last = NA_ROW_BLOCKS - 1

    @pl.when(rb < NA_ROW_BLOCKS)
    def _latent():
        k_row0 = jnp.clip(rb * NA_Q_ROWS - NA_WIN_ROWS // 2, 0, GRID_ROWS - NA_K_ROWS)
        start = pl.multiple_of(k_row0 * GRID_W, GRID_W)
        cls = jnp.where(rb == 0, 0, jnp.where(rb == last, 2, 1))

        @pl.when((pl.program_id(1) == 0) & ((rb <= 1) | (rb == last)))
        def _assemble():
            for head in range(2 * n_blocks):
                full_ref[head * 3 + cls] = _na_bias(bias_ref, head, rb)

        blocks = []
        for j in range(n_blocks):
            ln = slice(j * 128, (j + 1) * 128)
            keys = [k_ref[0, pl.ds(start, NA_KN), ln], k_ref[0, SEQ:, ln]]
            values = [v_ref[0, pl.ds(start, NA_KN), ln], v_ref[0, SEQ:, ln]]
            biases = [functools.partial(lambda head: [full_ref[head * 3 + cls], None], 2 * j + h)
                      for h in range(2)]
            q = q_ref[0, :, ln]
            blocks.append(([q, q], keys, values, biases, [None, None]))
        for j, o in enumerate(_lane_pair_attention(blocks)):
            o_ref[0, :, j * 128:(j + 1) * 128] = o.astype(BF16)

    @pl.when(rb == NA_ROW_BLOCKS)
    def _context():
        blocks = []
        for j in range(n_blocks):
            ln = slice(j * 128, (j + 1) * 128)
            blocks.append(([q_ref[0, :, ln]], [k_ref[0, SEQ:, ln]], [v_ref[0, SEQ:, ln]],
                           [None, None], [None, None]))
        for j, o in enumerate(_lane_pair_attention(blocks)):
            o_ref[0, :, j * 128:(j + 1) * 128] = o.astype(BF16)


def _na_attention(qkv, bias):
    bsz = qkv.shape[0]
    hps = 8
    groups = NA_HEADS // hps
    w = hps * HEAD_DIM
    return pl.pallas_call(
        _na_kernel,
        grid=(groups, bsz, NA_ROW_BLOCKS + 1),
        in_specs=[
            pl.BlockSpec((1, NA_QN, w), lambda p, b, r: (b, r, p)),
            pl.BlockSpec((1, TOK, w), lambda p, b, r: (b, 0, groups + p)),
            pl.BlockSpec((1, TOK, w), lambda p, b, r: (b, 0, 2 * groups + p)),
            pl.BlockSpec((1, hps * NA_BIAS_BLOCKS, GRID_W, 128), lambda p, b, r: (p, 0, 0, 0)),
        ],
        out_specs=pl.BlockSpec((1, NA_QN, w), lambda p, b, r: (b, r, p)),
        out_shape=jax.ShapeDtypeStruct((bsz, TOK, NA_WIDTH), BF16),
        scratch_shapes=[pltpu.VMEM((3 * hps, NA_KN, NA_QN), F32)],
        compiler_params=_params("arbitrary", "arbitrary", "arbitrary"),
    )(qkv, qkv, qkv, bias.reshape(groups, hps * NA_BIAS_BLOCKS, GRID_W, 128))


def _block_rows(i, size):
    start = i * size
    return pl.ds(start if isinstance(start, int) else pl.multiple_of(start, size), size)


def _split3(g):
    hi = g.astype(BF16)
    r1 = g - hi.astype(F32)
    mid = r1.astype(BF16)
    lo = (r1 - mid.astype(F32)).astype(BF16)
    return hi, mid, lo


def _gla_kernel(q_ref, k_ref, v_ref, gate_ref, lr_ref, wa2_ref, ba_ref, gn_ref, o_ref,
                acc_ref, cum_ref, qt_ref, u_ref, dec_ref, sp_ref, st_ref):
    c = GLA_CHUNK
    ii = lax.broadcasted_iota(jnp.int32, (c, c), 0)
    jj = lax.broadcasted_iota(jnp.int32, (c, c), 1)
    incl = (jj <= ii, jj >= ii)
    tri = tuple(jnp.where(m, 1.0, 0.0).astype(BF16) for m in incl)
    head0 = lax.broadcasted_iota(jnp.int32, (1, 128), 1) < GLA_DK
    per_tile = TM // c

    def decays(t, carry):
        rows = _block_rows(t, TM)
        lr = lr_ref[0, rows, :].astype(BF16)
        z2 = _mm(lr, wa2_ref[...]) + ba_ref[...]
        for d in range(2):
            z = z2[:, d * 128:(d + 1) * 128]
            g = (jnp.minimum(z, 0.0) - jnp.log1p(jnp.exp(-jnp.abs(z)))) / GLA_NORMALIZER
            wide = jnp.concatenate([g[i * c:(i + 1) * c] for i in range(per_tile)], axis=1)
            hi, mid, lo = _split3(wide)
            cum = _mm(tri[d], hi) + _mm(tri[d], mid) + _mm(tri[d], lo)
            cum_ref[d, rows, :] = jnp.concatenate([cum[:, i * 128:(i + 1) * 128] for i in range(per_tile)], axis=0)
        return carry

    t2 = lax.broadcasted_iota(jnp.int32, (c, 2 * c), 0)
    j2 = lax.broadcasted_iota(jnp.int32, (c, 2 * c), 1) % c
    incl2 = (j2 <= t2, j2 >= t2)

    def by_head(a):
        zero = jnp.zeros_like(a)
        return jnp.concatenate([jnp.where(head0, a, zero), jnp.where(head0, zero, a)], axis=0)

    def weights(ci):
        rows = _block_rows(ci, c)
        qc = q_ref[0, rows, :]
        kc = k_ref[0, rows, :]
        raw = []
        k_ends = []
        for d in range(2):
            cum = cum_ref[d, rows, :]
            tot = cum[c - 1:c, :] if d == 0 else cum[0:1, :]
            q_t = (qc * jnp.exp(cum)).astype(BF16)
            k_t = (kc * jnp.exp(-cum)).astype(BF16)
            k_ends.append((kc * jnp.exp(tot - cum)).astype(BF16))
            dec_ref[d, ci] = jnp.exp(tot)
            qt_ref[ci, :, d * 128:(d + 1) * 128] = by_head(q_t)
            raw.append(_nt(q_t, by_head(k_t)))
        return raw, k_ends

    def outputs(ci, raw, k_ends):
        rows = _block_rows(ci, c)
        v2 = v_ref[0, rows, :].astype(BF16)
        zero_v = jnp.zeros((c, GLA_DV), BF16)
        v_diag = jnp.concatenate([jnp.concatenate([v2[:, :GLA_DV], zero_v], axis=1),
                                  jnp.concatenate([zero_v, v2[:, GLA_DV:]], axis=1)], axis=0)
        p_sum = jnp.where(incl2[0], raw[0], 0.0) + jnp.where(incl2[1], raw[1], 0.0)
        acc_ref[rows, :] = _mm(p_sum.astype(BF16), v_diag)
        uu = _tn(v2, jnp.concatenate(k_ends, axis=1))
        for d in range(2):
            blk = uu[:, d * 128:(d + 1) * 128]
            u_ref[d, ci] = jnp.where(head0, blk[:GLA_DV], blk[GLA_DV:])

    def intra(t):
        ahead = weights(t * per_tile)
        for i in range(per_tile):
            cur = ahead
            if i + 1 < per_tile:
                ahead = weights(t * per_tile + i + 1)
            outputs(t * per_tile + i, *cur)

    def decays_then_intra(t, carry):
        decays(t + 1, carry)
        intra(t)
        return carry

    decays(0, 0)
    lax.fori_loop(0, TILES - 1, decays_then_intra, 0)
    intra(TILES - 1)

    st_ref[...] = jnp.zeros_like(st_ref)

    def scan(i, carry):
        order = (jnp.where(i < GLA_CTX_CHUNKS, GLA_CHUNKS - GLA_CTX_CHUNKS + i, i - GLA_CTX_CHUNKS),
                 GLA_CHUNKS - 1 - i)
        for d in range(2):
            s = st_ref[d]
            sp_ref[order[d], :, d * 128:(d + 1) * 128] = s.astype(BF16)
            st_ref[d] = dec_ref[d, order[d]] * s + u_ref[d, order[d]]
        return carry

    lax.fori_loop(0, GLA_CHUNKS, scan, 0)

    def inter(t):
        for i in range(per_tile):
            ci = t * per_tile + i
            rows = _block_rows(ci, c)
            o = _nt(qt_ref[ci], sp_ref[ci])
            acc_ref[rows, :] = acc_ref[rows, :] + jnp.concatenate([o[:c], o[c:]], axis=1)

    def finish(t):
        rows = _block_rows(t, TM)
        gate = gate_ref[0, rows, :]
        sw = gate * jax.nn.sigmoid(gate)
        for h in range(2):
            vs = slice(h * GLA_DV, (h + 1) * GLA_DV)
            o = acc_ref[rows, vs]
            o = o * lax.rsqrt(jnp.mean(o * o, axis=-1, keepdims=True) + EPS)
            o_ref[0, rows, vs] = (o * gn_ref[:, vs] * sw[:, vs]).astype(BF16)

    def inter_then_finish(t, carry):
        inter(t)
        finish(t - 1)
        return carry

    inter(0)
    lax.fori_loop(1, TILES, inter_then_finish, 0)
    finish(TILES - 1)


def _gla(gla_in, wa2, ba, gnorm):
    bsz = gla_in.shape[0]
    pairs = GLA_HEADS // 2
    qk_blocks = GLA_QK_WIDTH // 128
    v_blocks = GLA_V_WIDTH // 256
    v0 = 2 * GLA_QK_WIDTH // 256
    lr_block = (2 * GLA_QK_WIDTH + 2 * GLA_V_WIDTH) // 128
    wa2_rows = jnp.zeros((2, 128, GLA_QK_WIDTH), F32)
    for d in range(2):
        wa2_rows = wa2_rows.at[d, d * GLA_RANK:(d + 1) * GLA_RANK].set(wa2[d])
    wa2 = wa2_rows.reshape(2, 128, pairs, 128).transpose(1, 2, 0, 3).reshape(128, pairs * 256).astype(BF16)
    ba = ba.reshape(2, pairs, 128).transpose(1, 0, 2).reshape(1, pairs * 256)
    return pl.pallas_call(
        _gla_kernel,
        grid=(bsz, pairs),
        in_specs=[
            pl.BlockSpec((1, TOK, 128), lambda b, p: (b, 0, p)),
            pl.BlockSpec((1, TOK, 128), lambda b, p: (b, 0, qk_blocks + p)),
            pl.BlockSpec((1, TOK, 256), lambda b, p: (b, 0, v0 + p)),
            pl.BlockSpec((1, TOK, 256), lambda b, p: (b, 0, v0 + v_blocks + p)),
            pl.BlockSpec((1, TOK, 128), lambda b, p: (b, 0, lr_block)),
            pl.BlockSpec((128, 256), lambda b, p: (0, p)),
            pl.BlockSpec((1, 256), lambda b, p: (0, p)),
            pl.BlockSpec((1, 256), lambda b, p: (0, p)),
        ],
        out_specs=pl.BlockSpec((1, TOK, 256), lambda b, p: (b, 0, p)),
        out_shape=jax.ShapeDtypeStruct((bsz, TOK, GLA_V_WIDTH), BF16),
        scratch_shapes=[
            pltpu.VMEM((TOK, 2 * GLA_DV), F32),
            pltpu.VMEM((2, TOK, 128), F32),
            pltpu.VMEM((GLA_CHUNKS, 2 * GLA_CHUNK, 256), BF16),
            pltpu.VMEM((2, GLA_CHUNKS, GLA_DV, 128), F32),
            pltpu.VMEM((2, GLA_CHUNKS, 1, 128), F32),
            pltpu.VMEM((GLA_CHUNKS, GLA_DV, 256), BF16),
            pltpu.VMEM((2, GLA_DV, 128), F32),
        ],
        compiler_params=_params("parallel", "parallel"),
    )(gla_in, gla_in, gla_in, gla_in, gla_in, wa2, ba, gnorm.reshape(1, GLA_V_WIDTH))


def _swa_kernel(sink_ref, q_ref, qr_ref, k_ref, v_ref, o_ref):
    qb = pl.program_id(1)
    pairs = SWA_KV_HEADS // 2

    def sinks(kp, j):
        return [sink_ref[(2 * kp + hk) * SWA_GROUP + j] for hk in range(2)]

    def lanes(kp, j):
        return slice((kp * SWA_GROUP + j) * 128, (kp * SWA_GROUP + j + 1) * 128)

    @pl.when(qb < LAT_TILES)
    def _latent():
        q0 = qb * SWA_TQ
        start = pl.multiple_of(jnp.clip(q0 - SWA_WINDOW, 0, SEQ - SWA_NLOC), SWA_WINDOW)
        kpos = start + lax.broadcasted_iota(jnp.int32, (SWA_NLOC, SWA_TQ), 0)
        qpos = q0 + lax.broadcasted_iota(jnp.int32, (SWA_NLOC, SWA_TQ), 1)
        window = jnp.where(jnp.abs(kpos - qpos) <= SWA_WINDOW, 0.0, NEG_INF)
        blocks = []
        for kp in range(pairs):
            kv = slice(kp * 128, (kp + 1) * 128)
            keys = [k_ref[0, pl.ds(start, SWA_NLOC), kv], k_ref[0, SEQ:, kv]]
            values = [v_ref[0, pl.ds(start, SWA_NLOC), kv], v_ref[0, SEQ:, kv]]
            for j in range(SWA_GROUP):
                ln = lanes(kp, j)
                blocks.append(([qr_ref[0, :, ln], q_ref[0, :, ln]], keys, values,
                               [[window, None], [window, None]], sinks(kp, j)))
        for i, o in enumerate(_lane_pair_attention(blocks)):
            o_ref[0, :, i * 128:(i + 1) * 128] = o.astype(BF16)

    @pl.when(qb == LAT_TILES)
    def _context():
        blocks = []
        for kp in range(pairs):
            kv = slice(kp * 128, (kp + 1) * 128)
            for j in range(SWA_GROUP):
                blocks.append(([q_ref[0, :, lanes(kp, j)]], [k_ref[0, SEQ:, kv]], [v_ref[0, SEQ:, kv]],
                               [None, None], sinks(kp, j)))
        for i, o in enumerate(_lane_pair_attention(blocks)):
            o_ref[0, :, i * 128:(i + 1) * 128] = o.astype(BF16)


def _swa_attention(q, qr, kr, v, sink):
    bsz = q.shape[0]
    tile = lambda b, t: (b, t, 0)
    whole = lambda b, t: (b, 0, 0)
    return pl.pallas_call(
        _swa_kernel,
        grid=(bsz, TILES),
        in_specs=[
            pl.BlockSpec(memory_space=pltpu.SMEM),
            pl.BlockSpec((1, SWA_TQ, D_MODEL), tile),
            pl.BlockSpec((1, SWA_TQ, D_MODEL), tile),
            pl.BlockSpec((1, TOK, SWA_KV_WIDTH), whole),
            pl.BlockSpec((1, TOK, SWA_KV_WIDTH), whole),
        ],
        out_specs=pl.BlockSpec((1, SWA_TQ, D_MODEL), tile),
        out_shape=jax.ShapeDtypeStruct((bsz, TOK, D_MODEL), BF16),
        compiler_params=_params("parallel", "arbitrary"),
    )(sink.astype(F32), q, qr, kr, v)


def _mlp_kernel(*refs, final_norm):
    oa_ref, ob_ref, mod_ref, modc_ref, g_ref, gf_ref, wo_ref, w1_ref, w2_ref, out_ref = refs[-10:]
    half = wo_ref.shape[0] // 2
    x = _stream_tile(refs[:-10])
    n = x.shape[0]
    mods = _mod_vectors(mod_ref, modc_ref, n, (2, 3, 4, 5), not final_norm)
    halves = (slice(0, n // 2), slice(n // 2, n))
    staged = []
    for r in halves:
        gate_mix, shift, scale, gate_mlp = [v if v.shape[0] == 1 else v[r] for v in mods]
        y = _mm(oa_ref[0, r, :], wo_ref[:half, :]) + _mm(ob_ref[0, r, :], wo_ref[half:, :])
        x1 = x[r] + gate_mix * y
        staged.append((x1, _norm_modulate(x1, g_ref[...], shift, scale).astype(BF16), gate_mlp))
    for r, (x1, h, gate_mlp) in zip(halves, staged):
        acc = jnp.zeros((n // 2, D_MODEL), F32)
        for c in range(D_FF // FF_CHUNK):
            t = jnp.maximum(_mm(h, w1_ref[:, c * FF_CHUNK:(c + 1) * FF_CHUNK]), 0.0)
            acc = acc + _mm((t * t).astype(BF16), w2_ref[c * FF_CHUNK:(c + 1) * FF_CHUNK, :])
        x2 = x1 + gate_mlp * acc
        if final_norm:
            x2 = x2 * lax.rsqrt(jnp.mean(x2 * x2, axis=-1, keepdims=True) + EPS) * gf_ref[...]
        out_ref[0, r, :] = x2


def _outproj_mlp(stream, oa, ob, ob_block, l, mods, gains, gain_final, wo, w1, w2):
    bsz = stream[0].shape[0]
    half = D_MODEL // 2
    final_norm = l == DEPTH - 1
    tm = TP_FINAL if final_norm else TP
    tiles = (SEQ if final_norm else TOK) // tm
    tile = lambda b, t: (b, t, 0)
    return pl.pallas_call(
        functools.partial(_mlp_kernel, final_norm=final_norm),
        grid=(bsz, tiles),
        in_specs=_stream_specs(stream, tm) + [
            pl.BlockSpec((1, tm, half), tile),
            pl.BlockSpec((1, tm, half), lambda b, t: (b, t, ob_block)),
        ] + _mod_specs(l) + [
            _layer_spec(l, 1, D_MODEL),
            pl.BlockSpec((1, D_MODEL), lambda b, t: (0, 0)),
            _layer_spec(l // 2, D_MODEL, D_MODEL),
            _layer_spec(l, D_MODEL, D_FF),
            _layer_spec(l, D_FF, D_MODEL),
        ],
        out_specs=pl.BlockSpec((1, tm, D_MODEL), tile),
        out_shape=jax.ShapeDtypeStruct((bsz, tiles * tm, D_MODEL), F32),
        compiler_params=_params("parallel", "parallel"),
    )(*_stream_args(stream, tm), oa, ob, mods, mods, gains, gain_final, wo, w1, w2)


def _even_in_weight(w):
    n_na = 3 * NA_WIDTH
    scale = np.ones((w.shape[-1],), np.float32)
    scale[:NA_WIDTH] = HEAD_DIM ** -0.5 * LOG2E
    scale[n_na:n_na + GLA_QK_WIDTH] = GLA_DK ** -0.5
    w = w * jnp.asarray(scale)
    pad = n_na + GLA_IN_WIDTH - w.shape[-1]
    return jnp.pad(w, ((0, 0), (0, 0), (0, pad))).astype(BF16)


def _odd_in_weight(w):
    n = w.shape[0]
    n_rot = D_MODEL + SWA_KV_WIDTH
    rot = w[..., :n_rot].reshape(n, D_MODEL, n_rot // HEAD_DIM, HEAD_DIM // 2, 2)
    rot = jnp.swapaxes(rot, 3, 4).reshape(n, D_MODEL, n_rot)
    q = _swa_head_order(rot[..., :D_MODEL] * (HEAD_DIM ** -0.5 * LOG2E), axis=2)
    return jnp.concatenate([q, rot[..., D_MODEL:], w[..., n_rot:]], axis=2).astype(BF16)


def _swa_head_order(a, axis):
    shape = a.shape
    split = shape[:axis] + (SWA_KV_HEADS // 2, 2, SWA_GROUP, HEAD_DIM) + shape[axis + 1:]
    return jnp.swapaxes(a.reshape(split), axis + 1, axis + 2).reshape(shape)


def kernel(x, c, ctx, c_ctx, ada_w, ada_b, norm_mix, norm_mlp, mlp_w1, mlp_w2, ab_w_in, ab_w_out, na_rpb,
           gla_wa2, gla_ba, gla_gnorm, swa_w_in, swa_w_out, swa_sink, norm_final):
    bsz = x.shape[0]
    assert x.shape == (bsz, SEQ, D_MODEL) and ctx.shape == (bsz, CTX_LEN, D_MODEL) and bsz <= 8

    cvec = jnp.zeros((MOD_ROWS, D_MODEL), F32).at[:bsz].set(c).at[8].set(c_ctx)
    mods = _ada_table(cvec, ada_w, ada_b).reshape(DEPTH, MOD_ROWS, 6, D_MODEL)
    cos_t, sin_t = _rope_tables()
    gain_final = norm_final.reshape(1, D_MODEL)
    g_mix = norm_mix.reshape(DEPTH, 1, D_MODEL)
    g_mlp = norm_mlp.reshape(DEPTH, 1, D_MODEL)
    w_in_even = _even_in_weight(ab_w_in)
    w_in_odd = _odd_in_weight(swa_w_in)
    w_out_even = ab_w_out.astype(BF16)
    w_out_odd = _swa_head_order(swa_w_out, axis=1).astype(BF16)
    w1 = mlp_w1.astype(BF16)
    w2 = mlp_w2.astype(BF16)
    na_bias = _na_bias_table(na_rpb)

    stream = (x, ctx)
    for l in range(DEPTH):
        j = l // 2
        if l % 2 == 0:
            na_in, gla_in = _inproj_even(stream, l, mods, g_mix, w_in_even)
            oa = _na_attention(na_in, na_bias[j])
            ob = _gla(gla_in, gla_wa2[j], gla_ba[j], gla_gnorm[j])
            ob_block = 0
            wo = w_out_even
        else:
            q, qr, kr, v = _inproj_odd(stream, l, mods, g_mix, w_in_odd, cos_t, sin_t)
            oa = ob = _swa_attention(q, qr, kr, v, swa_sink[j] * LOG2E)
            ob_block = 1
            wo = w_out_odd
        xs = _outproj_mlp(stream, oa, ob, ob_block, l, mods, g_mlp, gain_final, wo, w1, w2)
        stream = (xs,)
    return xs
```

```python
import functools

import numpy as np
import jax
import jax.numpy as jnp
from jax import lax
from jax.experimental import pallas as pl
from jax.experimental.pallas import tpu as pltpu

D_MODEL = 1024
SEQ = 2048
DEPTH = 4
GRID_W = 64
GRID_ROWS = SEQ // GRID_W
CTX_LEN = 256
TOK = SEQ + CTX_LEN
HEAD_DIM = 64
EPS = 1e-6
NEG_INF = -1e30
LOG2E = 1.4426950408889634

NA_HEADS = 8
NA_WIN_ROWS = 8
NA_WIN_COLS = 16
NA_WIDTH = NA_HEADS * HEAD_DIM
NA_Q_ROWS = 4
NA_K_ROWS = NA_Q_ROWS + NA_WIN_ROWS - 1
NA_QN = NA_Q_ROWS * GRID_W
NA_KN = NA_K_ROWS * GRID_W
NA_ROW_BLOCKS = GRID_ROWS // NA_Q_ROWS

GLA_HEADS = 4
GLA_DK = 64
GLA_DV = 128
GLA_RANK = 16
GLA_NORMALIZER = 16.0
GLA_CHUNK = 64
GLA_QK_WIDTH = GLA_HEADS * GLA_DK
GLA_V_WIDTH = GLA_HEADS * GLA_DV
GLA_IN_WIDTH = 2 * GLA_QK_WIDTH + 2 * GLA_V_WIDTH + 128
GLA_CTX_CHUNKS = CTX_LEN // GLA_CHUNK
GLA_CHUNKS = TOK // GLA_CHUNK

SWA_HEADS = 16
SWA_KV_HEADS = 4
SWA_GROUP = SWA_HEADS // SWA_KV_HEADS
SWA_WINDOW = 128
SWA_KV_WIDTH = SWA_KV_HEADS * HEAD_DIM
SWA_TQ = 256
SWA_NLOC = SWA_TQ + 2 * SWA_WINDOW

D_FF = 4 * D_MODEL
FF_CHUNK = 1024
ROPE_THETA = 10000.0

TM = 256
TILES = TOK // TM
LAT_TILES = SEQ // TM
TP = 768
TP_FINAL = 1024
_ROW_HALVES = (slice(0, TP // 2), slice(TP // 2, TP))
MOD_ROWS = 16
VMEM_LIMIT = 56 * 1024 * 1024

F32 = jnp.float32
BF16 = jnp.bfloat16


def _nt(a, b):
    return lax.dot_general(a, b, (((1,), (1,)), ((), ())), preferred_element_type=F32)


def _tn(a, b):
    return lax.dot_general(a, b, (((0,), (0,)), ((), ())), preferred_element_type=F32)


def _mm(a, b):
    return jnp.dot(a, b, preferred_element_type=F32)


def _params(*sem):
    return pltpu.CompilerParams(dimension_semantics=sem, vmem_limit_bytes=VMEM_LIMIT)


def _mod_specs(l):
    return [
        pl.BlockSpec((None, 1, 6, D_MODEL), lambda b, t: (l, b, 0, 0)),
        pl.BlockSpec((None, 1, 6, D_MODEL), lambda b, t: (l, 8, 0, 0)),
    ]


def _mod_vectors(mod_ref, modc_ref, rows, idx, has_ctx):
    mod = mod_ref[0]
    if not has_ctx:
        return [mod[i:i + 1] for i in idx]
    modc = modc_ref[0]
    row = lax.broadcasted_iota(jnp.int32, (rows, 1), 0)
    is_ctx = (pl.program_id(1) == TOK // rows - 1) & (row >= rows - CTX_LEN)
    return [jnp.where(is_ctx, modc[i:i + 1], mod[i:i + 1]) for i in idx]


def _layer_spec(l, *tail):
    return pl.BlockSpec((None,) + tail, lambda b, t: (l,) + (0,) * len(tail), pipeline_mode=pl.Buffered(1))


def _stream_specs(stream, tm):
    if len(stream) == 1:
        return [pl.BlockSpec((1, tm, D_MODEL), lambda b, t: (b, t, 0))]
    n = tm // CTX_LEN
    last = SEQ // CTX_LEN - 1
    pieces = [pl.BlockSpec((1, CTX_LEN, D_MODEL), functools.partial(
        lambda i, b, t: (b, jnp.minimum(n * t + i, last), 0), i)) for i in range(n)]
    return pieces + [pl.BlockSpec((1, CTX_LEN, D_MODEL), lambda b, t: (b, 0, 0))]


def _stream_args(stream, tm):
    return list(stream) if len(stream) == 1 else [stream[0]] * (tm // CTX_LEN) + [stream[1]]


def _stream_tile(refs):
    if len(refs) == 1:
        return refs[0][0]
    n = len(refs) - 1
    tail = jnp.where(pl.program_id(1) == TOK // (n * CTX_LEN) - 1, refs[n][0], refs[n - 1][0])
    return jnp.concatenate([r[0] for r in refs[:n - 1]] + [tail], axis=0)


def _ada_kernel(c_ref, w_ref, b_ref, o_ref):
    s = c_ref[...]
    s = s * jax.nn.sigmoid(s)
    o_ref[0] = _mm(s.astype(BF16), w_ref[0].astype(BF16)) + b_ref[0]


def _ada_table(cvec, ada_w, ada_b):
    nb = 6 * D_MODEL // 1024
    return pl.pallas_call(
        _ada_kernel,
        grid=(DEPTH, nb),
        in_specs=[
            pl.BlockSpec((MOD_ROWS, D_MODEL), lambda l, n: (0, 0)),
            pl.BlockSpec((1, D_MODEL, 1024), lambda l, n: (l, 0, n)),
            pl.BlockSpec((1, 1, 1024), lambda l, n: (l, 0, n)),
        ],
        out_specs=pl.BlockSpec((1, MOD_ROWS, 1024), lambda l, n: (l, 0, n)),
        out_shape=jax.ShapeDtypeStruct((DEPTH, MOD_ROWS, 6 * D_MODEL), F32),
        compiler_params=_params("parallel", "parallel"),
    )(cvec, ada_w, ada_b.reshape(DEPTH, 1, 6 * D_MODEL))


def _norm_modulate(x, gain, shift, scale):
    y = x * lax.rsqrt(jnp.mean(x * x, axis=-1, keepdims=True) + EPS) * gain
    return y * (1.0 + scale) + shift


def _inproj_even_kernel(*refs):
    mod_ref, modc_ref, g_ref, w_ref, na_ref, gla_ref = refs[-6:]
    shift, scale = _mod_vectors(mod_ref, modc_ref, TP, (0, 1), True)
    x = _stream_tile(refs[:-6])
    n_na = 3 * NA_WIDTH
    hs = [_norm_modulate(x[r], g_ref[...], shift[r], scale[r]).astype(BF16) for r in _ROW_HALVES]
    for r, h in zip(_ROW_HALVES, hs):
        na_ref[0, r, :] = _mm(h, w_ref[:, :n_na]).astype(BF16)
        gla_ref[0, r, :] = _mm(h, w_ref[:, n_na:])


def _inproj_even(stream, l, mods, gains, w):
    bsz = stream[0].shape[0]
    n_na = 3 * NA_WIDTH
    return pl.pallas_call(
        _inproj_even_kernel,
        grid=(bsz, TOK // TP),
        in_specs=_stream_specs(stream, TP) + _mod_specs(l) + [
            _layer_spec(l, 1, D_MODEL),
            _layer_spec(l // 2, D_MODEL, n_na + GLA_IN_WIDTH),
        ],
        out_specs=[
            pl.BlockSpec((1, TP, n_na), lambda b, t: (b, t, 0)),
            pl.BlockSpec((1, TP, GLA_IN_WIDTH), lambda b, t: (b, t, 0)),
        ],
        out_shape=[
            jax.ShapeDtypeStruct((bsz, TOK, n_na), BF16),
            jax.ShapeDtypeStruct((bsz, TOK, GLA_IN_WIDTH), F32),
        ],
        compiler_params=_params("parallel", "parallel"),
    )(*_stream_args(stream, TP), mods, mods, gains, w)


def _rope(a, cos, sin, first_half):
    swapped = jnp.where(first_half, pltpu.roll(a, 96, 1), pltpu.roll(a, 32, 1))
    return a * cos + swapped * sin


def _inproj_odd_kernel(*refs):
    mod_ref, modc_ref, g_ref, w_ref, cos_ref, sin_ref, q_ref, qr_ref, kr_ref, v_ref = refs[-10:]
    shift, scale = _mod_vectors(mod_ref, modc_ref, TP, (0, 1), True)
    x = _stream_tile(refs[:-10])
    first_half = (lax.broadcasted_iota(jnp.int32, (TP // 2, 128), 1) % HEAD_DIM) < HEAD_DIM // 2
    wide = 256
    hs = [_norm_modulate(x[r], g_ref[...], shift[r], scale[r]).astype(BF16) for r in _ROW_HALVES]
    for r, h in zip(_ROW_HALVES, hs):
        cos = cos_ref[r, :]
        sin = sin_ref[r, :]
        for j in range(D_MODEL // wide):
            a = _mm(h, w_ref[:, j * wide:(j + 1) * wide])
            q_ref[0, r, j * wide:(j + 1) * wide] = a.astype(BF16)
            for t in range(wide // 128):
                c0 = j * wide + t * 128
                qr_ref[0, r, c0:c0 + 128] = _rope(a[:, t * 128:(t + 1) * 128], cos, sin, first_half).astype(BF16)
        a = _mm(h, w_ref[:, D_MODEL:D_MODEL + SWA_KV_WIDTH])
        for t in range(SWA_KV_WIDTH // 128):
            kr_ref[0, r, t * 128:(t + 1) * 128] = _rope(a[:, t * 128:(t + 1) * 128], cos, sin,
                                                        first_half).astype(BF16)
        v_ref[0, r, :] = _mm(h, w_ref[:, D_MODEL + SWA_KV_WIDTH:]).astype(BF16)


def _inproj_odd(stream, l, mods, gains, w, cos_t, sin_t):
    bsz = stream[0].shape[0]
    n_in = D_MODEL + 2 * SWA_KV_WIDTH
    tile = lambda b, t: (b, t, 0)
    return pl.pallas_call(
        _inproj_odd_kernel,
        grid=(bsz, TOK // TP),
        in_specs=_stream_specs(stream, TP) + _mod_specs(l) + [
            _layer_spec(l, 1, D_MODEL),
            _layer_spec(l // 2, D_MODEL, n_in),
            pl.BlockSpec((TP, 128), lambda b, t: (t, 0)),
            pl.BlockSpec((TP, 128), lambda b, t: (t, 0)),
        ],
        out_specs=[
            pl.BlockSpec((1, TP, D_MODEL), tile),
            pl.BlockSpec((1, TP, D_MODEL), tile),
            pl.BlockSpec((1, TP, SWA_KV_WIDTH), tile),
            pl.BlockSpec((1, TP, SWA_KV_WIDTH), tile),
        ],
        out_shape=[
            jax.ShapeDtypeStruct((bsz, TOK, D_MODEL), BF16),
            jax.ShapeDtypeStruct((bsz, TOK, D_MODEL), BF16),
            jax.ShapeDtypeStruct((bsz, TOK, SWA_KV_WIDTH), BF16),
            jax.ShapeDtypeStruct((bsz, TOK, SWA_KV_WIDTH), BF16),
        ],
        compiler_params=_params("parallel", "parallel"),
    )(*_stream_args(stream, TP), mods, mods, gains, w, cos_t, sin_t)


def _rope_tables():
    t = np.arange(SEQ)
    n_freq = HEAD_DIM // 4
    inv = jnp.asarray(ROPE_THETA, F32) ** (-jnp.arange(n_freq, dtype=F32) / n_freq)
    row = jnp.asarray(t // GRID_W, F32)
    col = jnp.asarray(t % GRID_W, F32)
    ang = jnp.concatenate([row[:, None] * inv, col[:, None] * inv], axis=-1)
    cos, sin = jnp.cos(ang), jnp.sin(ang)
    cos_t = jnp.tile(cos, (1, 4))
    sin_t = jnp.tile(jnp.concatenate([-sin, sin], axis=-1), (1, 2))
    cos_t = jnp.concatenate([cos_t, jnp.ones((CTX_LEN, 128), F32)], axis=0)
    sin_t = jnp.concatenate([sin_t, jnp.zeros((CTX_LEN, 128), F32)], axis=0)
    return cos_t, sin_t


NA_ROW_OFFSETS = 2 * NA_WIN_ROWS - 1
NA_BIAS_BLOCKS = NA_ROW_OFFSETS + 1


def _na_col_select():
    kc = np.arange(GRID_W)[:, None]
    c = np.arange(128)[None, :] % GRID_W
    w0 = np.clip(c - NA_WIN_COLS // 2, 0, GRID_W - NA_WIN_COLS)
    ok = (kc >= w0) & (kc < w0 + NA_WIN_COLS)
    offset = kc - c + NA_WIN_COLS - 1
    sel = (offset[None] == np.arange(2 * NA_WIN_COLS - 1)[:, None, None]) & ok[None]
    return sel.astype(np.float32), ok


_NA_COL_SELECT, _NA_COL_OK = _na_col_select()


def _na_bias_table(rpb):
    layers = rpb.shape[0]
    sel = jnp.asarray(_NA_COL_SELECT.reshape(2 * NA_WIN_COLS - 1, GRID_W * 128))
    col = jnp.dot(rpb.astype(F32).reshape(-1, 2 * NA_WIN_COLS - 1), sel, precision=lax.Precision.HIGHEST)
    col = col.reshape(layers * NA_HEADS, NA_ROW_OFFSETS, GRID_W, 128)
    col = jnp.where(jnp.asarray(_NA_COL_OK), col * LOG2E, NEG_INF)
    col = jnp.concatenate([col, jnp.full((layers * NA_HEADS, 1, GRID_W, 128), NEG_INF, F32)], axis=1)
    return col.reshape(layers, NA_HEADS, NA_BIAS_BLOCKS, GRID_W, 128)


def _na_bias(bias_ref, head, rb):
    lane = lax.broadcasted_iota(jnp.int32, (1, 128), 1)
    low = lane < GRID_W
    q_row0 = rb * NA_Q_ROWS
    k_row0 = jnp.clip(q_row0 - NA_WIN_ROWS // 2, 0, GRID_ROWS - NA_K_ROWS)
    key_rows = []
    for j in range(NA_K_ROWS):
        kr = k_row0 + j
        blocks = []
        for i in range(NA_Q_ROWS):
            r = q_row0 + i
            r0 = jnp.clip(r - NA_WIN_ROWS // 2, 0, GRID_ROWS - NA_WIN_ROWS)
            inside = (kr >= r0) & (kr < r0 + NA_WIN_ROWS)
            idx = jnp.where(inside, kr - r + NA_WIN_ROWS - 1, NA_ROW_OFFSETS)
            blocks.append(bias_ref[0, head * NA_BIAS_BLOCKS + idx])
        pieces = [jnp.where(low, blocks[i], blocks[i + 1]) for i in range(0, NA_Q_ROWS, 2)]
        key_rows.append(jnp.concatenate(pieces, axis=1))
    return jnp.concatenate(key_rows, axis=0)


def _lane_pair_attention(blocks, lookahead=4):
    lane = lax.broadcasted_iota(jnp.int32, (1, 128), 1)
    half = (lane < HEAD_DIM, lane >= HEAD_DIM)
    row = lax.broadcasted_iota(jnp.int32, (128, 1), 0)
    row_half = (row < HEAD_DIM, row >= HEAD_DIM)

    augmented = {}

    def with_ones(v, h):
        if (id(v), h) not in augmented:
            augmented[(id(v), h)] = jnp.where(half[h], v, jnp.ones_like(v))
        return augmented[(id(v), h)]

    def scores(j, h):
        qs, keys, _, biases, _ = blocks[j]
        s = [_nt(k, jnp.where(half[h], q, jnp.zeros_like(q))) for q, k in zip(qs, keys)]
        if biases[h] is not None:
            bias = biases[h]() if callable(biases[h]) else biases[h]
            s = [x if b is None else x + b for x, b in zip(s, bias)]
        return s

    def output(j, h, s):
        _, _, values, _, floor = blocks[j]
        m = jnp.max(s[0], axis=0, keepdims=True)
        for x in s[1:]:
            m = jnp.maximum(m, jnp.max(x, axis=0, keepdims=True))
        if floor[h] is not None:
            m = jnp.maximum(m, floor[h])
        acc = None
        for x, v in zip(s, values):
            p = jnp.exp2((x - m).astype(BF16))
            pv = _tn(with_ones(v, h), p)
            acc = pv if acc is None else acc + pv
        den = acc[HEAD_DIM:HEAD_DIM + 1] if h == 0 else acc[0:1]
        if floor[h] is not None:
            den = den + jnp.exp2(floor[h] - m)
        return acc / den

    heads = [(j, h) for j in range(len(blocks)) for h in range(2)]
    res = {}
    ahead = [scores(*head) for head in heads[:lookahead]]
    for i, (j, h) in enumerate(heads):
        if i + lookahead < len(heads):
            ahead.append(scores(*heads[i + lookahead]))
        res[(j, h)] = output(j, h, ahead.pop(0))
    return [jnp.where(row_half[0], res[(j, 0)], res[(j, 1)]).T for j in range(len(blocks))]


def _na_kernel(q_ref, k_ref, v_ref, bias_ref, o_ref, full_ref):
    rb = pl.program_id(2)
    n_blocks = q_ref.shape[2] // 128
    last = NA_ROW_BLOCKS - 1

    @pl.when(rb < NA_ROW_BLOCKS)
    def _latent():
        k_row0 = jnp.clip(rb * NA_Q_ROWS - NA_WIN_ROWS // 2, 0, GRID_ROWS - NA_K_ROWS)
        start = pl.multiple_of(k_row0 * GRID_W, GRID_W)
        cls = jnp.where(rb == 0, 0, jnp.where(rb == last, 2, 1))

        @pl.when((pl.program_id(1) == 0) & ((rb <= 1) | (rb == last)))
        def _assemble():
            for head in range(2 * n_blocks):
                full_ref[head * 3 + cls] = _na_bias(bias_ref, head, rb)

        blocks = []
        for j in range(n_blocks):
            ln = slice(j * 128, (j + 1) * 128)
            keys = [k_ref[0, pl.ds(start, NA_KN), ln], k_ref[0, SEQ:, ln]]
            values = [v_ref[0, pl.ds(start, NA_KN), ln], v_ref[0, SEQ:, ln]]
            biases = [functools.partial(lambda head: [full_ref[head * 3 + cls], None], 2 * j + h)
                      for h in range(2)]
            q = q_ref[0, :, ln]
            blocks.append(([q, q], keys, values, biases, [None, None]))
        for j, o in enumerate(_lane_pair_attention(blocks)):
            o_ref[0, :, j * 128:(j + 1) * 128] = o.astype(BF16)

    @pl.when(rb == NA_ROW_BLOCKS)
    def _context():
        blocks = []
        for j in range(n_blocks):
            ln = slice(j * 128, (j + 1) * 128)
            blocks.append(([q_ref[0, :, ln]], [k_ref[0, SEQ:, ln]], [v_ref[0, SEQ:, ln]],
                           [None, None], [None, None]))
        for j, o in enumerate(_lane_pair_attention(blocks)):
            o_ref[0, :, j * 128:(j + 1) * 128] = o.astype(BF16)


def _na_attention(qkv, bias):
    bsz = qkv.shape[0]
    hps = 8
    groups = NA_HEADS // hps
    w = hps * HEAD_DIM
    return pl.pallas_call(
        _na_kernel,
        grid=(groups, bsz, NA_ROW_BLOCKS + 1),
        in_specs=[
            pl.BlockSpec((1, NA_QN, w), lambda p, b, r: (b, r, p)),
            pl.BlockSpec((1, TOK, w), lambda p, b, r: (b, 0, groups + p)),
            pl.BlockSpec((1, TOK, w), lambda p, b, r: (b, 0, 2 * groups + p)),
            pl.BlockSpec((1, hps * NA_BIAS_BLOCKS, GRID_W, 128), lambda p, b, r: (p, 0, 0, 0)),
        ],
        out_specs=pl.BlockSpec((1, NA_QN, w), lambda p, b, r: (b, r, p)),
        out_shape=jax.ShapeDtypeStruct((bsz, TOK, NA_WIDTH), BF16),
        scratch_shapes=[pltpu.VMEM((3 * hps, NA_KN, NA_QN), F32)],
        compiler_params=_params("arbitrary", "arbitrary", "arbitrary"),
    )(qkv, qkv, qkv, bias.reshape(groups, hps * NA_BIAS_BLOCKS, GRID_W, 128))


def _block_rows(i, size):
    start = i * size
    return pl.ds(start if isinstance(start, int) else pl.multiple_of(start, size), size)


def _split3(g):
    hi = g.astype(BF16)
    r1 = g - hi.astype(F32)
    mid = r1.astype(BF16)
    lo = (r1 - mid.astype(F32)).astype(BF16)
    return hi, mid, lo


def _gla_kernel(q_ref, k_ref, v_ref, gate_ref, lr_ref, wa2_ref, ba_ref, gn_ref, o_ref,
                acc_ref, cum_ref, qt_ref, u_ref, dec_ref, sp_ref, st_ref):
    c = GLA_CHUNK
    ii = lax.broadcasted_iota(jnp.int32, (c, c), 0)
    jj = lax.broadcasted_iota(jnp.int32, (c, c), 1)
    incl = (jj <= ii, jj >= ii)
    tri = tuple(jnp.where(m, 1.0, 0.0).astype(BF16) for m in incl)
    head0 = lax.broadcasted_iota(jnp.int32, (1, 128), 1) < GLA_DK
    per_tile = TM // c

    def decays(t, carry):
        rows = _block_rows(t, TM)
        lr = lr_ref[0, rows, :].astype(BF16)
        z2 = _mm(lr, wa2_ref[...]) + ba_ref[...]
        for d in range(2):
            z = z2[:, d * 128:(d + 1) * 128]
            g = (jnp.minimum(z, 0.0) - jnp.log1p(jnp.exp(-jnp.abs(z)))) / GLA_NORMALIZER
            wide = jnp.concatenate([g[i * c:(i + 1) * c] for i in range(per_tile)], axis=1)
            hi, mid, lo = _split3(wide)
            cum = _mm(tri[d], hi) + _mm(tri[d], mid) + _mm(tri[d], lo)
            cum_ref[d, rows, :] = jnp.concatenate([cum[:, i * 128:(i + 1) * 128] for i in range(per_tile)], axis=0)
        return carry

    t2 = lax.broadcasted_iota(jnp.int32, (c, 2 * c), 0)
    j2 = lax.broadcasted_iota(jnp.int32, (c, 2 * c), 1) % c
    incl2 = (j2 <= t2, j2 >= t2)

    def by_head(a):
        zero = jnp.zeros_like(a)
        return jnp.concatenate([jnp.where(head0, a, zero), jnp.where(head0, zero, a)], axis=0)

    def weights(ci):
        rows = _block_rows(ci, c)
        qc = q_ref[0, rows, :]
        kc = k_ref[0, rows, :]
        raw = []
        k_ends = []
        for d in range(2):
            cum = cum_ref[d, rows, :]
            tot = cum[c - 1:c, :] if d == 0 else cum[0:1, :]
            q_t = (qc * jnp.exp(cum)).astype(BF16)
            k_t = (kc * jnp.exp(-cum)).astype(BF16)
            k_ends.append((kc * jnp.exp(tot - cum)).astype(BF16))
            dec_ref[d, ci] = jnp.exp(tot)
            qt_ref[ci, :, d * 128:(d + 1) * 128] = by_head(q_t)
            raw.append(_nt(q_t, by_head(k_t)))
        return raw, k_ends

    def outputs(ci, raw, k_ends):
        rows = _block_rows(ci, c)
        v2 = v_ref[0, rows, :].astype(BF16)
        zero_v = jnp.zeros((c, GLA_DV), BF16)
        v_diag = jnp.concatenate([jnp.concatenate([v2[:, :GLA_DV], zero_v], axis=1),
                                  jnp.concatenate([zero_v, v2[:, GLA_DV:]], axis=1)], axis=0)
        p_sum = jnp.where(incl2[0], raw[0], 0.0) + jnp.where(incl2[1], raw[1], 0.0)
        acc_ref[rows, :] = _mm(p_sum.astype(BF16), v_diag)
        uu = _tn(v2, jnp.concatenate(k_ends, axis=1))
        for d in range(2):
            blk = uu[:, d * 128:(d + 1) * 128]
            u_ref[d, ci] = jnp.where(head0, blk[:GLA_DV], blk[GLA_DV:])

    def intra(t):
        ahead = [weights(t * per_tile + i) for i in range(per_tile)]
        for i in range(per_tile):
            outputs(t * per_tile + i, *ahead[i])

    def decays_then_intra(t, carry):
        decays(t + 1, carry)
        intra(t)
        return carry

    decays(0, 0)
    lax.fori_loop(0, TILES - 1, decays_then_intra, 0)
    intra(TILES - 1)

    st_ref[...] = jnp.zeros_like(st_ref)

    def scan(i, carry):
        order = (jnp.where(i < GLA_CTX_CHUNKS, GLA_CHUNKS - GLA_CTX_CHUNKS + i, i - GLA_CTX_CHUNKS),
                 GLA_CHUNKS - 1 - i)
        for d in range(2):
            s = st_ref[d]
            sp_ref[order[d], :, d * 128:(d + 1) * 128] = s.astype(BF16)
            st_ref[d] = dec_ref[d, order[d]] * s + u_ref[d, order[d]]
        return carry

    lax.fori_loop(0, GLA_CHUNKS, scan, 0)

    def finish(t, carry):
        rows = _block_rows(t, TM)
        inter = [_nt(qt_ref[t * per_tile + i], sp_ref[t * per_tile + i]) for i in range(per_tile)]
        gate = gate_ref[0, rows, :]
        sw = gate * jax.nn.sigmoid(gate)
        for h in range(2):
            vs = slice(h * GLA_DV, (h + 1) * GLA_DV)
            o = acc_ref[rows, vs] + jnp.concatenate([x[h * c:(h + 1) * c] for x in inter], axis=0)
            o = o * lax.rsqrt(jnp.mean(o * o, axis=-1, keepdims=True) + EPS)
            o_ref[0, rows, vs] = (o * gn_ref[:, vs] * sw[:, vs]).astype(BF16)
        return carry

    lax.fori_loop(0, TILES, finish, 0)


def _gla(gla_in, wa2, ba, gnorm):
    bsz = gla_in.shape[0]
    pairs = GLA_HEADS // 2
    qk_blocks = GLA_QK_WIDTH // 128
    v_blocks = GLA_V_WIDTH // 256
    v0 = 2 * GLA_QK_WIDTH // 256
    lr_block = (2 * GLA_QK_WIDTH + 2 * GLA_V_WIDTH) // 128
    wa2_rows = jnp.zeros((2, 128, GLA_QK_WIDTH), F32)
    for d in range(2):
        wa2_rows = wa2_rows.at[d, d * GLA_RANK:(d + 1) * GLA_RANK].set(wa2[d])
    wa2 = wa2_rows.reshape(2, 128, pairs, 128).transpose(1, 2, 0, 3).reshape(128, pairs * 256).astype(BF16)
    ba = ba.reshape(2, pairs, 128).transpose(1, 0, 2).reshape(1, pairs * 256)
    return pl.pallas_call(
        _gla_kernel,
        grid=(bsz, pairs),
        in_specs=[
            pl.BlockSpec((1, TOK, 128), lambda b, p: (b, 0, p)),
            pl.BlockSpec((1, TOK, 128), lambda b, p: (b, 0, qk_blocks + p)),
            pl.BlockSpec((1, TOK, 256), lambda b, p: (b, 0, v0 + p)),
            pl.BlockSpec((1, TOK, 256), lambda b, p: (b, 0, v0 + v_blocks + p)),
            pl.BlockSpec((1, TOK, 128), lambda b, p: (b, 0, lr_block)),
            pl.BlockSpec((128, 256), lambda b, p: (0, p)),
            pl.BlockSpec((1, 256), lambda b, p: (0, p)),
            pl.BlockSpec((1, 256), lambda b, p: (0, p)),
        ],
        out_specs=pl.BlockSpec((1, TOK, 256), lambda b, p: (b, 0, p)),
        out_shape=jax.ShapeDtypeStruct((bsz, TOK, GLA_V_WIDTH), BF16),
        scratch_shapes=[
            pltpu.VMEM((TOK, 2 * GLA_DV), F32),
            pltpu.VMEM((2, TOK, 128), F32),
            pltpu.VMEM((GLA_CHUNKS, 2 * GLA_CHUNK, 256), BF16),
            pltpu.VMEM((2, GLA_CHUNKS, GLA_DV, 128), F32),
            pltpu.VMEM((2, GLA_CHUNKS, 1, 128), F32),
            pltpu.VMEM((GLA_CHUNKS, GLA_DV, 256), BF16),
            pltpu.VMEM((2, GLA_DV, 128), F32),
        ],
        compiler_params=_params("parallel", "parallel"),
    )(gla_in, gla_in, gla_in, gla_in, gla_in, wa2, ba, gnorm.reshape(1, GLA_V_WIDTH))


def _swa_kernel(sink_ref, q_ref, qr_ref, k_ref, v_ref, o_ref):
    qb = pl.program_id(1)
    pairs = SWA_KV_HEADS // 2

    def sinks(kp, j):
        return [sink_ref[(2 * kp + hk) * SWA_GROUP + j] for hk in range(2)]

    def lanes(kp, j):
        return slice((kp * SWA_GROUP + j) * 128, (kp * SWA_GROUP + j + 1) * 128)

    @pl.when(qb < LAT_TILES)
    def _latent():
        q0 = qb * SWA_TQ
        start = pl.multiple_of(jnp.clip(q0 - SWA_WINDOW, 0, SEQ - SWA_NLOC), SWA_WINDOW)
        kpos = start + lax.broadcasted_iota(jnp.int32, (SWA_NLOC, SWA_TQ), 0)
        qpos = q0 + lax.broadcasted_iota(jnp.int32, (SWA_NLOC, SWA_TQ), 1)
        window = jnp.where(jnp.abs(kpos - qpos) <= SWA_WINDOW, 0.0, NEG_INF)
        blocks = []
        for kp in range(pairs):
            kv = slice(kp * 128, (kp + 1) * 128)
            keys = [k_ref[0, pl.ds(start, SWA_NLOC), kv], k_ref[0, SEQ:, kv]]
            values = [v_ref[0, pl.ds(start, SWA_NLOC), kv], v_ref[0, SEQ:, kv]]
            for j in range(SWA_GROUP):
                ln = lanes(kp, j)
                blocks.append(([qr_ref[0, :, ln], q_ref[0, :, ln]], keys, values,
                               [[window, None], [window, None]], sinks(kp, j)))
        for i, o in enumerate(_lane_pair_attention(blocks)):
            o_ref[0, :, i * 128:(i + 1) * 128] = o.astype(BF16)

    @pl.when(qb == LAT_TILES)
    def _context():
        blocks = []
        for kp in range(pairs):
            kv = slice(kp * 128, (kp + 1) * 128)
            for j in range(SWA_GROUP):
                blocks.append(([q_ref[0, :, lanes(kp, j)]], [k_ref[0, SEQ:, kv]], [v_ref[0, SEQ:, kv]],
                               [None, None], sinks(kp, j)))
        for i, o in enumerate(_lane_pair_attention(blocks)):
            o_ref[0, :, i * 128:(i + 1) * 128] = o.astype(BF16)


def _swa_attention(q, qr, kr, v, sink):
    bsz = q.shape[0]
    tile = lambda b, t: (b, t, 0)
    whole = lambda b, t: (b, 0, 0)
    return pl.pallas_call(
        _swa_kernel,
        grid=(bsz, TILES),
        in_specs=[
            pl.BlockSpec(memory_space=pltpu.SMEM),
            pl.BlockSpec((1, SWA_TQ, D_MODEL), tile),
            pl.BlockSpec((1, SWA_TQ, D_MODEL), tile),
            pl.BlockSpec((1, TOK, SWA_KV_WIDTH), whole),
            pl.BlockSpec((1, TOK, SWA_KV_WIDTH), whole),
        ],
        out_specs=pl.BlockSpec((1, SWA_TQ, D_MODEL), tile),
        out_shape=jax.ShapeDtypeStruct((bsz, TOK, D_MODEL), BF16),
        compiler_params=_params("parallel", "arbitrary"),
    )(sink.astype(F32), q, qr, kr, v)


def _mlp_kernel(*refs, final_norm):
    oa_ref, ob_ref, mod_ref, modc_ref, g_ref, gf_ref, wo_ref, w1_ref, w2_ref, out_ref = refs[-10:]
    half = wo_ref.shape[0] // 2
    x = _stream_tile(refs[:-10])
    n = x.shape[0]
    mods = _mod_vectors(mod_ref, modc_ref, n, (2, 3, 4, 5), not final_norm)
    halves = (slice(0, n // 2), slice(n // 2, n))
    staged = []
    for r in halves:
        gate_mix, shift, scale, gate_mlp = [v if v.shape[0] == 1 else v[r] for v in mods]
        y = _mm(oa_ref[0, r, :], wo_ref[:half, :]) + _mm(ob_ref[0, r, :], wo_ref[half:, :])
        x1 = x[r] + gate_mix * y
        staged.append((x1, _norm_modulate(x1, g_ref[...], shift, scale).astype(BF16), gate_mlp))
    for r, (x1, h, gate_mlp) in zip(halves, staged):
        acc = jnp.zeros((n // 2, D_MODEL), F32)
        for c in range(D_FF // FF_CHUNK):
            t = jnp.maximum(_mm(h, w1_ref[:, c * FF_CHUNK:(c + 1) * FF_CHUNK]), 0.0)
            acc = acc + _mm((t * t).astype(BF16), w2_ref[c * FF_CHUNK:(c + 1) * FF_CHUNK, :])
        x2 = x1 + gate_mlp * acc
        if final_norm:
            x2 = x2 * lax.rsqrt(jnp.mean(x2 * x2, axis=-1, keepdims=True) + EPS) * gf_ref[...]
        out_ref[0, r, :] = x2


def _outproj_mlp(stream, oa, ob, ob_block, l, mods, gains, gain_final, wo, w1, w2):
    bsz = stream[0].shape[0]
    half = D_MODEL // 2
    final_norm = l == DEPTH - 1
    tm = TP_FINAL if final_norm else TP
    tiles = (SEQ if final_norm else TOK) // tm
    tile = lambda b, t: (b, t, 0)
    return pl.pallas_call(
        functools.partial(_mlp_kernel, final_norm=final_norm),
        grid=(bsz, tiles),
        in_specs=_stream_specs(stream, tm) + [
            pl.BlockSpec((1, tm, half), tile),
            pl.BlockSpec((1, tm, half), lambda b, t: (b, t, ob_block)),
        ] + _mod_specs(l) + [
            _layer_spec(l, 1, D_MODEL),
            pl.BlockSpec((1, D_MODEL), lambda b, t: (0, 0)),
            _layer_spec(l // 2, D_MODEL, D_MODEL),
            _layer_spec(l, D_MODEL, D_FF),
            _layer_spec(l, D_FF, D_MODEL),
        ],
        out_specs=pl.BlockSpec((1, tm, D_MODEL), tile),
        out_shape=jax.ShapeDtypeStruct((bsz, tiles * tm, D_MODEL), F32),
        compiler_params=_params("parallel", "parallel"),
    )(*_stream_args(stream, tm), oa, ob, mods, mods, gains, gain_final, wo, w1, w2)


def _even_in_weight(w):
    n_na = 3 * NA_WIDTH
    scale = np.ones((w.shape[-1],), np.float32)
    scale[:NA_WIDTH] = HEAD_DIM ** -0.5 * LOG2E
    scale[n_na:n_na + GLA_QK_WIDTH] = GLA_DK ** -0.5
    w = w * jnp.asarray(scale)
    pad = n_na + GLA_IN_WIDTH - w.shape[-1]
    return jnp.pad(w, ((0, 0), (0, 0), (0, pad))).astype(BF16)


def _odd_in_weight(w):
    n = w.shape[0]
    n_rot = D_MODEL + SWA_KV_WIDTH
    rot = w[..., :n_rot].reshape(n, D_MODEL, n_rot // HEAD_DIM, HEAD_DIM // 2, 2)
    rot = jnp.swapaxes(rot, 3, 4).reshape(n, D_MODEL, n_rot)
    q = _swa_head_order(rot[..., :D_MODEL] * (HEAD_DIM ** -0.5 * LOG2E), axis=2)
    return jnp.concatenate([q, rot[..., D_MODEL:], w[..., n_rot:]], axis=2).astype(BF16)


def _swa_head_order(a, axis):
    shape = a.shape
    split = shape[:axis] + (SWA_KV_HEADS // 2, 2, SWA_GROUP, HEAD_DIM) + shape[axis + 1:]
    return jnp.swapaxes(a.reshape(split), axis + 1, axis + 2).reshape(shape)


def kernel(x, c, ctx, c_ctx, ada_w, ada_b, norm_mix, norm_mlp, mlp_w1, mlp_w2, ab_w_in, ab_w_out, na_rpb,
           gla_wa2, gla_ba, gla_gnorm, swa_w_in, swa_w_out, swa_sink, norm_final):
    bsz = x.shape[0]
    assert x.shape == (bsz, SEQ, D_MODEL) and ctx.shape == (bsz, CTX_LEN, D_MODEL) and bsz <= 8

    cvec = jnp.zeros((MOD_ROWS, D_MODEL), F32).at[:bsz].set(c).at[8].set(c_ctx)
    mods = _ada_table(cvec, ada_w, ada_b).reshape(DEPTH, MOD_ROWS, 6, D_MODEL)
    cos_t, sin_t = _rope_tables()
    gain_final = norm_final.reshape(1, D_MODEL)
    g_mix = norm_mix.reshape(DEPTH, 1, D_MODEL)
    g_mlp = norm_mlp.reshape(DEPTH, 1, D_MODEL)
    w_in_even = _even_in_weight(ab_w_in)
    w_in_odd = _odd_in_weight(swa_w_in)
    w_out_even = ab_w_out.astype(BF16)
    w_out_odd = _swa_head_order(swa_w_out, axis=1).astype(BF16)
    w1 = mlp_w1.astype(BF16)
    w2 = mlp_w2.astype(BF16)
    na_bias = _na_bias_table(na_rpb)

    stream = (x, ctx)
    for l in range(DEPTH):
        j = l // 2
        if l % 2 == 0:
            na_in, gla_in = _inproj_even(stream, l, mods, g_mix, w_in_even)
            oa = _na_attention(na_in, na_bias[j])
            ob = _gla(gla_in, gla_wa2[j], gla_ba[j], gla_gnorm[j])
            ob_block = 0
            wo = w_out_even
        else:
            q, qr, kr, v = _inproj_odd(stream, l, mods, g_mix, w_in_odd, cos_t, sin_t)
            oa = ob = _swa_attention(q, qr, kr, v, swa_sink[j] * LOG2E)
            ob_block = 1
            wo = w_out_odd
        xs = _outproj_mlp(stream, oa, ob, ob_block, l, mods, g_mlp, gain_final, wo, w1, w2)
        stream = (xs,)
    return xs
```
